```python
import jax
import jax.numpy as jnp
from jax import lax
import numpy as np

D_MODEL = 1024
BATCH = 8
SEQ = 2048
DEPTH = 2

MEM_LEN = 256
N_ATT_HEADS = 8
ATT_HEAD_DIM = 64
ATT_WIDTH = N_ATT_HEADS * ATT_HEAD_DIM
N_RWKV_HEADS = 8
RWKV_HEAD_DIM = 64
RWKV_WIDTH = N_RWKV_HEADS * RWKV_HEAD_DIM
MIX_WIDTH = ATT_WIDTH + RWKV_WIDTH
MOBA_BLOCK = 256
MOBA_TOP_BLOCKS = 3
MOBA_Q_CHUNK = 16
DECAY_LORA = 64
ICLR_LORA = 64
VRES_LORA = 32
GATE_LORA = 128
RWKV_GN_EPS = 64e-5
N_MEM_HEADS = 4
MEM_HEAD_DIM = D_MODEL // N_MEM_HEADS
N_EXPERTS = 32
N_EXPERT_GROUPS = 8
EXPERTS_PER_GROUP = N_EXPERTS // N_EXPERT_GROUPS
TOP_K = 2
D_EXPERT = 512
MOE_ROW_BLOCK = 128
LN_EPS = 1e-5
RMS_EPS = 1e-6
NEG_INF = -1e30
DEEPNORM_ALPHA = (2 * DEPTH) ** 0.25
DEEPNORM_BETA = (8 * DEPTH) ** -0.25
SHIFT_COLS = 3 * RWKV_WIDTH + DECAY_LORA + ICLR_LORA + GATE_LORA
IN_COLS = 3 * ATT_WIDTH + SHIFT_COLS
RWKV_SPLITS = (RWKV_WIDTH, 2 * RWKV_WIDTH, 3 * RWKV_WIDTH, 3 * RWKV_WIDTH + DECAY_LORA, 3 * RWKV_WIDTH + DECAY_LORA + ICLR_LORA)

kernel_name = 'hybrid_moba_rwkv7_moe_deepnorm'


def layer_norm(x, g, b):
    xf = x.astype(jnp.float32)
    mu = jnp.mean(xf, -1, keepdims=True)
    var = jnp.mean(jnp.square(xf - mu), -1, keepdims=True)
    return ((xf - mu) * lax.rsqrt(var + LN_EPS) * g + b).astype(x.dtype)


def alibi_slopes(n_heads):
    return 2.0 ** (-8.0 * jnp.arange(1, n_heads + 1, dtype=jnp.float32) / n_heads)


def token_shift(z):
    return jnp.pad(z, ((0, 0), (1, 0), (0, 0)))[:, :-1]


def moba_attention(q, k, v):
    B, S, H, hd = q.shape
    nb = -(-S // MOBA_BLOCK)
    s_pad = nb * MOBA_BLOCK
    ksel = max(1, min(MOBA_TOP_BLOCKS, nb - 1))
    pad = ((0, 0), (0, 0), (0, s_pad - S), (0, 0))
    qh = jnp.pad(q.transpose(0, 2, 1, 3), pad) * hd ** -0.5
    kh = jnp.pad(k.transpose(0, 2, 1, 3), pad)
    vh = jnp.pad(v.transpose(0, 2, 1, 3), pad)
    kb = kh.reshape(B, H, nb, MOBA_BLOCK, hd)
    vb = vh.reshape(B, H, nb, MOBA_BLOCK, hd)
    kmean = jnp.mean(kb, axis=3)
    slopes = alibi_slopes(H)[None, :, None, None]
    bi = jnp.arange(B)[:, None, None, None]
    hi = jnp.arange(H)[None, :, None, None]
    blk_ids = jnp.arange(nb)
    offs = jnp.arange(MOBA_BLOCK)
    n_sel = ksel * MOBA_BLOCK

    def chunk(start):
        cur = start // MOBA_BLOCK
        pos_q = start + jnp.arange(MOBA_Q_CHUNK)
        qc = lax.dynamic_slice(qh, (0, 0, start, 0), (B, H, MOBA_Q_CHUNK, hd))
        gate = jnp.einsum('bhqd,bhnd->bhqn', qc, kmean)
        gate = jnp.where(blk_ids < cur, gate, NEG_INF)
        _, idx = lax.top_k(gate, ksel)
        valid = jnp.arange(ksel) < cur
        k_sel = kb[bi, hi, idx]
        v_sel = vb[bi, hi, idx]
        s_sel = jnp.einsum('bhqd,bhqnld->bhqnl', qc, k_sel)
        dist_sel = pos_q[:, None, None] - (idx[..., None] * MOBA_BLOCK + offs)
        s_sel = jnp.where(valid[:, None], s_sel - slopes[..., None] * dist_sel, NEG_INF)
        k_own = lax.dynamic_slice(kh, (0, 0, cur * MOBA_BLOCK, 0), (B, H, MOBA_BLOCK, hd))
        v_own = lax.dynamic_slice(vh, (0, 0, cur * MOBA_BLOCK, 0), (B, H, MOBA_BLOCK, hd))
        dist_own = pos_q[:, None] - (cur * MOBA_BLOCK + offs)[None, :]
        s_own = jnp.einsum('bhqd,bhld->bhql', qc, k_own)
        s_own = jnp.where(dist_own >= 0, s_own - slopes * dist_own, NEG_INF)
        logits = jnp.concatenate([s_sel.reshape(B, H, MOBA_Q_CHUNK, n_sel), s_own], axis=-1)
        p = jax.nn.softmax(logits, axis=-1)
        p_sel = p[..., :n_sel].reshape(B, H, MOBA_Q_CHUNK, ksel, MOBA_BLOCK)
        p_own = p[..., n_sel:]
        return (jnp.einsum('bhqnl,bhqnld->bhqd', p_sel, v_sel)
                + jnp.einsum('bhql,bhld->bhqd', p_own, v_own))

    starts = jnp.arange(0, s_pad, MOBA_Q_CHUNK)
    out = lax.map(chunk, starts)
    out = out.transpose(1, 2, 0, 3, 4).reshape(B, H, s_pad, hd)[:, :, :S]
    return out.transpose(0, 2, 1, 3)


def rwkv7_recurrence(r, w, k, v, a, b):
    B, S, H, hd = r.shape

    def step(state, inp):
        r_t, w_t, k_t, v_t, a_t, b_t = inp
        sa = jnp.einsum('bhij,bhj->bhi', state, a_t)
        state = (state * w_t[:, :, None, :] + sa[..., None] * b_t[:, :, None, :]
                 + v_t[..., None] * k_t[:, :, None, :])
        return state, jnp.einsum('bhij,bhj->bhi', state, r_t)

    xs = tuple(t.transpose(1, 0, 2, 3) for t in (r, w, k, v, a, b))
    _, y = lax.scan(step, jnp.zeros((B, H, hd, hd), jnp.float32), xs)
    return y.transpose(1, 0, 2, 3)


def rwkv7_time_mix(r, k, v, zw, za, zg, w0, w_decay_up, a0, w_iclr_up, w_gate_up,
                   k_k, k_a, r_k, lnx_g, lnx_b):
    B, S, _ = r.shape
    H, hd = N_RWKV_HEADS, RWKV_HEAD_DIM
    log_w = -jax.nn.softplus(-(w0 + jnp.tanh(zw) @ w_decay_up)) - 0.5
    decay = jnp.exp(-jnp.exp(log_w))
    a = jax.nn.sigmoid(a0 + za @ w_iclr_up)
    g = jax.nn.sigmoid(zg) @ w_gate_up
    heads = lambda t: t.reshape(B, S, H, hd)
    kk = heads(k * k_k)
    kk = kk * lax.rsqrt(jnp.maximum(jnp.sum(kk * kk, -1, keepdims=True), 1e-24))
    k = heads(k * (1.0 + (a - 1.0) * k_a))
    r, v, a = heads(r), heads(v), heads(a)
    y = rwkv7_recurrence(r, heads(decay), k, v, -kk, kk * a)
    mu = jnp.mean(y, -1, keepdims=True)
    var = jnp.mean(jnp.square(y - mu), -1, keepdims=True)
    y = ((y - mu) * lax.rsqrt(var + RWKV_GN_EPS)).reshape(B, S, RWKV_WIDTH) * lnx_g + lnx_b
    bonus = (jnp.sum(r * k * r_k, -1, keepdims=True) * v).reshape(B, S, RWKV_WIDTH)
    return (y + bonus) * g


def memory_cross_attention(x, mem, w_q, w_kv, w_o):
    B, S, D = x.shape
    q = (x @ w_q).reshape(B, S, N_MEM_HEADS, MEM_HEAD_DIM).astype(jnp.float32)
    kv = (mem @ w_kv).reshape(B, mem.shape[1], 2, N_MEM_HEADS, MEM_HEAD_DIM).astype(jnp.float32)
    s = jnp.einsum('bshd,bmhd->bhsm', q, kv[:, :, 0]) * MEM_HEAD_DIM ** -0.5
    p = jax.nn.softmax(s, axis=-1)
    o = jnp.einsum('bhsm,bmhd->bshd', p, kv[:, :, 1]).reshape(B, S, D).astype(x.dtype)
    return o @ w_o


def route(xf, w_router, b_router):
    s = jax.nn.sigmoid((xf @ w_router).astype(jnp.float32))
    sel = s + b_router.astype(jnp.float32)
    grp = jnp.sum(lax.top_k(sel.reshape(-1, N_EXPERT_GROUPS, EXPERTS_PER_GROUP), 2)[0], -1)
    g_best = jnp.argmax(grp, axis=-1)
    in_grp = (jnp.arange(N_EXPERTS) // EXPERTS_PER_GROUP)[None, :] == g_best[:, None]
    _, e_idx = lax.top_k(jnp.where(in_grp, sel, NEG_INF), TOP_K)
    w = jnp.take_along_axis(s, e_idx, axis=-1)
    return e_idx, w / jnp.sum(w, -1, keepdims=True)


def moe_ffn(xf, e_idx, gate, w_gate, w_up, w_down):
    N, D = xf.shape
    A = N * TOP_K
    eid = e_idx.reshape(-1)
    tok = jnp.repeat(jnp.arange(N), TOP_K)
    gt = gate.reshape(-1)
    order = jnp.argsort(eid)
    eid_s, tok_s, gt_s = eid[order], tok[order], gt[order]
    counts = jnp.bincount(eid, length=N_EXPERTS)
    off = jnp.cumsum(counts) - counts
    pcounts = (counts + MOE_ROW_BLOCK - 1) // MOE_ROW_BLOCK * MOE_ROW_BLOCK
    pend = jnp.cumsum(pcounts)
    poff = pend - pcounts
    dest = poff[eid_s] + (jnp.arange(A) - off[eid_s])
    M = A + N_EXPERTS * MOE_ROW_BLOCK
    n_blk = M // MOE_ROW_BLOCK
    row_tok = jnp.full((M,), N, jnp.int32).at[dest].set(tok_s.astype(jnp.int32))
    row_gate = jnp.zeros((M,), xf.dtype).at[dest].set(gt_s)
    blk_expert = jnp.minimum(jnp.searchsorted(pend, jnp.arange(n_blk) * MOE_ROW_BLOCK, side='right'), N_EXPERTS - 1)
    x_rows = jnp.concatenate([xf, jnp.zeros((1, D), xf.dtype)], 0)[row_tok]
    x_rows = x_rows.reshape(n_blk, MOE_ROW_BLOCK, D)

    def expert_block(args):
        xb, e = args
        h = jax.nn.silu(xb @ w_gate[e]) * (xb @ w_up[e])
        return h @ w_down[e]

    y = lax.map(expert_block, (x_rows, blk_expert)).reshape(M, D)
    return jax.ops.segment_sum(y * row_gate[:, None], row_tok, num_segments=N + 1)[:N]


def setup_inputs(seed: int = 0) -> dict:
    key = jax.random.key(seed)
    ks = iter(jax.random.split(key, 40))
    f32 = jnp.float32

    def nrm(shape, scale):
        return jax.random.normal(next(ks), shape, f32) * scale

    def unif(shape, lo, hi):
        return jax.random.uniform(next(ks), shape, f32, lo, hi)

    beta = DEEPNORM_BETA
    rv = 3 * ATT_WIDTH + 2 * RWKV_WIDTH
    col_scale = (jnp.ones((IN_COLS,), f32).at[2 * ATT_WIDTH:3 * ATT_WIDTH].set(beta)
                 .at[rv:rv + RWKV_WIDTH].set(beta))
    kv_scale = jnp.concatenate([jnp.ones((D_MODEL,), f32), jnp.full((D_MODEL,), beta, f32)])
    dm1 = max(DEPTH - 1, 0)
    return {
        'x': nrm((BATCH, SEQ, D_MODEL), 1.0),
        'mem': nrm((BATCH, MEM_LEN, D_MODEL), 1.0),
        'w_in': nrm((DEPTH, D_MODEL, IN_COLS), D_MODEL ** -0.5) * col_scale,
        'w_in_vres': nrm((dm1, D_MODEL, VRES_LORA), D_MODEL ** -0.5),
        'mu_shift': unif((DEPTH, SHIFT_COLS), 0.0, 1.0),
        'mu_vres': unif((dm1, VRES_LORA), 0.0, 1.0),
        'w0': unif((DEPTH, RWKV_WIDTH), -3.0, 0.0),
        'w_decay_up': nrm((DEPTH, DECAY_LORA, RWKV_WIDTH), 0.5 * DECAY_LORA ** -0.5),
        'a0': nrm((DEPTH, RWKV_WIDTH), 0.1),
        'w_iclr_up': nrm((DEPTH, ICLR_LORA, RWKV_WIDTH), ICLR_LORA ** -0.5),
        'v0': nrm((dm1, RWKV_WIDTH), 0.1),
        'w_vres_up': nrm((dm1, VRES_LORA, RWKV_WIDTH), VRES_LORA ** -0.5),
        'w_gate_up': nrm((DEPTH, GATE_LORA, RWKV_WIDTH), GATE_LORA ** -0.5),
        'k_k': 0.85 + nrm((DEPTH, RWKV_WIDTH), 0.05),
        'k_a': 1.0 + nrm((DEPTH, RWKV_WIDTH), 0.05),
        'r_k': nrm((DEPTH, N_RWKV_HEADS, RWKV_HEAD_DIM), 0.1),
        'lnx_g': 1.0 + nrm((DEPTH, RWKV_WIDTH), 0.05),
        'lnx_b': nrm((DEPTH, RWKV_WIDTH), 0.02),
        'attn_out_g': 1.0 + nrm((DEPTH, ATT_WIDTH), 0.05),
        'w_out': nrm((DEPTH, MIX_WIDTH, D_MODEL), beta * MIX_WIDTH ** -0.5),
        'w_mem_q': nrm((DEPTH, D_MODEL, D_MODEL), D_MODEL ** -0.5),
        'w_mem_kv': nrm((DEPTH, D_MODEL, 2 * D_MODEL), D_MODEL ** -0.5) * kv_scale,
        'w_mem_o': nrm((DEPTH, D_MODEL, D_MODEL), beta * D_MODEL ** -0.5),
        'w_router': nrm((D_MODEL, N_EXPERTS), D_MODEL ** -0.5),
        'b_router': nrm((N_EXPERTS,), 0.01),
        'w_exp_gate': nrm((DEPTH, N_EXPERTS, D_MODEL, D_EXPERT), D_MODEL ** -0.5),
        'w_exp_up': nrm((DEPTH, N_EXPERTS, D_MODEL, D_EXPERT), D_MODEL ** -0.5),
        'w_exp_down': nrm((DEPTH, N_EXPERTS, D_EXPERT, D_MODEL), beta * D_EXPERT ** -0.5),
        'ln_g': 1.0 + nrm((DEPTH, 3, D_MODEL), 0.05),
        'ln_b': nrm((DEPTH, 3, D_MODEL), 0.02),
    }


def reference(x, mem, w_in, w_in_vres, mu_shift, mu_vres, w0, w_decay_up, a0, w_iclr_up,
              v0, w_vres_up, w_gate_up, k_k, k_a, r_k, lnx_g, lnx_b, attn_out_g, w_out,
              w_mem_q, w_mem_kv, w_mem_o, w_router, b_router, w_exp_gate, w_exp_up,
              w_exp_down, ln_g, ln_b):
    B, S, D = x.shape
    f32 = jnp.float32
    v_first = None
    for l in range(DEPTH):
        w_comb = w_in[l] if l == 0 else jnp.concatenate([w_in[l], w_in_vres[l - 1]], axis=1)
        z = x @ w_comb
        att = z[..., :3 * ATT_WIDTH].astype(f32).reshape(B, S, 3, N_ATT_HEADS, ATT_HEAD_DIM)
        o_att = moba_attention(att[:, :, 0], att[:, :, 1], att[:, :, 2])
        o_att = (o_att * lax.rsqrt(jnp.mean(o_att * o_att, -1, keepdims=True) + RMS_EPS)
                 * attn_out_g[l].reshape(N_ATT_HEADS, ATT_HEAD_DIM)).reshape(B, S, ATT_WIDTH)
        zs = z[..., 3 * ATT_WIDTH:3 * ATT_WIDTH + SHIFT_COLS].astype(f32)
        zs = zs + (token_shift(zs) - zs) * mu_shift[l]
        r, kr, vr, zw, za, zg = jnp.split(zs, RWKV_SPLITS, axis=-1)
        if l == 0:
            v_first = vr
        else:
            zv = z[..., 3 * ATT_WIDTH + SHIFT_COLS:].astype(f32)
            zv = zv + (token_shift(zv) - zv) * mu_vres[l - 1]
            vr = vr + (v_first - vr) * jax.nn.sigmoid(v0[l - 1] + zv @ w_vres_up[l - 1])
        o_rwkv = rwkv7_time_mix(r, kr, vr, zw, za, zg, w0[l], w_decay_up[l], a0[l], w_iclr_up[l],
                                w_gate_up[l], k_k[l], k_a[l], r_k[l], lnx_g[l], lnx_b[l])
        mix = jnp.concatenate([o_att, o_rwkv], axis=-1).astype(x.dtype)
        x = layer_norm(DEEPNORM_ALPHA * x + mix @ w_out[l], ln_g[l, 0], ln_b[l, 0])
        x = layer_norm(DEEPNORM_ALPHA * x + memory_cross_attention(x, mem, w_mem_q[l], w_mem_kv[l], w_mem_o[l]),
                       ln_g[l, 1], ln_b[l, 1])
        xf = x.reshape(B * S, D)
        e_idx, gate = route(xf, w_router, b_router)
        y = moe_ffn(xf, e_idx, gate.astype(x.dtype), w_exp_gate[l], w_exp_up[l], w_exp_down[l])
        x = layer_norm(DEEPNORM_ALPHA * x + y.reshape(B, S, D), ln_g[l, 2], ln_b[l, 2])
    return x
```

```python
import functools

import jax
import jax.numpy as jnp
from jax import lax
from jax.experimental import pallas as pl
from jax.experimental.pallas import tpu as pltpu

F32 = jnp.float32
BF16 = jnp.bfloat16
I32 = jnp.int32
HIGHEST = lax.Precision.HIGHEST

D_MODEL = 1024
DEPTH = 2
N_HEADS = 8
HEAD_DIM = 64
WIDTH = N_HEADS * HEAD_DIM
N_PAIRS = N_HEADS // 2
LANES = 128
MOBA_BLOCK = 256
MOBA_TOP = 3
LORA_COLS = 384
IN_COLS_PAD = 3 * WIDTH + 3 * WIDTH + LORA_COLS
RWKV_GN_EPS = 64e-5
N_MEM_HEADS = 4
MEM_HEAD_DIM = D_MODEL // N_MEM_HEADS
N_EXPERTS = 32
N_GROUPS = 8
D_EXPERT = 512
LN_EPS = 1e-5
RMS_EPS = 1e-6
NEG = -1e30
ALPHA = (2 * DEPTH) ** 0.25
CHUNK = 64
MOE_ROWS = 256

VMEM_LIMIT = 48 * 1024 * 1024

NT_DIMS = (((1,), (1,)), ((), ()))
TN_DIMS = (((0,), (0,)), ((), ()))


def _cparams(sem):
    return pltpu.CompilerParams(dimension_semantics=sem, vmem_limit_bytes=VMEM_LIMIT)


def _split(a):
    hi = a.astype(BF16)
    lo = (a - hi.astype(F32)).astype(BF16)
    return hi, lo


def _dot3(a, b, dims=(((1,), (0,)), ((), ()))):
    ah, al = _split(a)
    bh, bl = _split(b)
    d = functools.partial(lax.dot_general, dimension_numbers=dims, preferred_element_type=F32)
    return d(ah, bh) + (d(al, bh) + d(ah, bl))


def _dot1(a, b, dims=(((1,), (0,)), ((), ()))):
    return lax.dot_general(a.astype(BF16), b.astype(BF16), dims, preferred_element_type=F32)


def _layer_norm(y, g, b):
    mu = jnp.mean(y, -1, keepdims=True)
    d = y - mu
    var = jnp.mean(d * d, -1, keepdims=True)
    return d * lax.rsqrt(var + LN_EPS) * g + b


def _sigmoid(x):
    return 1.0 / (1.0 + jnp.exp(-x))


def _inproj_kernel(x_ref, w_ref, att_ref, rkv_ref, lora_ref):
    xb = x_ref[...].astype(BF16)
    att_ref[...] = jnp.dot(xb, w_ref[:, 0:3 * WIDTH], preferred_element_type=F32)
    rkv_ref[...] = jnp.dot(xb, w_ref[:, 3 * WIDTH:6 * WIDTH], preferred_element_type=F32)
    lora_ref[...] = jnp.dot(xb, w_ref[:, 6 * WIDTH:IN_COLS_PAD], preferred_element_type=F32)


def _inproj(x2d, w_pad, tm=256):
    n = x2d.shape[0]
    return pl.pallas_call(
        _inproj_kernel,
        grid=(n // tm,),
        in_specs=[pl.BlockSpec((tm, D_MODEL), lambda i: (i, 0)),
                  pl.BlockSpec((D_MODEL, IN_COLS_PAD), lambda i: (0, 0))],
        out_specs=[pl.BlockSpec((tm, 3 * WIDTH), lambda i: (i, 0)),
                   pl.BlockSpec((tm, 3 * WIDTH), lambda i: (i, 0)),
                   pl.BlockSpec((tm, LORA_COLS), lambda i: (i, 0))],
        out_shape=[jax.ShapeDtypeStruct((n, 3 * WIDTH), F32),
                   jax.ShapeDtypeStruct((n, 3 * WIDTH), F32),
                   jax.ShapeDtypeStruct((n, LORA_COLS), F32)],
        compiler_params=_cparams(("parallel",)),
        name="inproj",
    )(x2d, w_pad)


def _moba_kernel(slopes_ref, q_ref, k_ref, v_ref, gain_ref, o_ref, kaug_ref, vb_ref, kmp_ref, *, seq):
    pair = pl.program_id(1)
    i = pl.program_id(2)
    nb = seq // MOBA_BLOCK
    blk = MOBA_BLOCK

    @pl.when(i == 0)
    def _():
        k = k_ref[...]
        rowblk = lax.shift_right_logical(lax.broadcasted_iota(I32, (seq, LANES), 0), 8)
        lane_s = lax.broadcasted_iota(I32, (seq, LANES), 1)
        r_i = lax.broadcasted_iota(I32, (LANES, seq), 0)
        c_b = lax.shift_right_logical(lax.broadcasted_iota(I32, (LANES, seq), 1), 8)
        for hh in range(2):
            ob = HEAD_DIM * (1 - hh)
            inhead = (lane_s >= HEAD_DIM * hh) & (lane_s < HEAD_DIM * (hh + 1))
            ind = (lane_s - ob) == rowblk
            kaug_ref[hh] = jnp.where(inhead, k, jnp.where(ind, 1.0, 0.0)).astype(BF16)
            indm = jnp.where((r_i - ob) == c_b, 1.0 / blk, 0.0)
            kmp_ref[hh] = jnp.dot(indm, k, precision=HIGHEST, preferred_element_type=F32)
        vb_ref[...] = v_ref[...].astype(BF16)

    q = q_ref[...] * (HEAD_DIM ** -0.5)
    row = lax.broadcasted_iota(I32, (blk, blk), 0)
    col = lax.broadcasted_iota(I32, (blk, blk), 1)
    rc = (row - col).astype(F32)
    lane = lax.broadcasted_iota(I32, (blk, LANES), 1)
    gain = gain_ref[...]
    outs = []
    for hh in range(2):
        slope = slopes_ref[2 * pair + hh]
        ob = HEAD_DIM * (1 - hh)
        inhead = (lane >= HEAD_DIM * hh) & (lane < HEAD_DIM * (hh + 1))
        qm = jnp.where(inhead, q, 0.0)
        gate = lax.dot_general(qm, kmp_ref[hh], NT_DIMS, precision=HIGHEST, preferred_element_type=F32)
        jl = lane - ob
        cnt = jnp.zeros((blk, LANES), I32)
        for jp in range(nb):
            colv = gate[:, ob + jp:ob + jp + 1]
            beats = (colv > gate) | ((colv == gate) & (jp < jl))
            cnt = cnt + jnp.where(beats & (jp < i), 1, 0)
        past = (jl >= 0) & (jl < i)
        sel = (cnt < MOBA_TOP) & past
        bias = jnp.where(sel, (-slope * blk) * (i - jl).astype(F32), NEG)
        bias = jnp.where(past, bias, 0.0)
        qa = jnp.where(inhead, q, bias).astype(BF16)

        t_off = -slope * rc
        t_own = jnp.where(rc >= 0, t_off, NEG)

        def scores(j, tmat):
            kb = kaug_ref[hh, pl.ds(pl.multiple_of(j * blk, blk), blk), :]
            return lax.dot_general(qa, kb, NT_DIMS, preferred_element_type=F32) + tmat

        def pv(p, j):
            vb = vb_ref[pl.ds(pl.multiple_of(j * blk, blk), blk), :]
            return jnp.dot(p.astype(BF16), vb, preferred_element_type=F32)

        s = scores(i, t_own)
        m = jnp.max(s, -1, keepdims=True)
        p = jnp.exp(s - m)
        l = jnp.sum(p, -1, keepdims=True)
        acc = pv(p, i)

        def body(j, carry):
            m, l, acc = carry
            s = scores(j, t_off)
            m_new = jnp.maximum(m, jnp.max(s, -1, keepdims=True))
            alpha = jnp.exp(m - m_new)
            p = jnp.exp(s - m_new)
            l = alpha * l + jnp.sum(p, -1, keepdims=True)
            acc = alpha * acc + pv(p, j)
            return m_new, l, acc

        m, l, acc = lax.fori_loop(0, i, body, (m, l, acc))
        out = acc / l
        ms = jnp.sum(jnp.where(inhead, out * out, 0.0), -1, keepdims=True) * (1.0 / HEAD_DIM)
        outs.append(out * lax.rsqrt(ms + RMS_EPS) * gain)
    o_ref[...] = jnp.where(lane < HEAD_DIM, outs[0], outs[1])


def _moba(z_att, slopes, gain, batch, seq):
    nb = seq // MOBA_BLOCK
    grid_spec = pltpu.PrefetchScalarGridSpec(
        num_scalar_prefetch=1,
        grid=(batch, N_PAIRS, nb),
        in_specs=[pl.BlockSpec((None, MOBA_BLOCK, LANES), lambda b, p, i, s: (b, i, p)),
                  pl.BlockSpec((None, seq, LANES), lambda b, p, i, s: (b, 0, N_PAIRS + p)),
                  pl.BlockSpec((None, seq, LANES), lambda b, p, i, s: (b, 0, 2 * N_PAIRS + p)),
                  pl.BlockSpec((1, LANES), lambda b, p, i, s: (0, p))],
        out_specs=pl.BlockSpec((None, MOBA_BLOCK, LANES), lambda b, p, i, s: (b, i, p)),
        scratch_shapes=[pltpu.VMEM((2, seq, LANES), BF16),
                        pltpu.VMEM((seq, LANES), BF16),
                        pltpu.VMEM((2, LANES, LANES), F32)],
    )
    return pl.pallas_call(
        functools.partial(_moba_kernel, seq=seq),
        grid_spec=grid_spec,
        out_shape=jax.ShapeDtypeStruct((batch, seq, WIDTH), F32),
        compiler_params=_cparams(("parallel", "parallel", "arbitrary")),
        name="moba",
    )(slopes, z_att, z_att, z_att, gain)


def _dot2x(a, b_exact):
    ah, al = _split(a)
    return (jnp.dot(ah, b_exact, preferred_element_type=F32)
            + jnp.dot(al, b_exact, preferred_element_type=F32))


def _head_block_diag(scale):
    r = lax.shift_right_logical(lax.broadcasted_iota(I32, (LANES, LANES), 0), 6)
    c = lax.shift_right_logical(lax.broadcasted_iota(I32, (LANES, LANES), 1), 6)
    return jnp.where(r == c, scale, 0.0).astype(BF16)


def _softplus(x):
    return jnp.maximum(x, 0.0) + jnp.log(1.0 + jnp.exp(-jnp.abs(x)))


def _rwkv_chunk_pair(lw, r, k2, v, a, b):
    c_ = CHUNK
    row = lax.broadcasted_iota(I32, (c_, c_), 0)
    col = lax.broadcasted_iota(I32, (c_, c_), 1)
    incl = row >= col
    strict = row > col
    lane = lax.broadcasted_iota(I32, (c_, LANES), 1)
    m0 = lane < HEAD_DIM
    tri = jnp.where(incl, 1.0, 0.0).astype(BF16)
    lw_hi, lw_lo = _split(lw)
    c = (jnp.dot(tri, lw_hi, preferred_element_type=F32) + jnp.dot(tri, lw_lo, preferred_element_type=F32))
    e_pos = jnp.exp(c)
    e_neg = jnp.exp(-c)
    at = a * jnp.exp(c - lw)
    bt = b * e_neg
    kt = k2 * e_neg
    rt = r * e_pos
    gc = e_pos[c_ - 1:c_, :]
    bp = bt * gc
    kp = kt * gc
    vs = jnp.concatenate([jnp.where(m0, v, 0.0), jnp.where(m0, 0.0, v)], axis=0)
    eye = jnp.where(row == col, 1.0, 0.0)
    ts, laks, qbs, qks = [], [], [], []
    for hh in range(2):
        mh = m0 if hh == 0 else jnp.logical_not(m0)
        am = jnp.where(mh, at, 0.0)
        rm = jnp.where(mh, rt, 0.0)
        lab = jnp.where(strict, _dot3(am, bt, NT_DIMS), 0.0)
        laks.append(jnp.where(strict, _dot3(am, kt, NT_DIMS), 0.0))
        qbs.append(jnp.where(incl, _dot3(rm, bt, NT_DIMS), 0.0))
        qks.append(jnp.where(incl, _dot3(rm, kt, NT_DIMS), 0.0))
        t = eye + lab
        lp = lab
        for _ in range(5):
            lp = _dot3(lp, lp)
            t = t + _dot3(t, lp)
        ts.append(t)
    tcat = jnp.concatenate(ts, axis=1)
    qcat = jnp.concatenate(qbs, axis=1)
    p = _dot3(jnp.concatenate(laks, axis=1), vs)
    yk = _dot3(jnp.concatenate(qks, axis=1), vs)
    grow = jnp.broadcast_to(gc, (c_, LANES))
    return at, p, tcat, bp, kp, grow, rt, qcat, yk


def _rwkv_pre_kernel(*refs, has_vres, tb):
    if has_vres:
        (rkv_ref, lora_ref, vf_ref, mu_rkv_ref, mu_lora_ref, vec_ref, wd_ref, wa_ref, wg_ref, wv_ref,
         at_ref, p_ref, tc_ref, bp_ref, kp_ref, v_ref, gr_ref, rt_ref, qc_ref, yk_ref, bonus_ref, g_ref,
         carry_ref, carryl_ref) = refs
    else:
        (rkv_ref, lora_ref, mu_rkv_ref, mu_lora_ref, vec_ref, wd_ref, wa_ref, wg_ref,
         at_ref, p_ref, tc_ref, bp_ref, kp_ref, v_ref, gr_ref, rt_ref, qc_ref, yk_ref, bonus_ref, g_ref,
         carry_ref, carryl_ref) = refs
    t_idx = pl.program_id(1)

    @pl.when(t_idx == 0)
    def _():
        carry_ref[...] = jnp.zeros_like(carry_ref)
        carryl_ref[...] = jnp.zeros_like(carryl_ref)

    def token_shift(z, cref, mu):
        first = lax.broadcasted_iota(I32, z.shape, 0) == 0
        prev = jnp.where(first, cref[0:1, :], pltpu.roll(z, 1, 0))
        cref[0:1, :] = z[tb - 1:tb, :]
        return z + (prev - z) * mu

    zs = token_shift(rkv_ref[...], carry_ref, mu_rkv_ref[...])
    zl = token_shift(lora_ref[...], carryl_ref, mu_lora_ref[...])
    l0 = zl[:, 0:LANES]
    w0, a0, v0 = vec_ref[0:1, :], vec_ref[1:2, :], vec_ref[2:3, :]
    k_k, k_a, r_k = vec_ref[3:4, :], vec_ref[4:5, :], vec_ref[5:6, :]
    dw = _dot1(jnp.tanh(l0), wd_ref[...])
    lw_all = -jnp.exp(-_softplus(-(w0 + dw)) - 0.5)
    a_lr = _sigmoid(a0 + _dot1(l0, wa_ref[...]))
    g_ref[...] = _dot1(_sigmoid(zl[:, LANES:2 * LANES]), wg_ref[...])
    r_all = zs[:, 0:WIDTH]
    k_all = zs[:, WIDTH:2 * WIDTH]
    v_all = zs[:, 2 * WIDTH:3 * WIDTH]
    if has_vres:
        mix = _sigmoid(v0 + _dot1(zl[:, 2 * LANES:3 * LANES], wv_ref[...]))
        v_all = v_all + (vf_ref[...] - v_all) * mix
    v_ref[...] = v_all
    bd = _head_block_diag(1.0)
    for pp in range(N_PAIRS):
        sl = slice(pp * LANES, (pp + 1) * LANES)
        r, k, v = r_all[:, sl], k_all[:, sl], v_all[:, sl]
        alr = a_lr[:, sl]
        kk = k * k_k[:, sl]
        ss = _dot2x(kk * kk, bd)
        kk = kk * lax.rsqrt(jnp.maximum(ss, 1e-24))
        k2 = k * (1.0 + (alr - 1.0) * k_a[:, sl])
        bonus_ref[:, sl] = _dot2x(r * k2 * r_k[:, sl], bd) * v
        a = -kk
        b = kk * alr
        lw = lw_all[:, sl]
        for ci in range(tb // CHUNK):
            rs = slice(ci * CHUNK, (ci + 1) * CHUNK)
            outs = _rwkv_chunk_pair(lw[rs], r[rs], k2[rs], v[rs], a[rs], b[rs])
            for ref, val in zip((at_ref, p_ref, tc_ref, bp_ref, kp_ref, gr_ref, rt_ref, qc_ref, yk_ref), outs):
                ref[rs, sl] = val


def _rwkv_pre(z_rkv, z_lora, v_first, mu_rkv, mu_lora, vecs, wd, wa, wg, wv, batch, seq, tb=128):
    has_vres = v_first is not None
    big = lambda w: pl.BlockSpec((None, tb, w), lambda b, t: (b, t, 0))
    full = lambda shape: pl.BlockSpec(shape, lambda b, t: (0,) * len(shape))
    in_specs = [big(3 * WIDTH), big(LORA_COLS)] + ([big(WIDTH)] if has_vres else [])
    in_specs += [full((1, 3 * WIDTH)), full((1, LORA_COLS)), full((8, WIDTH)),
                 full((LANES, WIDTH)), full((LANES, WIDTH)), full((LANES, WIDTH))]
    args = [z_rkv, z_lora] + ([v_first] if has_vres else []) + [mu_rkv, mu_lora, vecs, wd, wa, wg]
    if has_vres:
        in_specs.append(full((LANES, WIDTH)))
        args.append(wv)
    n_out = 12
    return pl.pallas_call(
        functools.partial(_rwkv_pre_kernel, has_vres=has_vres, tb=tb),
        grid=(batch, seq // tb),
        in_specs=in_specs,
        out_specs=[big(WIDTH)] * n_out,
        out_shape=[jax.ShapeDtypeStruct((batch, seq, WIDTH), F32)] * n_out,
        scratch_shapes=[pltpu.VMEM((8, 3 * WIDTH), F32), pltpu.VMEM((8, LORA_COLS), F32)],
        compiler_params=_cparams(("parallel", "arbitrary")),
        name="rwkv_pre",
    )(*args)


def _rwkv_scan_kernel(at_ref, p_ref, tc_ref, bp_ref, kp_ref, v_ref, gr_ref, rt_ref, qc_ref, yk_ref,
                      bonus_ref, g_ref, lnx_ref, o_ref, st_ref, *, ts):
    t_idx = pl.program_id(1)

    @pl.when(t_idx == 0)
    def _():
        st_ref[...] = jnp.zeros_like(st_ref)

    lane = lax.broadcasted_iota(I32, (CHUNK, LANES), 1)
    m0 = lane < HEAD_DIM
    bdmask = (lax.shift_right_logical(lax.broadcasted_iota(I32, (LANES, LANES), 0), 6)
              == lax.shift_right_logical(lax.broadcasted_iota(I32, (LANES, LANES), 1), 6))
    ones_t = jnp.full((CHUNK, LANES), 1.0 / CHUNK, BF16)

    def stack(x):
        return jnp.concatenate([jnp.where(m0, x, 0.0), jnp.where(m0, 0.0, x)], axis=0)

    states = [st_ref[pp] for pp in range(N_PAIRS)]
    for ci in range(ts // CHUNK):
        rs = slice(ci * CHUNK, (ci + 1) * CHUNK)
        for pp in range(N_PAIRS):
            sl = slice(pp * LANES, (pp + 1) * LANES)
            st = states[pp]
            x = _dot3(at_ref[rs, sl], st) + p_ref[rs, sl]
            u = _dot3(tc_ref[rs, sl], stack(x))
            o_ref[rs, sl] = _dot3(rt_ref[rs, sl], st) + _dot3(qc_ref[rs, sl], stack(u)) + yk_ref[rs, sl]
            gh, gl = _split(gr_ref[rs, sl])
            gl2 = (gr_ref[rs, sl] - gh.astype(F32) - gl.astype(F32)).astype(BF16)
            tn = functools.partial(lax.dot_general, dimension_numbers=TN_DIMS, preferred_element_type=F32)
            gcol = tn(gh, ones_t) + tn(gl, ones_t) + tn(gl2, ones_t)
            upd = _dot3(bp_ref[rs, sl], u, TN_DIMS) + _dot3(kp_ref[rs, sl], v_ref[rs, sl], TN_DIMS)
            states[pp] = gcol * st + jnp.where(bdmask, upd, 0.0)
    for pp in range(N_PAIRS):
        st_ref[pp] = states[pp]

    bd = _head_block_diag(1.0 / HEAD_DIM)
    for pp in range(N_PAIRS):
        sl = slice(pp * LANES, (pp + 1) * LANES)
        y = o_ref[:, sl]
        mu = _dot2x(y, bd)
        d = y - mu
        var = _dot2x(d * d, bd)
        yn = d * lax.rsqrt(var + RWKV_GN_EPS) * lnx_ref[0:1, sl] + lnx_ref[1:2, sl]
        o_ref[:, sl] = (yn + bonus_ref[:, sl]) * g_ref[:, sl]


def _rwkv_scan(pre, lnx, batch, seq, ts=256):
    big = pl.BlockSpec((None, ts, WIDTH), lambda b, t: (b, t, 0))
    return pl.pallas_call(
        functools.partial(_rwkv_scan_kernel, ts=ts),
        grid=(batch, seq // ts),
        in_specs=[big] * 12 + [pl.BlockSpec((8, WIDTH), lambda b, t: (0, 0))],
        out_specs=big,
        out_shape=jax.ShapeDtypeStruct((batch, seq, WIDTH), F32),
        scratch_shapes=[pltpu.VMEM((N_PAIRS, LANES, LANES), F32)],
        compiler_params=_cparams(("parallel", "arbitrary")),
        name="rwkv_scan",
    )(*pre, lnx)


def _outproj_kernel(x_ref, oa_ref, orw_ref, w_ref, ln_ref, o_ref):
    y = (jnp.dot(oa_ref[...].astype(BF16), w_ref[0:WIDTH, :], preferred_element_type=F32)
         + jnp.dot(orw_ref[...].astype(BF16), w_ref[WIDTH:2 * WIDTH, :], preferred_element_type=F32))
    o_ref[...] = _layer_norm(ALPHA * x_ref[...] + y, ln_ref[0:1, :], ln_ref[1:2, :])


def _outproj(x2d, o_att, o_rwkv, w_out, ln, tm=512):
    n = x2d.shape[0]
    return pl.pallas_call(
        _outproj_kernel,
        grid=(n // tm,),
        in_specs=[pl.BlockSpec((tm, D_MODEL), lambda i: (i, 0)),
                  pl.BlockSpec((tm, WIDTH), lambda i: (i, 0)),
                  pl.BlockSpec((tm, WIDTH), lambda i: (i, 0)),
                  pl.BlockSpec((2 * WIDTH, D_MODEL), lambda i: (0, 0)),
                  pl.BlockSpec((8, D_MODEL), lambda i: (0, 0))],
        out_specs=pl.BlockSpec((tm, D_MODEL), lambda i: (i, 0)),
        out_shape=jax.ShapeDtypeStruct((n, D_MODEL), F32),
        compiler_params=_cparams(("parallel",)),
        name="outproj_ln",
    )(x2d, o_att, o_rwkv, w_out, ln)


def _kvproj_kernel(m_ref, w_ref, o_ref):
    o_ref[...] = jnp.dot(m_ref[...].astype(BF16), w_ref[...], preferred_element_type=F32).astype(BF16)


def _kvproj(mem2d, w_kv, tm=256):
    n = mem2d.shape[0]
    return pl.pallas_call(
        _kvproj_kernel,
        grid=(n // tm,),
        in_specs=[pl.BlockSpec((tm, D_MODEL), lambda i: (i, 0)),
                  pl.BlockSpec((D_MODEL, 2 * D_MODEL), lambda i: (0, 0))],
        out_specs=pl.BlockSpec((tm, 2 * D_MODEL), lambda i: (i, 0)),
        out_shape=jax.ShapeDtypeStruct((n, 2 * D_MODEL), BF16),
        compiler_params=_cparams(("parallel",)),
        name="kvproj",
    )(mem2d, w_kv)


def _memattn_kernel(x_ref, kv_ref, wq_ref, wo_ref, ln_ref, wr_ref, o_ref, logit_ref):
    x = x_ref[...]
    q = jnp.dot(x.astype(BF16), wq_ref[...], preferred_element_type=F32) * (MEM_HEAD_DIM ** -0.5)
    heads = []
    for h in range(N_MEM_HEADS):
        sl = slice(h * MEM_HEAD_DIM, (h + 1) * MEM_HEAD_DIM)
        kh = kv_ref[:, sl]
        vh = kv_ref[:, D_MODEL + h * MEM_HEAD_DIM:D_MODEL + (h + 1) * MEM_HEAD_DIM]
        s = lax.dot_general(q[:, sl].astype(BF16), kh, NT_DIMS, preferred_element_type=F32)
        p = jnp.exp(s - jnp.max(s, -1, keepdims=True))
        o = jnp.dot(p.astype(BF16), vh, preferred_element_type=F32)
        heads.append(o / jnp.sum(p, -1, keepdims=True))
    o = jnp.concatenate(heads, axis=-1).astype(BF16)
    y = jnp.dot(o, wo_ref[...], preferred_element_type=F32)
    x2 = _layer_norm(ALPHA * x + y, ln_ref[0:1, :], ln_ref[1:2, :])
    o_ref[...] = x2
    logit_ref[...] = jnp.dot(x2, wr_ref[...], precision=HIGHEST, preferred_element_type=F32)


def _memattn(x3d, kv3d, wq, wo, ln, w_router_pad, tm=512):
    batch, seq, _ = x3d.shape
    mlen = kv3d.shape[1]
    full = lambda shape: pl.BlockSpec(shape, lambda b, t: (0,) * len(shape))
    return pl.pallas_call(
        _memattn_kernel,
        grid=(batch, seq // tm),
        in_specs=[pl.BlockSpec((None, tm, D_MODEL), lambda b, t: (b, t, 0)),
                  pl.BlockSpec((None, mlen, 2 * D_MODEL), lambda b, t: (b, 0, 0)),
                  full((D_MODEL, D_MODEL)), full((D_MODEL, D_MODEL)), full((8, D_MODEL)),
                  full((D_MODEL, LANES))],
        out_specs=[pl.BlockSpec((None, tm, D_MODEL), lambda b, t: (b, t, 0)),
                   pl.BlockSpec((None, tm, LANES), lambda b, t: (b, t, 0))],
        out_shape=[jax.ShapeDtypeStruct((batch, seq, D_MODEL), F32),
                   jax.ShapeDtypeStruct((batch, seq, LANES), F32)],
        compiler_params=_cparams(("parallel", "parallel")),
        name="memattn_ln",
    )(x3d, kv3d, wq, wo, ln, w_router_pad)


def _row_gather_kernel(nused_ref, idx_ref, x_hbm, o_ref, sem):
    i = pl.program_id(0)
    rows = o_ref.shape[0]

    @pl.when(i < nused_ref[0])
    def _():
        def copy(r):
            return pltpu.make_async_copy(x_hbm.at[pl.ds(idx_ref[0, 0, r], 1)], o_ref.at[pl.ds(r, 1)], sem)

        def start(r, c):
            copy(r).start()
            return c

        def wait(r, c):
            copy(r).wait()
            return c

        lax.fori_loop(0, rows, start, 0)
        lax.fori_loop(0, rows, wait, 0)

    @pl.when(i >= nused_ref[0])
    def _():
        o_ref[...] = jnp.zeros_like(o_ref)


def _row_gather(x2d, row_tok3, n_used, rows):
    n_blk = row_tok3.shape[0]
    grid_spec = pltpu.PrefetchScalarGridSpec(
        num_scalar_prefetch=1,
        grid=(n_blk,),
        in_specs=[pl.BlockSpec((1, 1, rows), lambda i, nu: (i, 0, 0), memory_space=pltpu.SMEM),
                  pl.BlockSpec(memory_space=pl.ANY)],
        out_specs=pl.BlockSpec((rows, D_MODEL), lambda i, nu: (i, 0)),
        scratch_shapes=[pltpu.SemaphoreType.DMA(())],
    )
    return pl.pallas_call(
        _row_gather_kernel,
        grid_spec=grid_spec,
        out_shape=jax.ShapeDtypeStruct((n_blk * rows, D_MODEL), F32),
        compiler_params=_cparams(("arbitrary",)),
        name="moe_gather",
    )(n_used, row_tok3, x2d)


def _experts_kernel(be_ref, nused_ref, x_ref, wg_ref, wu_ref, wd_ref, o_ref):
    i = pl.program_id(0)

    @pl.when(i < nused_ref[0])
    def _():
        xb = x_ref[...].astype(BF16)
        hg = jnp.dot(xb, wg_ref[...], preferred_element_type=F32)
        hu = jnp.dot(xb, wu_ref[...], preferred_element_type=F32)
        h = hg * _sigmoid(hg) * hu
        o_ref[...] = jnp.dot(h.astype(BF16), wd_ref[...], preferred_element_type=F32)

    @pl.when(i >= nused_ref[0])
    def _():
        o_ref[...] = jnp.zeros_like(o_ref)


def _experts(xs, blk_expert, n_used, w_gate, w_up, w_down, rows):
    n_blk = xs.shape[0] // rows
    row_map = lambda i, be, nu: (jnp.minimum(i, nu[0] - 1), 0)
    w_map = lambda i, be, nu: (be[jnp.minimum(i, nu[0] - 1)], 0, 0)
    grid_spec = pltpu.PrefetchScalarGridSpec(
        num_scalar_prefetch=2,
        grid=(n_blk,),
        in_specs=[pl.BlockSpec((rows, D_MODEL), row_map),
                  pl.BlockSpec((None, D_MODEL, D_EXPERT), w_map),
                  pl.BlockSpec((None, D_MODEL, D_EXPERT), w_map),
                  pl.BlockSpec((None, D_EXPERT, D_MODEL), w_map)],
        out_specs=pl.BlockSpec((rows, D_MODEL), lambda i, be, nu: (i, 0)),
    )
    return pl.pallas_call(
        _experts_kernel,
        grid_spec=grid_spec,
        out_shape=jax.ShapeDtypeStruct((n_blk * rows, D_MODEL), F32),
        compiler_params=_cparams(("arbitrary",)),
        name="moe_experts",
    )(blk_expert, n_used, xs, w_gate, w_up, w_down)


def _combine_kernel(dest_ref, x_ref, gate_ref, ln_ref, y_hbm, o_ref, buf_ref, sem):
    tm = x_ref.shape[0]

    def copy(slot, t):
        src = y_hbm.at[pl.ds(dest_ref[0, 0, slot * tm + t], 1)]
        return pltpu.make_async_copy(src, buf_ref.at[slot, pl.ds(t, 1)], sem)

    def start(t, c):
        copy(0, t).start()
        copy(1, t).start()
        return c

    def wait(t, c):
        copy(0, t).wait()
        copy(1, t).wait()
        return c

    lax.fori_loop(0, tm, start, 0)
    lax.fori_loop(0, tm, wait, 0)
    y = gate_ref[:, 0:1] * buf_ref[0] + gate_ref[:, 1:2] * buf_ref[1]
    o_ref[...] = _layer_norm(ALPHA * x_ref[...] + y, ln_ref[0:1, :], ln_ref[1:2, :])


def _combine(x2d, gates, dest3, ys, ln, tm=256):
    n = x2d.shape[0]
    return pl.pallas_call(
        _combine_kernel,
        grid=(n // tm,),
        in_specs=[pl.BlockSpec((1, 1, 2 * tm), lambda i: (i, 0, 0), memory_space=pltpu.SMEM),
                  pl.BlockSpec((tm, D_MODEL), lambda i: (i, 0)),
                  pl.BlockSpec((tm, 2), lambda i: (i, 0)),
                  pl.BlockSpec((8, D_MODEL), lambda i: (0, 0)),
                  pl.BlockSpec(memory_space=pl.ANY)],
        out_specs=pl.BlockSpec((tm, D_MODEL), lambda i: (i, 0)),
        out_shape=jax.ShapeDtypeStruct((n, D_MODEL), F32),
        scratch_shapes=[pltpu.VMEM((2, tm, D_MODEL), F32), pltpu.SemaphoreType.DMA(())],
        compiler_params=_cparams(("arbitrary",)),
        name="moe_combine_ln",
    )(dest3, x2d, gates, ln, ys)


def _route(logits, b_router):
    s = jax.nn.sigmoid(logits[:, :N_EXPERTS])
    sel = s + b_router
    per_group = N_EXPERTS // N_GROUPS
    grp = jnp.sum(lax.top_k(sel.reshape(-1, N_GROUPS, per_group), 2)[0], -1)
    g_best = jnp.argmax(grp, axis=-1)
    in_grp = (jnp.arange(N_EXPERTS) // per_group)[None, :] == g_best[:, None]
    _, e_idx = lax.top_k(jnp.where(in_grp, sel, NEG), 2)
    w = jnp.take_along_axis(s, e_idx, axis=-1)
    return e_idx.astype(I32), w / jnp.sum(w, -1, keepdims=True)


def _dispatch_plan(e_idx, rows):
    n = e_idx.shape[0]
    a = 2 * n
    eid = e_idx.reshape(-1)
    onehot = (eid[:, None] == jnp.arange(N_EXPERTS, dtype=I32)[None, :]).astype(I32)
    csum = jnp.cumsum(onehot, axis=0)
    rank = jnp.take_along_axis(csum - onehot, eid[:, None], axis=1)[:, 0]
    counts = csum[-1]
    pcounts = (counts + rows - 1) // rows * rows
    pend = jnp.cumsum(pcounts)
    poff = pend - pcounts
    dest = (poff[eid] + rank).astype(I32)
    m = a + N_EXPERTS * rows
    n_blk = m // rows
    tok = (jnp.arange(a, dtype=I32) // 2)
    row_tok = jnp.zeros((m,), I32).at[dest].set(tok)
    blk_expert = jnp.minimum(jnp.searchsorted(pend, jnp.arange(n_blk, dtype=I32) * rows, side='right'),
                             N_EXPERTS - 1).astype(I32)
    n_used = (pend[-1] // rows).astype(I32).reshape(1)
    return dest, row_tok.reshape(n_blk, 1, rows), blk_expert, n_used


def _moe(x2, logits, b_router, w_gate, w_up, w_down, ln, rows=MOE_ROWS, tm=256):
    n = x2.shape[0]
    e_idx, gates = _route(logits, b_router)
    dest, row_tok3, blk_expert, n_used = _dispatch_plan(e_idx, rows)
    xs = _row_gather(x2, row_tok3, n_used, rows)
    ys = _experts(xs, blk_expert, n_used, w_gate, w_up, w_down, rows)
    dest3 = dest.reshape(n // tm, tm, 2).transpose(0, 2, 1).reshape(n // tm, 1, 2 * tm)
    return _combine(x2, gates, dest3, ys, ln, tm)


def _pad_rows(w, rows_before, total):
    return jnp.pad(w, ((rows_before, total - rows_before - w.shape[0]), (0, 0)))


def kernel(x, mem, w_in, w_in_vres, mu_shift, mu_vres, w0, w_decay_up, a0, w_iclr_up, v0, w_vres_up,
           w_gate_up, k_k, k_a, r_k, lnx_g, lnx_b, attn_out_g, w_out, w_mem_q, w_mem_kv, w_mem_o,
           w_router, b_router, w_exp_gate, w_exp_up, w_exp_down, ln_g, ln_b):
    batch, seq, d = x.shape
    n = batch * seq
    mlen = mem.shape[1]
    zeros = jnp.zeros
    slopes = 2.0 ** (-8.0 * jnp.arange(1, N_HEADS + 1, dtype=F32) / N_HEADS)
    w_router_pad = jnp.pad(w_router, ((0, 0), (0, LANES - N_EXPERTS)))
    mem2d = mem.reshape(batch * mlen, d)
    n_shift = w_in.shape[2] - 3 * WIDTH
    n_lora = n_shift - 3 * WIDTH
    n_vres = w_in_vres.shape[2]
    x2d = x.reshape(n, d)
    v_first = None
    for l in range(DEPTH):
        has_vres = l > 0
        w_extra = w_in_vres[l - 1] if has_vres else zeros((d, n_vres), F32)
        w_pad = jnp.concatenate(
            [w_in[l], w_extra, zeros((d, IN_COLS_PAD - w_in.shape[2] - n_vres), F32)], axis=1).astype(BF16)
        z_att, z_rkv, z_lora = _inproj(x2d, w_pad)
        o_att = _moba(z_att.reshape(batch, seq, 3 * WIDTH), slopes, attn_out_g[l].reshape(1, WIDTH), batch, seq)

        mu_rkv = mu_shift[l, :3 * WIDTH].reshape(1, 3 * WIDTH)
        mu_extra = mu_vres[l - 1] if has_vres else zeros((n_vres,), F32)
        mu_lora = jnp.concatenate(
            [mu_shift[l, 3 * WIDTH:], mu_extra, zeros((LORA_COLS - n_lora - n_vres,), F32)]).reshape(1, LORA_COLS)
        v0_l = v0[l - 1] if has_vres else zeros((WIDTH,), F32)
        vecs = jnp.stack([w0[l], a0[l], v0_l, k_k[l], k_a[l], r_k[l].reshape(WIDTH),
                          zeros((WIDTH,), F32), zeros((WIDTH,), F32)])
        n_dec, n_iclr = w_decay_up.shape[1], w_iclr_up.shape[1]
        wd = _pad_rows(w_decay_up[l], 0, LANES).astype(BF16)
        wa = _pad_rows(w_iclr_up[l], n_dec, LANES).astype(BF16)
        wg = w_gate_up[l].astype(BF16)
        wv = _pad_rows(w_vres_up[l - 1], 0, LANES).astype(BF16) if has_vres else None
        pre = _rwkv_pre(z_rkv.reshape(batch, seq, 3 * WIDTH), z_lora.reshape(batch, seq, LORA_COLS),
                        v_first, mu_rkv, mu_lora, vecs, wd, wa, wg, wv, batch, seq)
        if not has_vres:
            v_first = pre[5]
        lnx = jnp.concatenate([lnx_g[l][None], lnx_b[l][None], zeros((6, WIDTH), F32)])
        o_rwkv = _rwkv_scan(pre, lnx, batch, seq)

        ln = lambda j: jnp.concatenate([ln_g[l, j][None], ln_b[l, j][None], zeros((6, d), F32)])
        x1 = _outproj(x2d, o_att.reshape(n, WIDTH), o_rwkv.reshape(n, WIDTH), w_out[l].astype(BF16), ln(0))

        kv = _kvproj(mem2d, w_mem_kv[l].astype(BF16)).reshape(batch, mlen, 2 * d)
        x2, logits = _memattn(x1.reshape(batch, seq, d), kv, w_mem_q[l].astype(BF16),
                              w_mem_o[l].astype(BF16), ln(1), w_router_pad)
        x2d = _moe(x2.reshape(n, d), logits.reshape(n, LANES), b_router,
                   w_exp_gate[l].astype(BF16), w_exp_up[l].astype(BF16), w_exp_down[l].astype(BF16), ln(2))
    return x2d.reshape(batch, seq, d)
```

```python
import functools

import jax
import jax.numpy as jnp
from jax import lax
from jax.experimental import pallas as pl
from jax.experimental.pallas import tpu as pltpu

F32 = jnp.float32
BF16 = jnp.bfloat16
I32 = jnp.int32
HIGHEST = lax.Precision.HIGHEST

D_MODEL = 1024
DEPTH = 2
N_HEADS = 8
HEAD_DIM = 64
WIDTH = N_HEADS * HEAD_DIM
N_PAIRS = N_HEADS // 2
LANES = 128
MOBA_BLOCK = 256
MOBA_TOP = 3
LORA_COLS = 384
IN_COLS_PAD = 3 * WIDTH + 3 * WIDTH + LORA_COLS
RWKV_GN_EPS = 64e-5
N_MEM_HEADS = 4
MEM_HEAD_DIM = D_MODEL // N_MEM_HEADS
N_EXPERTS = 32
N_GROUPS = 8
D_EXPERT = 512
LN_EPS = 1e-5
RMS_EPS = 1e-6
NEG = -1e30
ALPHA = (2 * DEPTH) ** 0.25
CHUNK = 64
MOE_ROWS = 256

VMEM_LIMIT = 48 * 1024 * 1024

NT_DIMS = (((1,), (1,)), ((), ()))
TN_DIMS = (((0,), (0,)), ((), ()))


def _cparams(sem):
    return pltpu.CompilerParams(dimension_semantics=sem, vmem_limit_bytes=VMEM_LIMIT)


def _split(a):
    hi = a.astype(BF16)
    lo = (a - hi.astype(F32)).astype(BF16)
    return hi, lo


def _dot3(a, b, dims=(((1,), (0,)), ((), ()))):
    ah, al = _split(a)
    bh, bl = _split(b)
    d = functools.partial(lax.dot_general, dimension_numbers=dims, preferred_element_type=F32)
    return d(ah, bh) + (d(al, bh) + d(ah, bl))


def _dot1(a, b, dims=(((1,), (0,)), ((), ()))):
    return lax.dot_general(a.astype(BF16), b.astype(BF16), dims, preferred_element_type=F32)


def _layer_norm(y, g, b):
    mu = jnp.mean(y, -1, keepdims=True)
    d = y - mu
    var = jnp.mean(d * d, -1, keepdims=True)
    return d * lax.rsqrt(var + LN_EPS) * g + b


def _sigmoid(x):
    return 1.0 / (1.0 + jnp.exp(-x))


def _inproj_kernel(x_ref, w_ref, att_ref, rkv_ref, lora_ref):
    xb = x_ref[...].astype(BF16)
    att_ref[...] = jnp.dot(xb, w_ref[:, 0:3 * WIDTH], preferred_element_type=F32)
    rkv_ref[...] = jnp.dot(xb, w_ref[:, 3 * WIDTH:6 * WIDTH], preferred_element_type=F32)
    lora_ref[...] = jnp.dot(xb, w_ref[:, 6 * WIDTH:IN_COLS_PAD], preferred_element_type=F32)


def _inproj(x2d, w_pad, tm=256):
    n = x2d.shape[0]
    return pl.pallas_call(
        _inproj_kernel,
        grid=(n // tm,),
        in_specs=[pl.BlockSpec((tm, D_MODEL), lambda i: (i, 0)),
                  pl.BlockSpec((D_MODEL, IN_COLS_PAD), lambda i: (0, 0))],
        out_specs=[pl.BlockSpec((tm, 3 * WIDTH), lambda i: (i, 0)),
                   pl.BlockSpec((tm, 3 * WIDTH), lambda i: (i, 0)),
                   pl.BlockSpec((tm, LORA_COLS), lambda i: (i, 0))],
        out_shape=[jax.ShapeDtypeStruct((n, 3 * WIDTH), F32),
                   jax.ShapeDtypeStruct((n, 3 * WIDTH), F32),
                   jax.ShapeDtypeStruct((n, LORA_COLS), F32)],
        compiler_params=_cparams(("parallel",)),
        name="inproj",
    )(x2d, w_pad)


def _moba_kernel(slopes_ref, q_ref, k_ref, v_ref, gain_ref, o_ref, kaug_ref, vb_ref, kmp_ref, *, seq):
    pair = pl.program_id(1)
    i = pl.program_id(2)
    nb = seq // MOBA_BLOCK
    blk = MOBA_BLOCK

    @pl.when(i == 0)
    def _():
        k = k_ref[...]
        rowblk = lax.shift_right_logical(lax.broadcasted_iota(I32, (seq, LANES), 0), 8)
        lane_s = lax.broadcasted_iota(I32, (seq, LANES), 1)
        r_i = lax.broadcasted_iota(I32, (LANES, seq), 0)
        c_b = lax.shift_right_logical(lax.broadcasted_iota(I32, (LANES, seq), 1), 8)
        for hh in range(2):
            ob = HEAD_DIM * (1 - hh)
            inhead = (lane_s >= HEAD_DIM * hh) & (lane_s < HEAD_DIM * (hh + 1))
            ind = (lane_s - ob) == rowblk
            kaug_ref[hh] = jnp.where(inhead, k, jnp.where(ind, 1.0, 0.0)).astype(BF16)
            indm = jnp.where((r_i - ob) == c_b, 1.0 / blk, 0.0)
            kmp_ref[hh] = jnp.dot(indm, k, precision=HIGHEST, preferred_element_type=F32)
        vb_ref[...] = v_ref[...].astype(BF16)

    q = q_ref[...] * (HEAD_DIM ** -0.5)
    row = lax.broadcasted_iota(I32, (blk, blk), 0)
    col = lax.broadcasted_iota(I32, (blk, blk), 1)
    rc = (row - col).astype(F32)
    lane = lax.broadcasted_iota(I32, (blk, LANES), 1)
    gain = gain_ref[...]
    outs = []
    for hh in range(2):
        slope = slopes_ref[2 * pair + hh]
        ob = HEAD_DIM * (1 - hh)
        inhead = (lane >= HEAD_DIM * hh) & (lane < HEAD_DIM * (hh + 1))
        qm = jnp.where(inhead, q, 0.0)
        gate = lax.dot_general(qm, kmp_ref[hh], NT_DIMS, precision=HIGHEST, preferred_element_type=F32)
        jl = lane - ob
        cnt = jnp.zeros((blk, LANES), I32)
        for jp in range(nb):
            colv = gate[:, ob + jp:ob + jp + 1]
            beats = (colv > gate) | ((colv == gate) & (jp < jl))
            cnt = cnt + jnp.where(beats & (jp < i), 1, 0)
        past = (jl >= 0) & (jl < i)
        sel = (cnt < MOBA_TOP) & past
        bias = jnp.where(sel, (-slope * blk) * (i - jl).astype(F32), NEG)
        bias = jnp.where(past, bias, 0.0)
        qa = jnp.where(inhead, q, bias).astype(BF16)

        t_off = -slope * rc
        t_own = jnp.where(rc >= 0, t_off, NEG)

        def scores(j, tmat):
            kb = kaug_ref[hh, pl.ds(pl.multiple_of(j * blk, blk), blk), :]
            return lax.dot_general(qa, kb, NT_DIMS, preferred_element_type=F32) + tmat

        def pv(p, j):
            vb = vb_ref[pl.ds(pl.multiple_of(j * blk, blk), blk), :]
            return jnp.dot(p.astype(BF16), vb, preferred_element_type=F32)

        s = scores(i, t_own)
        m = jnp.max(s, -1, keepdims=True)
        p = jnp.exp(s - m)
        l = jnp.sum(p, -1, keepdims=True)
        acc = pv(p, i)

        def body(j, carry):
            m, l, acc = carry
            s = scores(j, t_off)
            m_new = jnp.maximum(m, jnp.max(s, -1, keepdims=True))
            alpha = jnp.exp(m - m_new)
            p = jnp.exp(s - m_new)
            l = alpha * l + jnp.sum(p, -1, keepdims=True)
            acc = alpha * acc + pv(p, j)
            return m_new, l, acc

        m, l, acc = lax.fori_loop(0, i, body, (m, l, acc))
        out = acc / l
        ms = jnp.sum(jnp.where(inhead, out * out, 0.0), -1, keepdims=True) * (1.0 / HEAD_DIM)
        outs.append(out * lax.rsqrt(ms + RMS_EPS) * gain)
    o_ref[...] = jnp.where(lane < HEAD_DIM, outs[0], outs[1])


def _moba(z_att, slopes, gain, batch, seq):
    nb = seq // MOBA_BLOCK
    grid_spec = pltpu.PrefetchScalarGridSpec(
        num_scalar_prefetch=1,
        grid=(batch, N_PAIRS, nb),
        in_specs=[pl.BlockSpec((None, MOBA_BLOCK, LANES), lambda b, p, i, s: (b, i, p)),
                  pl.BlockSpec((None, seq, LANES), lambda b, p, i, s: (b, 0, N_PAIRS + p)),
                  pl.BlockSpec((None, seq, LANES), lambda b, p, i, s: (b, 0, 2 * N_PAIRS + p)),
                  pl.BlockSpec((1, LANES), lambda b, p, i, s: (0, p))],
        out_specs=pl.BlockSpec((None, MOBA_BLOCK, LANES), lambda b, p, i, s: (b, i, p)),
        scratch_shapes=[pltpu.VMEM((2, seq, LANES), BF16),
                        pltpu.VMEM((seq, LANES), BF16),
                        pltpu.VMEM((2, LANES, LANES), F32)],
    )
    return pl.pallas_call(
        functools.partial(_moba_kernel, seq=seq),
        grid_spec=grid_spec,
        out_shape=jax.ShapeDtypeStruct((batch, seq, WIDTH), F32),
        compiler_params=_cparams(("parallel", "parallel", "arbitrary")),
        name="moba",
    )(slopes, z_att, z_att, z_att, gain)


def _dot2x(a, b_exact):
    ah, al = _split(a)
    return (jnp.dot(ah, b_exact, preferred_element_type=F32)
            + jnp.dot(al, b_exact, preferred_element_type=F32))


def _head_block_diag(scale):
    r = lax.shift_right_logical(lax.broadcasted_iota(I32, (LANES, LANES), 0), 6)
    c = lax.shift_right_logical(lax.broadcasted_iota(I32, (LANES, LANES), 1), 6)
    return jnp.where(r == c, scale, 0.0).astype(BF16)


def _softplus(x):
    return jnp.maximum(x, 0.0) + jnp.log(1.0 + jnp.exp(-jnp.abs(x)))


def _rwkv_chunk_pair(lw, r, k2, v, a, b):
    c_ = CHUNK
    row = lax.broadcasted_iota(I32, (c_, c_), 0)
    col = lax.broadcasted_iota(I32, (c_, c_), 1)
    incl = row >= col
    strict = row > col
    lane = lax.broadcasted_iota(I32, (c_, LANES), 1)
    m0 = lane < HEAD_DIM
    tri = jnp.where(incl, 1.0, 0.0).astype(BF16)
    lw_hi, lw_lo = _split(lw)
    c = (jnp.dot(tri, lw_hi, preferred_element_type=F32) + jnp.dot(tri, lw_lo, preferred_element_type=F32))
    e_pos = jnp.exp(c)
    e_neg = jnp.exp(-c)
    at = a * jnp.exp(c - lw)
    bt = b * e_neg
    kt = k2 * e_neg
    rt = r * e_pos
    gc = e_pos[c_ - 1:c_, :]
    bp = bt * gc
    kp = kt * gc
    vs = jnp.concatenate([jnp.where(m0, v, 0.0), jnp.where(m0, 0.0, v)], axis=0)
    eye = jnp.where(row == col, 1.0, 0.0)
    ts, laks, qbs, qks = [], [], [], []
    for hh in range(2):
        mh = m0 if hh == 0 else jnp.logical_not(m0)
        am = jnp.where(mh, at, 0.0)
        rm = jnp.where(mh, rt, 0.0)
        lab = jnp.where(strict, _dot3(am, bt, NT_DIMS), 0.0)
        laks.append(jnp.where(strict, _dot3(am, kt, NT_DIMS), 0.0))
        qbs.append(jnp.where(incl, _dot3(rm, bt, NT_DIMS), 0.0))
        qks.append(jnp.where(incl, _dot3(rm, kt, NT_DIMS), 0.0))
        t = eye + lab
        lp = lab
        for _ in range(5):
            lp = _dot3(lp, lp)
            t = t + _dot3(t, lp)
        ts.append(t)
    tcat = jnp.concatenate(ts, axis=1)
    qcat = jnp.concatenate(qbs, axis=1)
    p = _dot3(jnp.concatenate(laks, axis=1), vs)
    yk = _dot3(jnp.concatenate(qks, axis=1), vs)
    grow = jnp.broadcast_to(gc, (c_, LANES))
    return at, p, tcat, bp, kp, grow, rt, qcat, yk


def _rwkv_pre_kernel(*refs, has_vres, tb):
    if has_vres:
        (rkv_ref, lora_ref, vf_ref, mu_rkv_ref, mu_lora_ref, vec_ref, wd_ref, wa_ref, wg_ref, wv_ref,
         at_ref, p_ref, tc_ref, bp_ref, kp_ref, v_ref, gr_ref, rt_ref, qc_ref, yk_ref, bonus_ref, g_ref,
         carry_ref, carryl_ref) = refs
    else:
        (rkv_ref, lora_ref, mu_rkv_ref, mu_lora_ref, vec_ref, wd_ref, wa_ref, wg_ref,
         at_ref, p_ref, tc_ref, bp_ref, kp_ref, v_ref, gr_ref, rt_ref, qc_ref, yk_ref, bonus_ref, g_ref,
         carry_ref, carryl_ref) = refs
    t_idx = pl.program_id(1)

    @pl.when(t_idx == 0)
    def _():
        carry_ref[...] = jnp.zeros_like(carry_ref)
        carryl_ref[...] = jnp.zeros_like(carryl_ref)

    def token_shift(z, cref, mu):
        first = lax.broadcasted_iota(I32, z.shape, 0) == 0
        prev = jnp.where(first, cref[0:1, :], pltpu.roll(z, 1, 0))
        cref[0:1, :] = z[tb - 1:tb, :]
        return z + (prev - z) * mu

    zs = token_shift(rkv_ref[...], carry_ref, mu_rkv_ref[...])
    zl = token_shift(lora_ref[...], carryl_ref, mu_lora_ref[...])
    l0 = zl[:, 0:LANES]
    w0, a0, v0 = vec_ref[0:1, :], vec_ref[1:2, :], vec_ref[2:3, :]
    k_k, k_a, r_k = vec_ref[3:4, :], vec_ref[4:5, :], vec_ref[5:6, :]
    dw = _dot1(jnp.tanh(l0), wd_ref[...])
    lw_all = -jnp.exp(-_softplus(-(w0 + dw)) - 0.5)
    a_lr = _sigmoid(a0 + _dot1(l0, wa_ref[...]))
    g_ref[...] = _dot1(_sigmoid(zl[:, LANES:2 * LANES]), wg_ref[...])
    r_all = zs[:, 0:WIDTH]
    k_all = zs[:, WIDTH:2 * WIDTH]
    v_all = zs[:, 2 * WIDTH:3 * WIDTH]
    if has_vres:
        mix = _sigmoid(v0 + _dot1(zl[:, 2 * LANES:3 * LANES], wv_ref[...]))
        v_all = v_all + (vf_ref[...] - v_all) * mix
    v_ref[...] = v_all
    bd = _head_block_diag(1.0)
    for pp in range(N_PAIRS):
        sl = slice(pp * LANES, (pp + 1) * LANES)
        r, k, v = r_all[:, sl], k_all[:, sl], v_all[:, sl]
        alr = a_lr[:, sl]
        kk = k * k_k[:, sl]
        ss = _dot2x(kk * kk, bd)
        kk = kk * lax.rsqrt(jnp.maximum(ss, 1e-24))
        k2 = k * (1.0 + (alr - 1.0) * k_a[:, sl])
        bonus_ref[:, sl] = _dot2x(r * k2 * r_k[:, sl], bd) * v
        a = -kk
        b = kk * alr
        lw = lw_all[:, sl]
        for ci in range(tb // CHUNK):
            rs = slice(ci * CHUNK, (ci + 1) * CHUNK)
            outs = _rwkv_chunk_pair(lw[rs], r[rs], k2[rs], v[rs], a[rs], b[rs])
            for ref, val in zip((at_ref, p_ref, tc_ref, bp_ref, kp_ref, gr_ref, rt_ref, qc_ref, yk_ref), outs):
                ref[rs, sl] = val


def _rwkv_pre(z_rkv, z_lora, v_first, mu_rkv, mu_lora, vecs, wd, wa, wg, wv, batch, seq, tb=128):
    has_vres = v_first is not None
    big = lambda w: pl.BlockSpec((None, tb, w), lambda b, t: (b, t, 0))
    full = lambda shape: pl.BlockSpec(shape, lambda b, t: (0,) * len(shape))
    in_specs = [big(3 * WIDTH), big(LORA_COLS)] + ([big(WIDTH)] if has_vres else [])
    in_specs += [full((1, 3 * WIDTH)), full((1, LORA_COLS)), full((8, WIDTH)),
                 full((LANES, WIDTH)), full((LANES, WIDTH)), full((LANES, WIDTH))]
    args = [z_rkv, z_lora] + ([v_first] if has_vres else []) + [mu_rkv, mu_lora, vecs, wd, wa, wg]
    if has_vres:
        in_specs.append(full((LANES, WIDTH)))
        args.append(wv)
    n_out = 12
    return pl.pallas_call(
        functools.partial(_rwkv_pre_kernel, has_vres=has_vres, tb=tb),
        grid=(batch, seq // tb),
        in_specs=in_specs,
        out_specs=[big(WIDTH)] * n_out,
        out_shape=[jax.ShapeDtypeStruct((batch, seq, WIDTH), F32)] * n_out,
        scratch_shapes=[pltpu.VMEM((8, 3 * WIDTH), F32), pltpu.VMEM((8, LORA_COLS), F32)],
        compiler_params=_cparams(("parallel", "arbitrary")),
        name="rwkv_pre",
    )(*args)


def _rwkv_scan_kernel(at_ref, p_ref, tc_ref, bp_ref, kp_ref, v_ref, gr_ref, rt_ref, qc_ref, yk_ref,
                      bonus_ref, g_ref, lnx_ref, o_ref, st_ref, *, ts):
    t_idx = pl.program_id(1)

    @pl.when(t_idx == 0)
    def _():
        st_ref[...] = jnp.zeros_like(st_ref)

    lane = lax.broadcasted_iota(I32, (CHUNK, LANES), 1)
    m0 = lane < HEAD_DIM
    bdmask = (lax.shift_right_logical(lax.broadcasted_iota(I32, (LANES, LANES), 0), 6)
              == lax.shift_right_logical(lax.broadcasted_iota(I32, (LANES, LANES), 1), 6))
    ones_t = jnp.full((CHUNK, LANES), 1.0 / CHUNK, BF16)

    def stack(x):
        return jnp.concatenate([jnp.where(m0, x, 0.0), jnp.where(m0, 0.0, x)], axis=0)

    states = [st_ref[pp] for pp in range(N_PAIRS)]
    for ci in range(ts // CHUNK):
        rs = slice(ci * CHUNK, (ci + 1) * CHUNK)
        for pp in range(N_PAIRS):
            sl = slice(pp * LANES, (pp + 1) * LANES)
            st = states[pp]
            x = _dot3(at_ref[rs, sl], st) + p_ref[rs, sl]
            u = _dot3(tc_ref[rs, sl], stack(x))
            o_ref[rs, sl] = _dot3(rt_ref[rs, sl], st) + _dot3(qc_ref[rs, sl], stack(u)) + yk_ref[rs, sl]
            gh, gl = _split(gr_ref[rs, sl])
            gl2 = (gr_ref[rs, sl] - gh.astype(F32) - gl.astype(F32)).astype(BF16)
            tn = functools.partial(lax.dot_general, dimension_numbers=TN_DIMS, preferred_element_type=F32)
            gcol = tn(gh, ones_t) + tn(gl, ones_t) + tn(gl2, ones_t)
            upd = _dot3(bp_ref[rs, sl], u, TN_DIMS) + _dot3(kp_ref[rs, sl], v_ref[rs, sl], TN_DIMS)
            states[pp] = gcol * st + jnp.where(bdmask, upd, 0.0)
    for pp in range(N_PAIRS):
        st_ref[pp] = states[pp]

    bd = _head_block_diag(1.0 / HEAD_DIM)
    for pp in range(N_PAIRS):
        sl = slice(pp * LANES, (pp + 1) * LANES)
        y = o_ref[:, sl]
        mu = _dot2x(y, bd)
        d = y - mu
        var = _dot2x(d * d, bd)
        yn = d * lax.rsqrt(var + RWKV_GN_EPS) * lnx_ref[0:1, sl] + lnx_ref[1:2, sl]
        o_ref[:, sl] = (yn + bonus_ref[:, sl]) * g_ref[:, sl]


def _rwkv_scan(pre, lnx, batch, seq, ts=256):
    big = pl.BlockSpec((None, ts, WIDTH), lambda b, t: (b, t, 0))
    return pl.pallas_call(
        functools.partial(_rwkv_scan_kernel, ts=ts),
        grid=(batch, seq // ts),
        in_specs=[big] * 12 + [pl.BlockSpec((8, WIDTH), lambda b, t: (0, 0))],
        out_specs=big,
        out_shape=jax.ShapeDtypeStruct((batch, seq, WIDTH), F32),
        scratch_shapes=[pltpu.VMEM((N_PAIRS, LANES, LANES), F32)],
        compiler_params=_cparams(("parallel", "arbitrary")),
        name="rwkv_scan",
    )(*pre, lnx)


def _outproj_kernel(x_ref, oa_ref, orw_ref, w_ref, ln_ref, o_ref):
    y = (jnp.dot(oa_ref[...].astype(BF16), w_ref[0:WIDTH, :], preferred_element_type=F32)
         + jnp.dot(orw_ref[...].astype(BF16), w_ref[WIDTH:2 * WIDTH, :], preferred_element_type=F32))
    o_ref[...] = _layer_norm(ALPHA * x_ref[...] + y, ln_ref[0:1, :], ln_ref[1:2, :])


def _outproj(x2d, o_att, o_rwkv, w_out, ln, tm=512):
    n = x2d.shape[0]
    return pl.pallas_call(
        _outproj_kernel,
        grid=(n // tm,),
        in_specs=[pl.BlockSpec((tm, D_MODEL), lambda i: (i, 0)),
                  pl.BlockSpec((tm, WIDTH), lambda i: (i, 0)),
                  pl.BlockSpec((tm, WIDTH), lambda i: (i, 0)),
                  pl.BlockSpec((2 * WIDTH, D_MODEL), lambda i: (0, 0)),
                  pl.BlockSpec((8, D_MODEL), lambda i: (0, 0))],
        out_specs=pl.BlockSpec((tm, D_MODEL), lambda i: (i, 0)),
        out_shape=jax.ShapeDtypeStruct((n, D_MODEL), F32),
        compiler_params=_cparams(("parallel",)),
        name="outproj_ln",
    )(x2d, o_att, o_rwkv, w_out, ln)


def _kvproj_kernel(m_ref, w_ref, o_ref):
    o_ref[...] = jnp.dot(m_ref[...].astype(BF16), w_ref[...], preferred_element_type=F32).astype(BF16)


def _kvproj(mem2d, w_kv, tm=256):
    n = mem2d.shape[0]
    return pl.pallas_call(
        _kvproj_kernel,
        grid=(n // tm,),
        in_specs=[pl.BlockSpec((tm, D_MODEL), lambda i: (i, 0)),
                  pl.BlockSpec((D_MODEL, 2 * D_MODEL), lambda i: (0, 0))],
        out_specs=pl.BlockSpec((tm, 2 * D_MODEL), lambda i: (i, 0)),
        out_shape=jax.ShapeDtypeStruct((n, 2 * D_MODEL), BF16),
        compiler_params=_cparams(("parallel",)),
        name="kvproj",
    )(mem2d, w_kv)


def _route_lanes(logits, b_router):
    lane = lax.broadcasted_iota(I32, logits.shape, 1)
    lane_f = lane.astype(F32)
    valid = lane < N_EXPERTS
    s = _sigmoid(logits)
    sel = jnp.where(valid, s + b_router, NEG)

    def partner(x, bit):
        return jnp.where((lane & bit) == 0, pltpu.roll(x, LANES - bit, 1), pltpu.roll(x, bit, 1))

    p1 = partner(sel, 1)
    hi1, lo1 = jnp.maximum(sel, p1), jnp.minimum(sel, p1)
    hi2, lo2 = partner(hi1, 2), partner(lo1, 2)
    gsum = jnp.maximum(hi1, hi2) + jnp.maximum(jnp.minimum(hi1, hi2), jnp.maximum(lo1, lo2))
    gmax = jnp.max(gsum, -1, keepdims=True)
    grp_f = lax.shift_right_logical(lane, 2).astype(F32)
    g_best = jnp.min(jnp.where((gsum == gmax) & valid, grp_f, 1e9), -1, keepdims=True)
    in_grp = (grp_f == g_best) & valid
    masked = jnp.where(in_grp, sel, NEG)
    v1 = jnp.max(masked, -1, keepdims=True)
    e1 = jnp.min(jnp.where(in_grp & (masked == v1), lane_f, 1e9), -1, keepdims=True)
    rest = in_grp & (lane_f != e1)
    masked2 = jnp.where(rest, sel, NEG)
    v2 = jnp.max(masked2, -1, keepdims=True)
    e2 = jnp.min(jnp.where(rest & (masked2 == v2), lane_f, 1e9), -1, keepdims=True)
    w1 = jnp.sum(jnp.where(lane_f == e1, s, 0.0), -1, keepdims=True)
    w2 = jnp.sum(jnp.where(lane_f == e2, s, 0.0), -1, keepdims=True)
    tot = w1 + w2
    out = jnp.where(lane == 0, e1, jnp.where(lane == 1, e2, jnp.where(lane == 2, w1 / tot, w2 / tot)))
    return jnp.where(lane < 4, out, 0.0)


def _memattn_kernel(x_ref, kv_ref, wq_ref, wo_ref, ln_ref, wr_ref, br_ref, o_ref, route_ref):
    x = x_ref[...]
    q = jnp.dot(x.astype(BF16), wq_ref[...], preferred_element_type=F32) * (MEM_HEAD_DIM ** -0.5)
    heads = []
    for h in range(N_MEM_HEADS):
        sl = slice(h * MEM_HEAD_DIM, (h + 1) * MEM_HEAD_DIM)
        kh = kv_ref[:, sl]
        vh = kv_ref[:, D_MODEL + h * MEM_HEAD_DIM:D_MODEL + (h + 1) * MEM_HEAD_DIM]
        s = lax.dot_general(q[:, sl].astype(BF16), kh, NT_DIMS, preferred_element_type=F32)
        p = jnp.exp(s - jnp.max(s, -1, keepdims=True))
        o = jnp.dot(p.astype(BF16), vh, preferred_element_type=F32)
        heads.append(o / jnp.sum(p, -1, keepdims=True))
    o = jnp.concatenate(heads, axis=-1).astype(BF16)
    y = jnp.dot(o, wo_ref[...], preferred_element_type=F32)
    x2 = _layer_norm(ALPHA * x + y, ln_ref[0:1, :], ln_ref[1:2, :])
    o_ref[...] = x2
    logits = jnp.dot(x2, wr_ref[...], precision=HIGHEST, preferred_element_type=F32)
    route_ref[...] = _route_lanes(logits, br_ref[0:1, :])


def _memattn(x3d, kv3d, wq, wo, ln, w_router_pad, b_router_pad, tm=512):
    batch, seq, _ = x3d.shape
    mlen = kv3d.shape[1]
    full = lambda shape: pl.BlockSpec(shape, lambda b, t: (0,) * len(shape))
    return pl.pallas_call(
        _memattn_kernel,
        grid=(batch, seq // tm),
        in_specs=[pl.BlockSpec((None, tm, D_MODEL), lambda b, t: (b, t, 0)),
                  pl.BlockSpec((None, mlen, 2 * D_MODEL), lambda b, t: (b, 0, 0)),
                  full((D_MODEL, D_MODEL)), full((D_MODEL, D_MODEL)), full((8, D_MODEL)),
                  full((D_MODEL, LANES)), full((8, LANES))],
        out_specs=[pl.BlockSpec((None, tm, D_MODEL), lambda b, t: (b, t, 0)),
                   pl.BlockSpec((None, tm, LANES), lambda b, t: (b, t, 0))],
        out_shape=[jax.ShapeDtypeStruct((batch, seq, D_MODEL), F32),
                   jax.ShapeDtypeStruct((batch, seq, LANES), F32)],
        compiler_params=_cparams(("parallel", "parallel")),
        name="memattn_ln",
    )(x3d, kv3d, wq, wo, ln, w_router_pad, b_router_pad)


def _row_gather_kernel(nused_ref, idx_ref, x_hbm, o_ref, sem):
    i = pl.program_id(0)
    rows = o_ref.shape[0]

    @pl.when(i < nused_ref[0])
    def _():
        def copy(r):
            return pltpu.make_async_copy(x_hbm.at[pl.ds(idx_ref[0, 0, r], 1)], o_ref.at[pl.ds(r, 1)], sem)

        def start(r, c):
            copy(r).start()
            return c

        def wait(r, c):
            copy(r).wait()
            return c

        lax.fori_loop(0, rows, start, 0)
        lax.fori_loop(0, rows, wait, 0)

    @pl.when(i >= nused_ref[0])
    def _():
        o_ref[...] = jnp.zeros_like(o_ref)


def _row_gather(x2d, row_tok3, n_used, rows):
    n_blk = row_tok3.shape[0]
    grid_spec = pltpu.PrefetchScalarGridSpec(
        num_scalar_prefetch=1,
        grid=(n_blk,),
        in_specs=[pl.BlockSpec((1, 1, rows), lambda i, nu: (i, 0, 0), memory_space=pltpu.SMEM),
                  pl.BlockSpec(memory_space=pl.ANY)],
        out_specs=pl.BlockSpec((rows, D_MODEL), lambda i, nu: (i, 0)),
        scratch_shapes=[pltpu.SemaphoreType.DMA(())],
    )
    return pl.pallas_call(
        _row_gather_kernel,
        grid_spec=grid_spec,
        out_shape=jax.ShapeDtypeStruct((n_blk * rows, D_MODEL), F32),
        compiler_params=_cparams(("arbitrary",)),
        name="moe_gather",
    )(n_used, row_tok3, x2d)


def _experts_kernel(be_ref, nused_ref, x_ref, wg_ref, wu_ref, wd_ref, o_ref):
    i = pl.program_id(0)

    @pl.when(i < nused_ref[0])
    def _():
        xb = x_ref[...].astype(BF16)
        hg = jnp.dot(xb, wg_ref[...], preferred_element_type=F32)
        hu = jnp.dot(xb, wu_ref[...], preferred_element_type=F32)
        h = hg * _sigmoid(hg) * hu
        o_ref[...] = jnp.dot(h.astype(BF16), wd_ref[...], preferred_element_type=F32)

    @pl.when(i >= nused_ref[0])
    def _():
        o_ref[...] = jnp.zeros_like(o_ref)


def _experts(xs, blk_expert, n_used, w_gate, w_up, w_down, rows):
    n_blk = xs.shape[0] // rows
    row_map = lambda i, be, nu: (jnp.minimum(i, nu[0] - 1), 0)
    w_map = lambda i, be, nu: (be[jnp.minimum(i, nu[0] - 1)], 0, 0)
    grid_spec = pltpu.PrefetchScalarGridSpec(
        num_scalar_prefetch=2,
        grid=(n_blk,),
        in_specs=[pl.BlockSpec((rows, D_MODEL), row_map),
                  pl.BlockSpec((None, D_MODEL, D_EXPERT), w_map),
                  pl.BlockSpec((None, D_MODEL, D_EXPERT), w_map),
                  pl.BlockSpec((None, D_EXPERT, D_MODEL), w_map)],
        out_specs=pl.BlockSpec((rows, D_MODEL), lambda i, be, nu: (i, 0)),
    )
    return pl.pallas_call(
        _experts_kernel,
        grid_spec=grid_spec,
        out_shape=jax.ShapeDtypeStruct((n_blk * rows, D_MODEL), F32),
        compiler_params=_cparams(("arbitrary",)),
        name="moe_experts",
    )(blk_expert, n_used, xs, w_gate, w_up, w_down)


def _combine_kernel(dest_ref, x_ref, gate_ref, ln_ref, y_hbm, o_ref, buf_ref, sem):
    tm = x_ref.shape[0]

    def copy(slot, t):
        src = y_hbm.at[pl.ds(dest_ref[0, 0, slot * tm + t], 1)]
        return pltpu.make_async_copy(src, buf_ref.at[slot, pl.ds(t, 1)], sem)

    def start(t, c):
        copy(0, t).start()
        copy(1, t).start()
        return c

    def wait(t, c):
        copy(0, t).wait()
        copy(1, t).wait()
        return c

    lax.fori_loop(0, tm, start, 0)
    lax.fori_loop(0, tm, wait, 0)
    y = gate_ref[:, 0:1] * buf_ref[0] + gate_ref[:, 1:2] * buf_ref[1]
    o_ref[...] = _layer_norm(ALPHA * x_ref[...] + y, ln_ref[0:1, :], ln_ref[1:2, :])


def _combine(x2d, gates, dest3, ys, ln, tm=256):
    n = x2d.shape[0]
    return pl.pallas_call(
        _combine_kernel,
        grid=(n // tm,),
        in_specs=[pl.BlockSpec((1, 1, 2 * tm), lambda i: (i, 0, 0), memory_space=pltpu.SMEM),
                  pl.BlockSpec((tm, D_MODEL), lambda i: (i, 0)),
                  pl.BlockSpec((tm, 2), lambda i: (i, 0)),
                  pl.BlockSpec((8, D_MODEL), lambda i: (0, 0)),
                  pl.BlockSpec(memory_space=pl.ANY)],
        out_specs=pl.BlockSpec((tm, D_MODEL), lambda i: (i, 0)),
        out_shape=jax.ShapeDtypeStruct((n, D_MODEL), F32),
        scratch_shapes=[pltpu.VMEM((2, tm, D_MODEL), F32), pltpu.SemaphoreType.DMA(())],
        compiler_params=_cparams(("arbitrary",)),
        name="moe_combine_ln",
    )(dest3, x2d, gates, ln, ys)


def _dispatch_plan(e_idx, rows):
    n = e_idx.shape[0]
    a = 2 * n
    eid = e_idx.reshape(-1)
    onehot = (eid[:, None] == jnp.arange(N_EXPERTS, dtype=I32)[None, :]).astype(I32)
    csum = jnp.cumsum(onehot, axis=0)
    rank = jnp.take_along_axis(csum - onehot, eid[:, None], axis=1)[:, 0]
    counts = csum[-1]
    pcounts = (counts + rows - 1) // rows * rows
    pend = jnp.cumsum(pcounts)
    poff = pend - pcounts
    dest = (poff[eid] + rank).astype(I32)
    m = a + N_EXPERTS * rows
    n_blk = m // rows
    tok = (jnp.arange(a, dtype=I32) // 2)
    row_tok = jnp.zeros((m,), I32).at[dest].set(tok)
    blk_expert = jnp.minimum(jnp.searchsorted(pend, jnp.arange(n_blk, dtype=I32) * rows, side='right'),
                             N_EXPERTS - 1).astype(I32)
    n_used = (pend[-1] // rows).astype(I32).reshape(1)
    return dest, row_tok.reshape(n_blk, 1, rows), blk_expert, n_used


def _moe(x2, route, w_gate, w_up, w_down, ln, rows=MOE_ROWS, tm=256):
    n = x2.shape[0]
    e_idx = route[:, 0:2].astype(I32)
    gates = route[:, 2:4]
    dest, row_tok3, blk_expert, n_used = _dispatch_plan(e_idx, rows)
    xs = _row_gather(x2, row_tok3, n_used, rows)
    ys = _experts(xs, blk_expert, n_used, w_gate, w_up, w_down, rows)
    dest3 = dest.reshape(n // tm, tm, 2).transpose(0, 2, 1).reshape(n // tm, 1, 2 * tm)
    return _combine(x2, gates, dest3, ys, ln, tm)


def _pad_rows(w, rows_before, total):
    return jnp.pad(w, ((rows_before, total - rows_before - w.shape[0]), (0, 0)))


def kernel(x, mem, w_in, w_in_vres, mu_shift, mu_vres, w0, w_decay_up, a0, w_iclr_up, v0, w_vres_up,
           w_gate_up, k_k, k_a, r_k, lnx_g, lnx_b, attn_out_g, w_out, w_mem_q, w_mem_kv, w_mem_o,
           w_router, b_router, w_exp_gate, w_exp_up, w_exp_down, ln_g, ln_b):
    batch, seq, d = x.shape
    n = batch * seq
    mlen = mem.shape[1]
    zeros = jnp.zeros
    slopes = 2.0 ** (-8.0 * jnp.arange(1, N_HEADS + 1, dtype=F32) / N_HEADS)
    w_router_pad = jnp.pad(w_router, ((0, 0), (0, LANES - N_EXPERTS)))
    b_router_pad = jnp.pad(b_router.reshape(1, N_EXPERTS), ((0, 7), (0, LANES - N_EXPERTS)))
    mem2d = mem.reshape(batch * mlen, d)
    n_shift = w_in.shape[2] - 3 * WIDTH
    n_lora = n_shift - 3 * WIDTH
    n_vres = w_in_vres.shape[2]
    x2d = x.reshape(n, d)
    v_first = None
    for l in range(DEPTH):
        has_vres = l > 0
        w_extra = w_in_vres[l - 1] if has_vres else zeros((d, n_vres), F32)
        w_pad = jnp.concatenate(
            [w_in[l], w_extra, zeros((d, IN_COLS_PAD - w_in.shape[2] - n_vres), F32)], axis=1).astype(BF16)
        z_att, z_rkv, z_lora = _inproj(x2d, w_pad)
        o_att = _moba(z_att.reshape(batch, seq, 3 * WIDTH), slopes, attn_out_g[l].reshape(1, WIDTH), batch, seq)

        mu_rkv = mu_shift[l, :3 * WIDTH].reshape(1, 3 * WIDTH)
        mu_extra = mu_vres[l - 1] if has_vres else zeros((n_vres,), F32)
        mu_lora = jnp.concatenate(
            [mu_shift[l, 3 * WIDTH:], mu_extra, zeros((LORA_COLS - n_lora - n_vres,), F32)]).reshape(1, LORA_COLS)
        v0_l = v0[l - 1] if has_vres else zeros((WIDTH,), F32)
        vecs = jnp.stack([w0[l], a0[l], v0_l, k_k[l], k_a[l], r_k[l].reshape(WIDTH),
                          zeros((WIDTH,), F32), zeros((WIDTH,), F32)])
        n_dec, n_iclr = w_decay_up.shape[1], w_iclr_up.shape[1]
        wd = _pad_rows(w_decay_up[l], 0, LANES).astype(BF16)
        wa = _pad_rows(w_iclr_up[l], n_dec, LANES).astype(BF16)
        wg = w_gate_up[l].astype(BF16)
        wv = _pad_rows(w_vres_up[l - 1], 0, LANES).astype(BF16) if has_vres else None
        pre = _rwkv_pre(z_rkv.reshape(batch, seq, 3 * WIDTH), z_lora.reshape(batch, seq, LORA_COLS),
                        v_first, mu_rkv, mu_lora, vecs, wd, wa, wg, wv, batch, seq)
        if not has_vres:
            v_first = pre[5]
        lnx = jnp.concatenate([lnx_g[l][None], lnx_b[l][None], zeros((6, WIDTH), F32)])
        o_rwkv = _rwkv_scan(pre, lnx, batch, seq)

        ln = lambda j: jnp.concatenate([ln_g[l, j][None], ln_b[l, j][None], zeros((6, d), F32)])
        x1 = _outproj(x2d, o_att.reshape(n, WIDTH), o_rwkv.reshape(n, WIDTH), w_out[l].astype(BF16), ln(0))

        kv = _kvproj(mem2d, w_mem_kv[l].astype(BF16)).reshape(batch, mlen, 2 * d)
        x2, route = _memattn(x1.reshape(batch, seq, d), kv, w_mem_q[l].astype(BF16),
                             w_mem_o[l].astype(BF16), ln(1), w_router_pad, b_router_pad)
        x2d = _moe(x2.reshape(n, d), route.reshape(n, LANES),
                   w_exp_gate[l].astype(BF16), w_exp_up[l].astype(BF16), w_exp_down[l].astype(BF16), ln(2))
    return x2d.reshape(batch, seq, d)
```

```python
import functools

import jax
import jax.numpy as jnp
from jax import lax
from jax.experimental import pallas as pl
from jax.experimental.pallas import tpu as pltpu

F32 = jnp.float32
BF16 = jnp.bfloat16
I32 = jnp.int32
HIGHEST = lax.Precision.HIGHEST

D_MODEL = 1024
DEPTH = 2
N_HEADS = 8
HEAD_DIM = 64
WIDTH = N_HEADS * HEAD_DIM
N_PAIRS = N_HEADS // 2
LANES = 128
MOBA_BLOCK = 256
MOBA_TOP = 3
LORA_COLS = 384
IN_COLS_PAD = 3 * WIDTH + 3 * WIDTH + LORA_COLS
RWKV_GN_EPS = 64e-5
N_MEM_HEADS = 4
MEM_HEAD_DIM = D_MODEL // N_MEM_HEADS
N_EXPERTS = 32
N_GROUPS = 8
D_EXPERT = 512
LN_EPS = 1e-5
RMS_EPS = 1e-6
NEG = -1e30
ALPHA = (2 * DEPTH) ** 0.25
CHUNK = 64
MOE_ROWS = 256

VMEM_LIMIT = 48 * 1024 * 1024

NT_DIMS = (((1,), (1,)), ((), ()))
TN_DIMS = (((0,), (0,)), ((), ()))


def _cparams(sem):
    return pltpu.CompilerParams(dimension_semantics=sem, vmem_limit_bytes=VMEM_LIMIT)


def _split(a):
    hi = a.astype(BF16)
    lo = (a - hi.astype(F32)).astype(BF16)
    return hi, lo


def _dot3(a, b, dims=(((1,), (0,)), ((), ()))):
    ah, al = _split(a)
    bh, bl = _split(b)
    d = functools.partial(lax.dot_general, dimension_numbers=dims, preferred_element_type=F32)
    return d(ah, bh) + (d(al, bh) + d(ah, bl))


def _dot1(a, b, dims=(((1,), (0,)), ((), ()))):
    return lax.dot_general(a.astype(BF16), b.astype(BF16), dims, preferred_element_type=F32)


def _layer_norm(y, g, b):
    mu = jnp.mean(y, -1, keepdims=True)
    d = y - mu
    var = jnp.mean(d * d, -1, keepdims=True)
    return d * lax.rsqrt(var + LN_EPS) * g + b


def _sigmoid(x):
    return 1.0 / (1.0 + jnp.exp(-x))


def _inproj_kernel(x_ref, w_ref, att_ref, rkv_ref, lora_ref):
    xb = x_ref[...].astype(BF16)
    att_ref[...] = jnp.dot(xb, w_ref[:, 0:3 * WIDTH], preferred_element_type=F32)
    rkv_ref[...] = jnp.dot(xb, w_ref[:, 3 * WIDTH:6 * WIDTH], preferred_element_type=F32)
    lora_ref[...] = jnp.dot(xb, w_ref[:, 6 * WIDTH:IN_COLS_PAD], preferred_element_type=F32)


def _inproj(x2d, w_pad, tm=256):
    n = x2d.shape[0]
    return pl.pallas_call(
        _inproj_kernel,
        grid=(n // tm,),
        in_specs=[pl.BlockSpec((tm, D_MODEL), lambda i: (i, 0)),
                  pl.BlockSpec((D_MODEL, IN_COLS_PAD), lambda i: (0, 0))],
        out_specs=[pl.BlockSpec((tm, 3 * WIDTH), lambda i: (i, 0)),
                   pl.BlockSpec((tm, 3 * WIDTH), lambda i: (i, 0)),
                   pl.BlockSpec((tm, LORA_COLS), lambda i: (i, 0))],
        out_shape=[jax.ShapeDtypeStruct((n, 3 * WIDTH), F32),
                   jax.ShapeDtypeStruct((n, 3 * WIDTH), F32),
                   jax.ShapeDtypeStruct((n, LORA_COLS), F32)],
        compiler_params=_cparams(("parallel",)),
        name="inproj",
    )(x2d, w_pad)


def _moba_kernel(slopes_ref, q_ref, k_ref, v_ref, gain_ref, o_ref, kaug_ref, vb_ref, kmp_ref, *, seq):
    pair = pl.program_id(1)
    i = pl.program_id(2)
    nb = seq // MOBA_BLOCK
    blk = MOBA_BLOCK

    @pl.when(i == 0)
    def _():
        k = k_ref[...]
        rowblk = lax.shift_right_logical(lax.broadcasted_iota(I32, (seq, LANES), 0), 8)
        lane_s = lax.broadcasted_iota(I32, (seq, LANES), 1)
        r_i = lax.broadcasted_iota(I32, (LANES, seq), 0)
        c_b = lax.shift_right_logical(lax.broadcasted_iota(I32, (LANES, seq), 1), 8)
        for hh in range(2):
            ob = HEAD_DIM * (1 - hh)
            inhead = (lane_s >= HEAD_DIM * hh) & (lane_s < HEAD_DIM * (hh + 1))
            ind = (lane_s - ob) == rowblk
            kaug_ref[hh] = jnp.where(inhead, k, jnp.where(ind, 1.0, 0.0)).astype(BF16)
            indm = jnp.where((r_i - ob) == c_b, 1.0 / blk, 0.0)
            kmp_ref[hh] = jnp.dot(indm, k, precision=HIGHEST, preferred_element_type=F32)
        vb_ref[...] = v_ref[...].astype(BF16)

    def query_block(ii):
        q = q_ref[...] * (HEAD_DIM ** -0.5)
        row = lax.broadcasted_iota(I32, (blk, blk), 0)
        col = lax.broadcasted_iota(I32, (blk, blk), 1)
        rc = (row - col).astype(F32)
        lane = lax.broadcasted_iota(I32, (blk, LANES), 1)
        gain = gain_ref[...]
        outs = []
        for hh in range(2):
            slope = slopes_ref[2 * pair + hh]
            ob = HEAD_DIM * (1 - hh)
            inhead = (lane >= HEAD_DIM * hh) & (lane < HEAD_DIM * (hh + 1))
            jl = lane - ob
            past = (jl >= 0) & (jl < ii)
            bias = (-slope * blk) * (ii - jl).astype(F32)
            if ii > MOBA_TOP:
                qm = jnp.where(inhead, q, 0.0)
                gate = lax.dot_general(qm, kmp_ref[hh], NT_DIMS, precision=HIGHEST, preferred_element_type=F32)
                cnt = jnp.zeros((blk, LANES), I32)
                for jp in range(ii):
                    colv = gate[:, ob + jp:ob + jp + 1]
                    beats = (colv > gate) | ((colv == gate) & (jp < jl))
                    cnt = cnt + jnp.where(beats, 1, 0)
                bias = jnp.where(cnt < MOBA_TOP, bias, NEG)
            bias = jnp.where(past, bias, 0.0)
            qa = jnp.where(inhead, q, bias).astype(BF16)

            t_off = -slope * rc
            t_own = jnp.where(rc >= 0, t_off, NEG)
            nk = (ii + 1) * blk
            s = lax.dot_general(qa, kaug_ref[hh, 0:nk, :], NT_DIMS, preferred_element_type=F32)
            parts = [s[:, j * blk:(j + 1) * blk] + (t_own if j == ii else t_off) for j in range(ii + 1)]
            m = functools.reduce(jnp.maximum, [jnp.max(x, -1, keepdims=True) for x in parts])
            ps = [jnp.exp(x - m) for x in parts]
            l = functools.reduce(lambda a, b: a + b, [jnp.sum(x, -1, keepdims=True) for x in ps])
            p_all = jnp.concatenate([x.astype(BF16) for x in ps], axis=1)
            out = jnp.dot(p_all, vb_ref[0:nk, :], preferred_element_type=F32) / l
            ms = jnp.sum(jnp.where(inhead, out * out, 0.0), -1, keepdims=True) * (1.0 / HEAD_DIM)
            outs.append(out * lax.rsqrt(ms + RMS_EPS) * gain)
        o_ref[...] = jnp.where(lane < HEAD_DIM, outs[0], outs[1])

    for ii in range(nb):
        pl.when(i == ii)(functools.partial(query_block, ii))


def _moba(z_att, slopes, gain, batch, seq):
    nb = seq // MOBA_BLOCK
    grid_spec = pltpu.PrefetchScalarGridSpec(
        num_scalar_prefetch=1,
        grid=(batch, N_PAIRS, nb),
        in_specs=[pl.BlockSpec((None, MOBA_BLOCK, LANES), lambda b, p, i, s: (b, i, p)),
                  pl.BlockSpec((None, seq, LANES), lambda b, p, i, s: (b, 0, N_PAIRS + p)),
                  pl.BlockSpec((None, seq, LANES), lambda b, p, i, s: (b, 0, 2 * N_PAIRS + p)),
                  pl.BlockSpec((1, LANES), lambda b, p, i, s: (0, p))],
        out_specs=pl.BlockSpec((None, MOBA_BLOCK, LANES), lambda b, p, i, s: (b, i, p)),
        scratch_shapes=[pltpu.VMEM((2, seq, LANES), BF16),
                        pltpu.VMEM((seq, LANES), BF16),
                        pltpu.VMEM((2, LANES, LANES), F32)],
    )
    return pl.pallas_call(
        functools.partial(_moba_kernel, seq=seq),
        grid_spec=grid_spec,
        out_shape=jax.ShapeDtypeStruct((batch, seq, WIDTH), F32),
        compiler_params=_cparams(("parallel", "parallel", "arbitrary")),
        name="moba",
    )(slopes, z_att, z_att, z_att, gain)


def _dot2x(a, b_exact):
    ah, al = _split(a)
    return (jnp.dot(ah, b_exact, preferred_element_type=F32)
            + jnp.dot(al, b_exact, preferred_element_type=F32))


def _head_block_diag(scale):
    r = lax.shift_right_logical(lax.broadcasted_iota(I32, (LANES, LANES), 0), 6)
    c = lax.shift_right_logical(lax.broadcasted_iota(I32, (LANES, LANES), 1), 6)
    return jnp.where(r == c, scale, 0.0).astype(BF16)


def _softplus(x):
    return jnp.maximum(x, 0.0) + jnp.log(1.0 + jnp.exp(-jnp.abs(x)))


def _rwkv_chunks(items):
    c_ = CHUNK
    row = lax.broadcasted_iota(I32, (c_, c_), 0)
    col = lax.broadcasted_iota(I32, (c_, c_), 1)
    incl = row >= col
    strict = row > col
    lane = lax.broadcasted_iota(I32, (c_, LANES), 1)
    m0 = lane < HEAD_DIM
    tri = jnp.where(incl, 1.0, 0.0).astype(BF16)
    eye = jnp.where(row == col, 1.0, 0.0)
    splits = [_split(it[0]) for it in items]
    cs = [jnp.dot(tri, hi, preferred_element_type=F32) + jnp.dot(tri, lo, preferred_element_type=F32)
          for hi, lo in splits]
    base = []
    for (lw, r, k2, v, a, b), c in zip(items, cs):
        e_pos = jnp.exp(c)
        e_neg = jnp.exp(-c)
        at = a * jnp.exp(c - lw)
        bt = b * e_neg
        kt = k2 * e_neg
        rt = r * e_pos
        gc = e_pos[c_ - 1:c_, :]
        vs = jnp.concatenate([jnp.where(m0, v, 0.0), jnp.where(m0, 0.0, v)], axis=0)
        base.append(dict(at=at, bt=bt, kt=kt, rt=rt, gc=gc, bp=bt * gc, kp=kt * gc, vs=vs))
    heads = []
    for d in base:
        for hh in range(2):
            mh = m0 if hh == 0 else jnp.logical_not(m0)
            am = jnp.where(mh, d['at'], 0.0)
            rm = jnp.where(mh, d['rt'], 0.0)
            heads.append(dict(
                lab=jnp.where(strict, _dot3(am, d['bt'], NT_DIMS), 0.0),
                lak=jnp.where(strict, _dot3(am, d['kt'], NT_DIMS), 0.0),
                qb=jnp.where(incl, _dot3(rm, d['bt'], NT_DIMS), 0.0),
                qk=jnp.where(incl, _dot3(rm, d['kt'], NT_DIMS), 0.0)))
    ts = [eye + h['lab'] for h in heads]
    lps = [h['lab'] for h in heads]
    for _ in range(5):
        lps = [_dot3(lp, lp) for lp in lps]
        ts = [t + _dot3(t, lp) for t, lp in zip(ts, lps)]
    outs = []
    for i, d in enumerate(base):
        h0, h1 = heads[2 * i], heads[2 * i + 1]
        tcat = jnp.concatenate([ts[2 * i], ts[2 * i + 1]], axis=1)
        qcat = jnp.concatenate([h0['qb'], h1['qb']], axis=1)
        p = _dot3(jnp.concatenate([h0['lak'], h1['lak']], axis=1), d['vs'])
        yk = _dot3(jnp.concatenate([h0['qk'], h1['qk']], axis=1), d['vs'])
        grow = jnp.broadcast_to(d['gc'], (c_, LANES))
        outs.append((d['at'], p, tcat, d['bp'], d['kp'], grow, d['rt'], qcat, yk))
    return outs


def _rwkv_pre_kernel(*refs, has_vres, tb):
    if has_vres:
        (rkv_ref, lora_ref, vf_ref, mu_rkv_ref, mu_lora_ref, vec_ref, wd_ref, wa_ref, wg_ref, wv_ref,
         at_ref, p_ref, tc_ref, bp_ref, kp_ref, v_ref, gr_ref, rt_ref, qc_ref, yk_ref, bonus_ref, g_ref,
         carry_ref, carryl_ref) = refs
    else:
        (rkv_ref, lora_ref, mu_rkv_ref, mu_lora_ref, vec_ref, wd_ref, wa_ref, wg_ref,
         at_ref, p_ref, tc_ref, bp_ref, kp_ref, v_ref, gr_ref, rt_ref, qc_ref, yk_ref, bonus_ref, g_ref,
         carry_ref, carryl_ref) = refs
    t_idx = pl.program_id(1)

    @pl.when(t_idx == 0)
    def _():
        carry_ref[...] = jnp.zeros_like(carry_ref)
        carryl_ref[...] = jnp.zeros_like(carryl_ref)

    def token_shift(z, cref, mu):
        first = lax.broadcasted_iota(I32, z.shape, 0) == 0
        prev = jnp.where(first, cref[0:1, :], pltpu.roll(z, 1, 0))
        cref[0:1, :] = z[tb - 1:tb, :]
        return z + (prev - z) * mu

    zs = token_shift(rkv_ref[...], carry_ref, mu_rkv_ref[...])
    zl = token_shift(lora_ref[...], carryl_ref, mu_lora_ref[...])
    l0 = zl[:, 0:LANES]
    w0, a0, v0 = vec_ref[0:1, :], vec_ref[1:2, :], vec_ref[2:3, :]
    k_k, k_a, r_k = vec_ref[3:4, :], vec_ref[4:5, :], vec_ref[5:6, :]
    dw = _dot1(jnp.tanh(l0), wd_ref[...])
    lw_all = -jnp.exp(-_softplus(-(w0 + dw)) - 0.5)
    a_lr = _sigmoid(a0 + _dot1(l0, wa_ref[...]))
    g_ref[...] = _dot1(_sigmoid(zl[:, LANES:2 * LANES]), wg_ref[...])
    r_all = zs[:, 0:WIDTH]
    k_all = zs[:, WIDTH:2 * WIDTH]
    v_all = zs[:, 2 * WIDTH:3 * WIDTH]
    if has_vres:
        mix = _sigmoid(v0 + _dot1(zl[:, 2 * LANES:3 * LANES], wv_ref[...]))
        v_all = v_all + (vf_ref[...] - v_all) * mix
    v_ref[...] = v_all
    bd = _head_block_diag(1.0)
    items, where = [], []
    for pp in range(N_PAIRS):
        sl = slice(pp * LANES, (pp + 1) * LANES)
        r, k, v = r_all[:, sl], k_all[:, sl], v_all[:, sl]
        alr = a_lr[:, sl]
        kk = k * k_k[:, sl]
        ss = _dot2x(kk * kk, bd)
        kk = kk * lax.rsqrt(jnp.maximum(ss, 1e-24))
        k2 = k * (1.0 + (alr - 1.0) * k_a[:, sl])
        bonus_ref[:, sl] = _dot2x(r * k2 * r_k[:, sl], bd) * v
        a = -kk
        b = kk * alr
        lw = lw_all[:, sl]
        for ci in range(tb // CHUNK):
            rs = slice(ci * CHUNK, (ci + 1) * CHUNK)
            items.append((lw[rs], r[rs], k2[rs], v[rs], a[rs], b[rs]))
            where.append((rs, sl))
    for (rs, sl), outs in zip(where, _rwkv_chunks(items)):
        for ref, val in zip((at_ref, p_ref, tc_ref, bp_ref, kp_ref, gr_ref, rt_ref, qc_ref, yk_ref), outs):
            ref[rs, sl] = val


def _rwkv_pre(z_rkv, z_lora, v_first, mu_rkv, mu_lora, vecs, wd, wa, wg, wv, batch, seq, tb=128):
    has_vres = v_first is not None
    big = lambda w: pl.BlockSpec((None, tb, w), lambda b, t: (b, t, 0))
    full = lambda shape: pl.BlockSpec(shape, lambda b, t: (0,) * len(shape))
    in_specs = [big(3 * WIDTH), big(LORA_COLS)] + ([big(WIDTH)] if has_vres else [])
    in_specs += [full((1, 3 * WIDTH)), full((1, LORA_COLS)), full((8, WIDTH)),
                 full((LANES, WIDTH)), full((LANES, WIDTH)), full((LANES, WIDTH))]
    args = [z_rkv, z_lora] + ([v_first] if has_vres else []) + [mu_rkv, mu_lora, vecs, wd, wa, wg]
    if has_vres:
        in_specs.append(full((LANES, WIDTH)))
        args.append(wv)
    n_out = 12
    return pl.pallas_call(
        functools.partial(_rwkv_pre_kernel, has_vres=has_vres, tb=tb),
        grid=(batch, seq // tb),
        in_specs=in_specs,
        out_specs=[big(WIDTH)] * n_out,
        out_shape=[jax.ShapeDtypeStruct((batch, seq, WIDTH), F32)] * n_out,
        scratch_shapes=[pltpu.VMEM((8, 3 * WIDTH), F32), pltpu.VMEM((8, LORA_COLS), F32)],
        compiler_params=_cparams(("parallel", "arbitrary")),
        name="rwkv_pre",
    )(*args)


def _rwkv_scan_kernel(at_ref, p_ref, tc_ref, bp_ref, kp_ref, v_ref, gr_ref, rt_ref, qc_ref, yk_ref,
                      bonus_ref, g_ref, lnx_ref, o_ref, st_ref, *, ts):
    t_idx = pl.program_id(1)

    @pl.when(t_idx == 0)
    def _():
        st_ref[...] = jnp.zeros_like(st_ref)

    lane = lax.broadcasted_iota(I32, (CHUNK, LANES), 1)
    m0 = lane < HEAD_DIM
    bdmask = (lax.shift_right_logical(lax.broadcasted_iota(I32, (LANES, LANES), 0), 6)
              == lax.shift_right_logical(lax.broadcasted_iota(I32, (LANES, LANES), 1), 6))
    ones_t = jnp.full((CHUNK, LANES), 1.0 / CHUNK, BF16)

    def stack(x):
        return jnp.concatenate([jnp.where(m0, x, 0.0), jnp.where(m0, 0.0, x)], axis=0)

    tn = functools.partial(lax.dot_general, dimension_numbers=TN_DIMS, preferred_element_type=F32)
    pairs = [slice(pp * LANES, (pp + 1) * LANES) for pp in range(N_PAIRS)]
    chunks = [slice(ci * CHUNK, (ci + 1) * CHUNK) for ci in range(ts // CHUNK)]

    gcols, kvs = {}, {}
    for ci, rs in enumerate(chunks):
        for pp, sl in enumerate(pairs):
            g = gr_ref[rs, sl]
            gh, gl = _split(g)
            gl2 = (g - gh.astype(F32) - gl.astype(F32)).astype(BF16)
            gcols[ci, pp] = tn(gh, ones_t) + tn(gl, ones_t) + tn(gl2, ones_t)
            kvs[ci, pp] = jnp.where(bdmask, _dot3(kp_ref[rs, sl], v_ref[rs, sl], TN_DIMS), 0.0)

    states = [st_ref[pp] for pp in range(N_PAIRS)]
    for ci, rs in enumerate(chunks):
        xs = [_dot3(at_ref[rs, sl], states[pp]) + p_ref[rs, sl] for pp, sl in enumerate(pairs)]
        us = [_dot3(tc_ref[rs, sl], stack(xs[pp])) for pp, sl in enumerate(pairs)]
        new = [gcols[ci, pp] * states[pp]
               + (jnp.where(bdmask, _dot3(bp_ref[rs, sl], us[pp], TN_DIMS), 0.0) + kvs[ci, pp])
               for pp, sl in enumerate(pairs)]
        for pp, sl in enumerate(pairs):
            o_ref[rs, sl] = (_dot3(rt_ref[rs, sl], states[pp]) + _dot3(qc_ref[rs, sl], stack(us[pp]))
                             + yk_ref[rs, sl])
        states = new
    for pp in range(N_PAIRS):
        st_ref[pp] = states[pp]

    bd = _head_block_diag(1.0 / HEAD_DIM)
    for pp in range(N_PAIRS):
        sl = slice(pp * LANES, (pp + 1) * LANES)
        y = o_ref[:, sl]
        mu = _dot2x(y, bd)
        d = y - mu
        var = _dot2x(d * d, bd)
        yn = d * lax.rsqrt(var + RWKV_GN_EPS) * lnx_ref[0:1, sl] + lnx_ref[1:2, sl]
        o_ref[:, sl] = (yn + bonus_ref[:, sl]) * g_ref[:, sl]


def _rwkv_scan(pre, lnx, batch, seq, ts=256):
    big = pl.BlockSpec((None, ts, WIDTH), lambda b, t: (b, t, 0))
    return pl.pallas_call(
        functools.partial(_rwkv_scan_kernel, ts=ts),
        grid=(batch, seq // ts),
        in_specs=[big] * 12 + [pl.BlockSpec((8, WIDTH), lambda b, t: (0, 0))],
        out_specs=big,
        out_shape=jax.ShapeDtypeStruct((batch, seq, WIDTH), F32),
        scratch_shapes=[pltpu.VMEM((N_PAIRS, LANES, LANES), F32)],
        compiler_params=_cparams(("parallel", "arbitrary")),
        name="rwkv_scan",
    )(*pre, lnx)


def _outproj_kernel(x_ref, oa_ref, orw_ref, w_ref, ln_ref, o_ref):
    y = (jnp.dot(oa_ref[...].astype(BF16), w_ref[0:WIDTH, :], preferred_element_type=F32)
         + jnp.dot(orw_ref[...].astype(BF16), w_ref[WIDTH:2 * WIDTH, :], preferred_element_type=F32))
    o_ref[...] = _layer_norm(ALPHA * x_ref[...] + y, ln_ref[0:1, :], ln_ref[1:2, :])


def _outproj(x2d, o_att, o_rwkv, w_out, ln, tm=512):
    n = x2d.shape[0]
    return pl.pallas_call(
        _outproj_kernel,
        grid=(n // tm,),
        in_specs=[pl.BlockSpec((tm, D_MODEL), lambda i: (i, 0)),
                  pl.BlockSpec((tm, WIDTH), lambda i: (i, 0)),
                  pl.BlockSpec((tm, WIDTH), lambda i: (i, 0)),
                  pl.BlockSpec((2 * WIDTH, D_MODEL), lambda i: (0, 0)),
                  pl.BlockSpec((8, D_MODEL), lambda i: (0, 0))],
        out_specs=pl.BlockSpec((tm, D_MODEL), lambda i: (i, 0)),
        out_shape=jax.ShapeDtypeStruct((n, D_MODEL), F32),
        compiler_params=_cparams(("parallel",)),
        name="outproj_ln",
    )(x2d, o_att, o_rwkv, w_out, ln)


def _kvproj_kernel(m_ref, w_ref, o_ref):
    o_ref[...] = jnp.dot(m_ref[...].astype(BF16), w_ref[...], preferred_element_type=F32).astype(BF16)


def _kvproj(mem2d, w_kv, tm=256):
    n = mem2d.shape[0]
    return pl.pallas_call(
        _kvproj_kernel,
        grid=(n // tm,),
        in_specs=[pl.BlockSpec((tm, D_MODEL), lambda i: (i, 0)),
                  pl.BlockSpec((D_MODEL, 2 * D_MODEL), lambda i: (0, 0))],
        out_specs=pl.BlockSpec((tm, 2 * D_MODEL), lambda i: (i, 0)),
        out_shape=jax.ShapeDtypeStruct((n, 2 * D_MODEL), BF16),
        compiler_params=_cparams(("parallel",)),
        name="kvproj",
    )(mem2d, w_kv)


def _route_lanes(logits, b_router):
    lane = lax.broadcasted_iota(I32, logits.shape, 1)
    lane_f = lane.astype(F32)
    valid = lane < N_EXPERTS
    s = _sigmoid(logits)
    sel = jnp.where(valid, s + b_router, NEG)

    def partner(x, bit):
        return jnp.where((lane & bit) == 0, pltpu.roll(x, LANES - bit, 1), pltpu.roll(x, bit, 1))

    p1 = partner(sel, 1)
    hi1, lo1 = jnp.maximum(sel, p1), jnp.minimum(sel, p1)
    hi2, lo2 = partner(hi1, 2), partner(lo1, 2)
    gsum = jnp.maximum(hi1, hi2) + jnp.maximum(jnp.minimum(hi1, hi2), jnp.maximum(lo1, lo2))
    gmax = jnp.max(gsum, -1, keepdims=True)
    grp_f = lax.shift_right_logical(lane, 2).astype(F32)
    g_best = jnp.min(jnp.where((gsum == gmax) & valid, grp_f, 1e9), -1, keepdims=True)
    in_grp = (grp_f == g_best) & valid
    masked = jnp.where(in_grp, sel, NEG)
    v1 = jnp.max(masked, -1, keepdims=True)
    e1 = jnp.min(jnp.where(in_grp & (masked == v1), lane_f, 1e9), -1, keepdims=True)
    rest = in_grp & (lane_f != e1)
    masked2 = jnp.where(rest, sel, NEG)
    v2 = jnp.max(masked2, -1, keepdims=True)
    e2 = jnp.min(jnp.where(rest & (masked2 == v2), lane_f, 1e9), -1, keepdims=True)
    w1 = jnp.sum(jnp.where(lane_f == e1, s, 0.0), -1, keepdims=True)
    w2 = jnp.sum(jnp.where(lane_f == e2, s, 0.0), -1, keepdims=True)
    tot = w1 + w2
    out = jnp.where(lane == 0, e1, jnp.where(lane == 1, e2, jnp.where(lane == 2, w1 / tot, w2 / tot)))
    return jnp.where(lane < 4, out, 0.0)


def _memattn_kernel(x_ref, kv_ref, wq_ref, wo_ref, ln_ref, wr_ref, br_ref, o_ref, route_ref):
    x = x_ref[...]
    q = jnp.dot(x.astype(BF16), wq_ref[...], preferred_element_type=F32) * (MEM_HEAD_DIM ** -0.5)
    heads = []
    for h in range(N_MEM_HEADS):
        sl = slice(h * MEM_HEAD_DIM, (h + 1) * MEM_HEAD_DIM)
        kh = kv_ref[:, sl]
        vh = kv_ref[:, D_MODEL + h * MEM_HEAD_DIM:D_MODEL + (h + 1) * MEM_HEAD_DIM]
        s = lax.dot_general(q[:, sl].astype(BF16), kh, NT_DIMS, preferred_element_type=F32)
        p = jnp.exp(s - jnp.max(s, -1, keepdims=True))
        o = jnp.dot(p.astype(BF16), vh, preferred_element_type=F32)
        heads.append(o / jnp.sum(p, -1, keepdims=True))
    o = jnp.concatenate(heads, axis=-1).astype(BF16)
    y = jnp.dot(o, wo_ref[...], preferred_element_type=F32)
    x2 = _layer_norm(ALPHA * x + y, ln_ref[0:1, :], ln_ref[1:2, :])
    o_ref[...] = x2
    logits = jnp.dot(x2, wr_ref[...], precision=HIGHEST, preferred_element_type=F32)
    route_ref[...] = _route_lanes(logits, br_ref[0:1, :])


def _memattn(x3d, kv3d, wq, wo, ln, w_router_pad, b_router_pad, tm=512):
    batch, seq, _ = x3d.shape
    mlen = kv3d.shape[1]
    full = lambda shape: pl.BlockSpec(shape, lambda b, t: (0,) * len(shape))
    return pl.pallas_call(
        _memattn_kernel,
        grid=(batch, seq // tm),
        in_specs=[pl.BlockSpec((None, tm, D_MODEL), lambda b, t: (b, t, 0)),
                  pl.BlockSpec((None, mlen, 2 * D_MODEL), lambda b, t: (b, 0, 0)),
                  full((D_MODEL, D_MODEL)), full((D_MODEL, D_MODEL)), full((8, D_MODEL)),
                  full((D_MODEL, LANES)), full((8, LANES))],
        out_specs=[pl.BlockSpec((None, tm, D_MODEL), lambda b, t: (b, t, 0)),
                   pl.BlockSpec((None, tm, LANES), lambda b, t: (b, t, 0))],
        out_shape=[jax.ShapeDtypeStruct((batch, seq, D_MODEL), F32),
                   jax.ShapeDtypeStruct((batch, seq, LANES), F32)],
        compiler_params=_cparams(("parallel", "parallel")),
        name="memattn_ln",
    )(x3d, kv3d, wq, wo, ln, w_router_pad, b_router_pad)


def _row_gather_kernel(nused_ref, idx_ref, x_hbm, o_ref, sem):
    i = pl.program_id(0)
    rows = o_ref.shape[0]

    @pl.when(i < nused_ref[0])
    def _():
        def copy(r):
            return pltpu.make_async_copy(x_hbm.at[pl.ds(idx_ref[0, 0, r], 1)], o_ref.at[pl.ds(r, 1)], sem)

        def start(r, c):
            copy(r).start()
            return c

        def wait(r, c):
            copy(r).wait()
            return c

        lax.fori_loop(0, rows, start, 0, unroll=8)
        lax.fori_loop(0, rows, wait, 0, unroll=8)

    @pl.when(i >= nused_ref[0])
    def _():
        o_ref[...] = jnp.zeros_like(o_ref)


def _row_gather(x2d, row_tok3, n_used, rows):
    n_blk = row_tok3.shape[0]
    grid_spec = pltpu.PrefetchScalarGridSpec(
        num_scalar_prefetch=1,
        grid=(n_blk,),
        in_specs=[pl.BlockSpec((1, 1, rows), lambda i, nu: (i, 0, 0), memory_space=pltpu.SMEM),
                  pl.BlockSpec(memory_space=pl.ANY)],
        out_specs=pl.BlockSpec((rows, D_MODEL), lambda i, nu: (i, 0)),
        scratch_shapes=[pltpu.SemaphoreType.DMA(())],
    )
    return pl.pallas_call(
        _row_gather_kernel,
        grid_spec=grid_spec,
        out_shape=jax.ShapeDtypeStruct((n_blk * rows, D_MODEL), F32),
        compiler_params=_cparams(("arbitrary",)),
        name="moe_gather",
    )(n_used, row_tok3, x2d)


def _experts_kernel(be_ref, nused_ref, x_ref, wg_ref, wu_ref, wd_ref, o_ref):
    i = pl.program_id(0)

    @pl.when(i < nused_ref[0])
    def _():
        xb = x_ref[...].astype(BF16)
        hg = jnp.dot(xb, wg_ref[...], preferred_element_type=F32)
        hu = jnp.dot(xb, wu_ref[...], preferred_element_type=F32)
        h = hg * _sigmoid(hg) * hu
        o_ref[...] = jnp.dot(h.astype(BF16), wd_ref[...], preferred_element_type=F32)

    @pl.when(i >= nused_ref[0])
    def _():
        o_ref[...] = jnp.zeros_like(o_ref)


def _experts(xs, blk_expert, n_used, w_gate, w_up, w_down, rows):
    n_blk = xs.shape[0] // rows
    row_map = lambda i, be, nu: (jnp.minimum(i, nu[0] - 1), 0)
    w_map = lambda i, be, nu: (be[jnp.minimum(i, nu[0] - 1)], 0, 0)
    grid_spec = pltpu.PrefetchScalarGridSpec(
        num_scalar_prefetch=2,
        grid=(n_blk,),
        in_specs=[pl.BlockSpec((rows, D_MODEL), row_map),
                  pl.BlockSpec((None, D_MODEL, D_EXPERT), w_map),
                  pl.BlockSpec((None, D_MODEL, D_EXPERT), w_map),
                  pl.BlockSpec((None, D_EXPERT, D_MODEL), w_map)],
        out_specs=pl.BlockSpec((rows, D_MODEL), lambda i, be, nu: (i, 0)),
    )
    return pl.pallas_call(
        _experts_kernel,
        grid_spec=grid_spec,
        out_shape=jax.ShapeDtypeStruct((n_blk * rows, D_MODEL), F32),
        compiler_params=_cparams(("arbitrary",)),
        name="moe_experts",
    )(blk_expert, n_used, xs, w_gate, w_up, w_down)


def _combine_kernel(dest_ref, x_ref, gate_ref, ln_ref, y_hbm, o_ref, buf_ref, sem):
    tm = x_ref.shape[0]

    def copy(slot, t):
        src = y_hbm.at[pl.ds(dest_ref[0, 0, slot * tm + t], 1)]
        return pltpu.make_async_copy(src, buf_ref.at[slot, pl.ds(t, 1)], sem)

    def start(t, c):
        copy(0, t).start()
        copy(1, t).start()
        return c

    def wait(t, c):
        copy(0, t).wait()
        copy(1, t).wait()
        return c

    lax.fori_loop(0, tm, start, 0, unroll=8)
    lax.fori_loop(0, tm, wait, 0, unroll=8)
    y = gate_ref[:, 0:1] * buf_ref[0] + gate_ref[:, 1:2] * buf_ref[1]
    o_ref[...] = _layer_norm(ALPHA * x_ref[...] + y, ln_ref[0:1, :], ln_ref[1:2, :])


def _combine(x2d, gates, dest3, ys, ln, tm=256):
    n = x2d.shape[0]
    return pl.pallas_call(
        _combine_kernel,
        grid=(n // tm,),
        in_specs=[pl.BlockSpec((1, 1, 2 * tm), lambda i: (i, 0, 0), memory_space=pltpu.SMEM),
                  pl.BlockSpec((tm, D_MODEL), lambda i: (i, 0)),
                  pl.BlockSpec((tm, 2), lambda i: (i, 0)),
                  pl.BlockSpec((8, D_MODEL), lambda i: (0, 0)),
                  pl.BlockSpec(memory_space=pl.ANY)],
        out_specs=pl.BlockSpec((tm, D_MODEL), lambda i: (i, 0)),
        out_shape=jax.ShapeDtypeStruct((n, D_MODEL), F32),
        scratch_shapes=[pltpu.VMEM((2, tm, D_MODEL), F32), pltpu.SemaphoreType.DMA(())],
        compiler_params=_cparams(("arbitrary",)),
        name="moe_combine_ln",
    )(dest3, x2d, gates, ln, ys)


def _dispatch_plan(e_idx, rows):
    n = e_idx.shape[0]
    a = 2 * n
    eid = e_idx.reshape(-1)
    onehot = (eid[:, None] == jnp.arange(N_EXPERTS, dtype=I32)[None, :]).astype(I32)
    csum = jnp.cumsum(onehot, axis=0)
    rank = jnp.take_along_axis(csum - onehot, eid[:, None], axis=1)[:, 0]
    counts = csum[-1]
    pcounts = (counts + rows - 1) // rows * rows
    pend = jnp.cumsum(pcounts)
    poff = pend - pcounts
    dest = (poff[eid] + rank).astype(I32)
    m = a + N_EXPERTS * rows
    n_blk = m // rows
    tok = (jnp.arange(a, dtype=I32) // 2)
    row_tok = jnp.zeros((m,), I32).at[dest].set(tok)
    blk_expert = jnp.minimum(jnp.searchsorted(pend, jnp.arange(n_blk, dtype=I32) * rows, side='right'),
                             N_EXPERTS - 1).astype(I32)
    n_used = (pend[-1] // rows).astype(I32).reshape(1)
    return dest, row_tok.reshape(n_blk, 1, rows), blk_expert, n_used


def _moe(x2, route, w_gate, w_up, w_down, ln, rows=MOE_ROWS, tm=256):
    n = x2.shape[0]
    e_idx = route[:, 0:2].astype(I32)
    gates = route[:, 2:4]
    dest, row_tok3, blk_expert, n_used = _dispatch_plan(e_idx, rows)
    xs = _row_gather(x2, row_tok3, n_used, rows)
    ys = _experts(xs, blk_expert, n_used, w_gate, w_up, w_down, rows)
    dest3 = dest.reshape(n // tm, tm, 2).transpose(0, 2, 1).reshape(n // tm, 1, 2 * tm)
    return _combine(x2, gates, dest3, ys, ln, tm)


def _pad_rows(w, rows_before, total):
    return jnp.pad(w, ((rows_before, total - rows_before - w.shape[0]), (0, 0)))


def kernel(x, mem, w_in, w_in_vres, mu_shift, mu_vres, w0, w_decay_up, a0, w_iclr_up, v0, w_vres_up,
           w_gate_up, k_k, k_a, r_k, lnx_g, lnx_b, attn_out_g, w_out, w_mem_q, w_mem_kv, w_mem_o,
           w_router, b_router, w_exp_gate, w_exp_up, w_exp_down, ln_g, ln_b):
    batch, seq, d = x.shape
    n = batch * seq
    mlen = mem.shape[1]
    zeros = jnp.zeros
    slopes = 2.0 ** (-8.0 * jnp.arange(1, N_HEADS + 1, dtype=F32) / N_HEADS)
    w_router_pad = jnp.pad(w_router, ((0, 0), (0, LANES - N_EXPERTS)))
    b_router_pad = jnp.pad(b_router.reshape(1, N_EXPERTS), ((0, 7), (0, LANES - N_EXPERTS)))
    mem2d = mem.reshape(batch * mlen, d)
    n_shift = w_in.shape[2] - 3 * WIDTH
    n_lora = n_shift - 3 * WIDTH
    n_vres = w_in_vres.shape[2]
    x2d = x.reshape(n, d)
    v_first = None
    for l in range(DEPTH):
        has_vres = l > 0
        w_extra = w_in_vres[l - 1] if has_vres else zeros((d, n_vres), F32)
        w_pad = jnp.concatenate(
            [w_in[l], w_extra, zeros((d, IN_COLS_PAD - w_in.shape[2] - n_vres), F32)], axis=1).astype(BF16)
        z_att, z_rkv, z_lora = _inproj(x2d, w_pad)
        o_att = _moba(z_att.reshape(batch, seq, 3 * WIDTH), slopes, attn_out_g[l].reshape(1, WIDTH), batch, seq)

        mu_rkv = mu_shift[l, :3 * WIDTH].reshape(1, 3 * WIDTH)
        mu_extra = mu_vres[l - 1] if has_vres else zeros((n_vres,), F32)
        mu_lora = jnp.concatenate(
            [mu_shift[l, 3 * WIDTH:], mu_extra, zeros((LORA_COLS - n_lora - n_vres,), F32)]).reshape(1, LORA_COLS)
        v0_l = v0[l - 1] if has_vres else zeros((WIDTH,), F32)
        vecs = jnp.stack([w0[l], a0[l], v0_l, k_k[l], k_a[l], r_k[l].reshape(WIDTH),
                          zeros((WIDTH,), F32), zeros((WIDTH,), F32)])
        n_dec, n_iclr = w_decay_up.shape[1], w_iclr_up.shape[1]
        wd = _pad_rows(w_decay_up[l], 0, LANES).astype(BF16)
        wa = _pad_rows(w_iclr_up[l], n_dec, LANES).astype(BF16)
        wg = w_gate_up[l].astype(BF16)
        wv = _pad_rows(w_vres_up[l - 1], 0, LANES).astype(BF16) if has_vres else None
        pre = _rwkv_pre(z_rkv.reshape(batch, seq, 3 * WIDTH), z_lora.reshape(batch, seq, LORA_COLS),
                        v_first, mu_rkv, mu_lora, vecs, wd, wa, wg, wv, batch, seq)
        if not has_vres:
            v_first = pre[5]
        lnx = jnp.concatenate([lnx_g[l][None], lnx_b[l][None], zeros((6, WIDTH), F32)])
        o_rwkv = _rwkv_scan(pre, lnx, batch, seq)

        ln = lambda j: jnp.concatenate([ln_g[l, j][None], ln_b[l, j][None], zeros((6, d), F32)])
        x1 = _outproj(x2d, o_att.reshape(n, WIDTH), o_rwkv.reshape(n, WIDTH), w_out[l].astype(BF16), ln(0))

        kv = _kvproj(mem2d, w_mem_kv[l].astype(BF16)).reshape(batch, mlen, 2 * d)
        x2, route = _memattn(x1.reshape(batch, seq, d), kv, w_mem_q[l].astype(BF16),
                             w_mem_o[l].astype(BF16), ln(1), w_router_pad, b_router_pad)
        x2d = _moe(x2.reshape(n, d), route.reshape(n, LANES),
                   w_exp_gate[l].astype(BF16), w_exp_up[l].astype(BF16), w_exp_down[l].astype(BF16), ln(2))
    return x2d.reshape(batch, seq, d)
```

```python
import functools

import jax
import jax.numpy as jnp
from jax import lax
from jax.experimental import pallas as pl
from jax.experimental.pallas import tpu as pltpu

F32 = jnp.float32
BF16 = jnp.bfloat16
I32 = jnp.int32

D_MODEL = 1024
DEPTH = 2
N_HEADS = 8
HEAD_DIM = 64
WIDTH = N_HEADS * HEAD_DIM
N_PAIRS = N_HEADS // 2
LANES = 128
MOBA_BLOCK = 256
MOBA_TOP = 3
GATE_ROWS = 16
POS_BLK = 16
POS_IN = 17
LORA_COLS = 384
IN_COLS_PAD = 3 * WIDTH + 3 * WIDTH + LORA_COLS
RWKV_GN_EPS = 64e-5
N_MEM_HEADS = 4
MEM_HEAD_DIM = D_MODEL // N_MEM_HEADS
N_EXPERTS = 32
N_GROUPS = 8
D_EXPERT = 512
LN_EPS = 1e-5
RMS_EPS = 1e-6
NEG = -1e30
ALPHA = (2 * DEPTH) ** 0.25
CHUNK = 64
MOE_ROWS = 256

VMEM_LIMIT = 48 * 1024 * 1024

NT_DIMS = (((1,), (1,)), ((), ()))
TN_DIMS = (((0,), (0,)), ((), ()))


def _cparams(sem):
    return pltpu.CompilerParams(dimension_semantics=sem, vmem_limit_bytes=VMEM_LIMIT)


def _split(a):
    hi = a.astype(BF16)
    lo = (a - hi.astype(F32)).astype(BF16)
    return hi, lo


def _dot1(a, b, dims=(((1,), (0,)), ((), ()))):
    return lax.dot_general(a.astype(BF16), b.astype(BF16), dims, preferred_element_type=F32)


def _layer_norm(y, g, b):
    mu = jnp.mean(y, -1, keepdims=True)
    d = y - mu
    var = jnp.mean(d * d, -1, keepdims=True)
    return d * lax.rsqrt(var + LN_EPS) * g + b


def _sigmoid(x):
    return 1.0 / (1.0 + jnp.exp(-x))


def _inproj_kernel(x_ref, w_ref, att_ref, rkv_ref, lora_ref):
    xb = x_ref[...].astype(BF16)
    att_ref[...] = jnp.dot(xb, w_ref[:, 0:3 * WIDTH], preferred_element_type=F32)
    rkv_ref[...] = jnp.dot(xb, w_ref[:, 3 * WIDTH:6 * WIDTH], preferred_element_type=F32)
    lora_ref[...] = jnp.dot(xb, w_ref[:, 6 * WIDTH:IN_COLS_PAD], preferred_element_type=F32)


def _inproj(x2d, w_pad, tm=256):
    n = x2d.shape[0]
    return pl.pallas_call(
        _inproj_kernel,
        grid=(n // tm,),
        in_specs=[pl.BlockSpec((tm, D_MODEL), lambda i: (i, 0)),
                  pl.BlockSpec((D_MODEL, IN_COLS_PAD), lambda i: (0, 0))],
        out_specs=[pl.BlockSpec((tm, 3 * WIDTH), lambda i: (i, 0)),
                   pl.BlockSpec((tm, 3 * WIDTH), lambda i: (i, 0)),
                   pl.BlockSpec((tm, LORA_COLS), lambda i: (i, 0))],
        out_shape=[jax.ShapeDtypeStruct((n, 3 * WIDTH), F32),
                   jax.ShapeDtypeStruct((n, 3 * WIDTH), F32),
                   jax.ShapeDtypeStruct((n, LORA_COLS), F32)],
        compiler_params=_cparams(("parallel",)),
        name="inproj",
    )(x2d, w_pad)


def _moba_kernel(slopes_ref, q_ref, k_ref, v_ref, gain_ref, o_ref, kaug_ref, vb_ref, kmp_ref, *, seq):
    pair = pl.program_id(1)
    i = pl.program_id(2)
    nb = seq // MOBA_BLOCK
    blk = MOBA_BLOCK

    @pl.when(i == 0)
    def _():
        k = k_ref[...]
        rowi = lax.broadcasted_iota(I32, (seq, LANES), 0)
        rowblk = lax.shift_right_logical(rowi, 8)
        rowin = rowi & (blk - 1)
        lane_s = lax.broadcasted_iota(I32, (seq, LANES), 1)
        kmean = jnp.sum(k.reshape(nb, blk, LANES), axis=1) * (1.0 / blk)
        for hh in range(2):
            slope = slopes_ref[2 * pair + hh]
            ob = HEAD_DIM * (1 - hh)
            inhead = (lane_s >= HEAD_DIM * hh) & (lane_s < HEAD_DIM * (hh + 1))
            c = lane_s - ob
            aug = jnp.where(c == rowblk, 1.0, 0.0)
            aug = jnp.where(c == POS_BLK, (slope * blk) * rowblk.astype(F32), aug)
            aug = jnp.where(c == POS_IN, slope * rowin.astype(F32), aug)
            kaug_ref[hh] = jnp.where(inhead, k, aug).astype(BF16)
            kmp_ref[hh] = jnp.zeros((LANES, LANES), F32)
            kmp_ref[hh, ob:ob + nb, :] = kmean
        vb_ref[...] = v_ref[...].astype(BF16)

    def query_block(ii):
        q = q_ref[...] * (HEAD_DIM ** -0.5)
        causal = (lax.broadcasted_iota(I32, (blk, blk), 0) >= lax.broadcasted_iota(I32, (blk, blk), 1))
        lane = lax.broadcasted_iota(I32, (blk, LANES), 1)
        gain = gain_ref[...]
        nk = (ii + 1) * blk
        inheads = [(lane >= HEAD_DIM * hh) & (lane < HEAD_DIM * (hh + 1)) for hh in range(2)]
        qas = []
        for hh in range(2):
            ob = HEAD_DIM * (1 - hh)
            inhead = inheads[hh]
            c = lane - ob
            qaug = jnp.where((c == POS_BLK) | (c == POS_IN), 1.0, 0.0)
            if ii > MOBA_TOP:
                qm = jnp.where(inhead, q, 0.0)
                gate = _dot1(kmp_ref[hh], qm, NT_DIMS)[ob:ob + GATE_ROWS, :]
                jrow = lax.broadcasted_iota(I32, (GATE_ROWS, blk), 0)
                cnt = jnp.zeros((GATE_ROWS, blk), I32)
                for jp in range(ii):
                    rowv = gate[jp:jp + 1, :]
                    beats = (rowv > gate) | ((rowv == gate) & (jp < jrow))
                    cnt = cnt + jnp.where(beats, 1, 0)
                drop = jnp.where((jrow < ii) & (cnt >= MOBA_TOP), NEG, 0.0).astype(BF16)
                place = (lax.broadcasted_iota(I32, (GATE_ROWS, LANES), 1) - ob
                         == lax.broadcasted_iota(I32, (GATE_ROWS, LANES), 0))
                qaug = qaug + lax.dot_general(drop, jnp.where(place, 1.0, 0.0).astype(BF16), TN_DIMS,
                                              preferred_element_type=F32)
            qas.append(jnp.where(inhead, q, qaug).astype(BF16))

        scores = [lax.dot_general(qas[hh], kaug_ref[hh, 0:nk, :], NT_DIMS, preferred_element_type=F32)
                  for hh in range(2)]
        probs, sums = [], []
        for s in scores:
            parts = [s[:, j * blk:(j + 1) * blk] for j in range(ii)]
            parts.append(jnp.where(causal, s[:, ii * blk:nk], NEG))
            m = jnp.max(functools.reduce(jnp.maximum, parts), -1, keepdims=True)
            ps = [jnp.exp(x - m) for x in parts]
            sums.append(jnp.sum(functools.reduce(lambda a, b: a + b, ps), -1, keepdims=True))
            probs.append(jnp.concatenate([x.astype(BF16) for x in ps], axis=1))
        outs = []
        for hh in range(2):
            out = jnp.dot(probs[hh], vb_ref[0:nk, :], preferred_element_type=F32) / sums[hh]
            ms = jnp.sum(jnp.where(inheads[hh], out * out, 0.0), -1, keepdims=True) * (1.0 / HEAD_DIM)
            outs.append(out * lax.rsqrt(ms + RMS_EPS) * gain)
        o_ref[...] = jnp.where(lane < HEAD_DIM, outs[0], outs[1])

    for ii in range(nb):
        pl.when(i == ii)(functools.partial(query_block, ii))


def _moba(z_att, slopes, gain, batch, seq):
    nb = seq // MOBA_BLOCK
    grid_spec = pltpu.PrefetchScalarGridSpec(
        num_scalar_prefetch=1,
        grid=(batch, N_PAIRS, nb),
        in_specs=[pl.BlockSpec((None, MOBA_BLOCK, LANES), lambda b, p, i, s: (b, i, p)),
                  pl.BlockSpec((None, seq, LANES), lambda b, p, i, s: (b, 0, N_PAIRS + p)),
                  pl.BlockSpec((None, seq, LANES), lambda b, p, i, s: (b, 0, 2 * N_PAIRS + p)),
                  pl.BlockSpec((1, LANES), lambda b, p, i, s: (0, p))],
        out_specs=pl.BlockSpec((None, MOBA_BLOCK, LANES), lambda b, p, i, s: (b, i, p)),
        scratch_shapes=[pltpu.VMEM((2, seq, LANES), BF16),
                        pltpu.VMEM((seq, LANES), BF16),
                        pltpu.VMEM((2, LANES, LANES), F32)],
    )
    return pl.pallas_call(
        functools.partial(_moba_kernel, seq=seq),
        grid_spec=grid_spec,
        out_shape=jax.ShapeDtypeStruct((batch, seq, WIDTH), F32),
        compiler_params=_cparams(("parallel", "parallel", "arbitrary")),
        name="moba",
    )(slopes, z_att, z_att, z_att, gain)


def _dot2x(a, b_exact):
    ah, al = _split(a)
    return (jnp.dot(ah, b_exact, preferred_element_type=F32)
            + jnp.dot(al, b_exact, preferred_element_type=F32))


def _head_block_diag(scale):
    r = lax.shift_right_logical(lax.broadcasted_iota(I32, (LANES, LANES), 0), 6)
    c = lax.shift_right_logical(lax.broadcasted_iota(I32, (LANES, LANES), 1), 6)
    return jnp.where(r == c, scale, 0.0).astype(BF16)


def _softplus(x):
    return jnp.maximum(x, 0.0) + jnp.log(1.0 + jnp.exp(-jnp.abs(x)))


def _rwkv_chunks(items):
    c_ = CHUNK
    row = lax.broadcasted_iota(I32, (c_, c_), 0)
    col = lax.broadcasted_iota(I32, (c_, c_), 1)
    incl = row >= col
    strict = row > col
    lane = lax.broadcasted_iota(I32, (c_, LANES), 1)
    m0 = lane < HEAD_DIM
    tri = jnp.where(incl, 1.0, 0.0).astype(BF16)
    eye = jnp.where(row == col, 1.0, 0.0)
    splits = [_split(it[0]) for it in items]
    cs = [jnp.dot(tri, hi, preferred_element_type=F32) + jnp.dot(tri, lo, preferred_element_type=F32)
          for hi, lo in splits]
    base = []
    for (lw, r, k2, v, a, b), c in zip(items, cs):
        e_pos = jnp.exp(c)
        e_neg = jnp.exp(-c)
        at = a * jnp.exp(c - lw)
        bt = b * e_neg
        kt = k2 * e_neg
        rt = r * e_pos
        gc = e_pos[c_ - 1:c_, :]
        vs = jnp.concatenate([jnp.where(m0, v, 0.0), jnp.where(m0, 0.0, v)], axis=0)
        base.append(dict(at=at, bt=bt, kt=kt, rt=rt, gc=gc, bp=bt * gc, kp=kt * gc, vs=vs))
    heads = []
    for d in base:
        for hh in range(2):
            mh = m0 if hh == 0 else jnp.logical_not(m0)
            am = jnp.where(mh, d['at'], 0.0)
            rm = jnp.where(mh, d['rt'], 0.0)
            heads.append(dict(
                lab=jnp.where(strict, _dot1(am, d['bt'], NT_DIMS), 0.0),
                lak=jnp.where(strict, _dot1(am, d['kt'], NT_DIMS), 0.0),
                qb=jnp.where(incl, _dot1(rm, d['bt'], NT_DIMS), 0.0),
                qk=jnp.where(incl, _dot1(rm, d['kt'], NT_DIMS), 0.0)))
    ts = [eye + h['lab'] for h in heads]
    lps = [h['lab'] for h in heads]
    for _ in range(5):
        lps = [_dot1(lp, lp) for lp in lps]
        ts = [t + _dot1(t, lp) for t, lp in zip(ts, lps)]
    outs = []
    for i, d in enumerate(base):
        h0, h1 = heads[2 * i], heads[2 * i + 1]
        tcat = jnp.concatenate([ts[2 * i], ts[2 * i + 1]], axis=1)
        qcat = jnp.concatenate([h0['qb'], h1['qb']], axis=1)
        p = _dot1(jnp.concatenate([h0['lak'], h1['lak']], axis=1), d['vs'])
        yk = _dot1(jnp.concatenate([h0['qk'], h1['qk']], axis=1), d['vs'])
        grow = jnp.broadcast_to(d['gc'], (c_, LANES))
        outs.append((d['at'].astype(BF16), p, tcat.astype(BF16), d['bp'].astype(BF16), d['kp'].astype(BF16),
                     grow, d['rt'].astype(BF16), qcat.astype(BF16), yk))
    return outs


def _rwkv_pre_kernel(*refs, has_vres, tb):
    if has_vres:
        (rkv_ref, lora_ref, vf_ref, mu_rkv_ref, mu_lora_ref, vec_ref, wd_ref, wa_ref, wg_ref, wv_ref,
         at_ref, p_ref, tc_ref, bp_ref, kp_ref, v_ref, gr_ref, rt_ref, qc_ref, yk_ref, bonus_ref, g_ref,
         carry_ref, carryl_ref) = refs
    else:
        (rkv_ref, lora_ref, mu_rkv_ref, mu_lora_ref, vec_ref, wd_ref, wa_ref, wg_ref,
         at_ref, p_ref, tc_ref, bp_ref, kp_ref, v_ref, gr_ref, rt_ref, qc_ref, yk_ref, bonus_ref, g_ref,
         carry_ref, carryl_ref) = refs
    t_idx = pl.program_id(1)

    @pl.when(t_idx == 0)
    def _():
        carry_ref[...] = jnp.zeros_like(carry_ref)
        carryl_ref[...] = jnp.zeros_like(carryl_ref)

    def token_shift(z, cref, mu):
        first = lax.broadcasted_iota(I32, z.shape, 0) == 0
        prev = jnp.where(first, cref[0:1, :], pltpu.roll(z, 1, 0))
        cref[0:1, :] = z[tb - 1:tb, :]
        return z + (prev - z) * mu

    zs = token_shift(rkv_ref[...], carry_ref, mu_rkv_ref[...])
    zl = token_shift(lora_ref[...], carryl_ref, mu_lora_ref[...])
    l0 = zl[:, 0:LANES]
    w0, a0, v0 = vec_ref[0:1, :], vec_ref[1:2, :], vec_ref[2:3, :]
    k_k, k_a, r_k = vec_ref[3:4, :], vec_ref[4:5, :], vec_ref[5:6, :]
    dw = _dot1(jnp.tanh(l0), wd_ref[...])
    lw_all = -jnp.exp(-_softplus(-(w0 + dw)) - 0.5)
    a_lr = _sigmoid(a0 + _dot1(l0, wa_ref[...]))
    g_ref[...] = _dot1(_sigmoid(zl[:, LANES:2 * LANES]), wg_ref[...])
    r_all = zs[:, 0:WIDTH]
    k_all = zs[:, WIDTH:2 * WIDTH]
    v_all = zs[:, 2 * WIDTH:3 * WIDTH]
    if has_vres:
        mix = _sigmoid(v0 + _dot1(zl[:, 2 * LANES:3 * LANES], wv_ref[...]))
        v_all = v_all + (vf_ref[...] - v_all) * mix
    v_ref[...] = v_all
    bd = _head_block_diag(1.0)
    items, where = [], []
    for pp in range(N_PAIRS):
        sl = slice(pp * LANES, (pp + 1) * LANES)
        r, k, v = r_all[:, sl], k_all[:, sl], v_all[:, sl]
        alr = a_lr[:, sl]
        kk = k * k_k[:, sl]
        ss = _dot2x(kk * kk, bd)
        kk = kk * lax.rsqrt(jnp.maximum(ss, 1e-24))
        k2 = k * (1.0 + (alr - 1.0) * k_a[:, sl])
        bonus_ref[:, sl] = _dot2x(r * k2 * r_k[:, sl], bd) * v
        a = -kk
        b = kk * alr
        lw = lw_all[:, sl]
        for ci in range(tb // CHUNK):
            rs = slice(ci * CHUNK, (ci + 1) * CHUNK)
            items.append((lw[rs], r[rs], k2[rs], v[rs], a[rs], b[rs]))
            where.append((rs, sl))
    for (rs, sl), outs in zip(where, _rwkv_chunks(items)):
        for ref, val in zip((at_ref, p_ref, tc_ref, bp_ref, kp_ref, gr_ref, rt_ref, qc_ref, yk_ref), outs):
            ref[rs, sl] = val


def _rwkv_pre(z_rkv, z_lora, v_first, mu_rkv, mu_lora, vecs, wd, wa, wg, wv, batch, seq, tb=128):
    has_vres = v_first is not None
    big = lambda w: pl.BlockSpec((None, tb, w), lambda b, t: (b, t, 0))
    full = lambda shape: pl.BlockSpec(shape, lambda b, t: (0,) * len(shape))
    in_specs = [big(3 * WIDTH), big(LORA_COLS)] + ([big(WIDTH)] if has_vres else [])
    in_specs += [full((1, 3 * WIDTH)), full((1, LORA_COLS)), full((8, WIDTH)),
                 full((LANES, WIDTH)), full((LANES, WIDTH)), full((LANES, WIDTH))]
    args = [z_rkv, z_lora] + ([v_first] if has_vres else []) + [mu_rkv, mu_lora, vecs, wd, wa, wg]
    if has_vres:
        in_specs.append(full((LANES, WIDTH)))
        args.append(wv)
    out_dtypes = [BF16, F32, BF16, BF16, BF16, F32, F32, BF16, BF16, F32, F32, F32]
    return pl.pallas_call(
        functools.partial(_rwkv_pre_kernel, has_vres=has_vres, tb=tb),
        grid=(batch, seq // tb),
        in_specs=in_specs,
        out_specs=[big(WIDTH)] * len(out_dtypes),
        out_shape=[jax.ShapeDtypeStruct((batch, seq, WIDTH), dt) for dt in out_dtypes],
        scratch_shapes=[pltpu.VMEM((8, 3 * WIDTH), F32), pltpu.VMEM((8, LORA_COLS), F32)],
        compiler_params=_cparams(("parallel", "arbitrary")),
        name="rwkv_pre",
    )(*args)


def _rwkv_scan_kernel(at_ref, p_ref, tc_ref, bp_ref, kp_ref, v_ref, gr_ref, rt_ref, qc_ref, yk_ref,
                      bonus_ref, g_ref, lnx_ref, o_ref, st_ref, *, ts):
    t_idx = pl.program_id(1)

    @pl.when(t_idx == 0)
    def _():
        st_ref[...] = jnp.zeros_like(st_ref)

    lane = lax.broadcasted_iota(I32, (CHUNK, LANES), 1)
    m0 = lane < HEAD_DIM
    bdmask = (lax.shift_right_logical(lax.broadcasted_iota(I32, (LANES, LANES), 0), 6)
              == lax.shift_right_logical(lax.broadcasted_iota(I32, (LANES, LANES), 1), 6))
    ones_t = jnp.full((CHUNK, LANES), 1.0 / CHUNK, BF16)

    def stack(x):
        return jnp.concatenate([jnp.where(m0, x, 0.0), jnp.where(m0, 0.0, x)], axis=0)

    tn = functools.partial(lax.dot_general, dimension_numbers=TN_DIMS, preferred_element_type=F32)
    nseq = at_ref.shape[0]
    chains = [(bb, pp, slice(pp * LANES, (pp + 1) * LANES)) for bb in range(nseq) for pp in range(N_PAIRS)]
    chunks = [slice(ci * CHUNK, (ci + 1) * CHUNK) for ci in range(ts // CHUNK)]

    gcols, kvs = {}, {}
    for ci, rs in enumerate(chunks):
        for bb, pp, sl in chains:
            gh, gl = _split(gr_ref[bb, rs, sl])
            gcols[ci, bb, pp] = tn(gh, ones_t) + tn(gl, ones_t)
            kvs[ci, bb, pp] = jnp.where(bdmask, _dot1(kp_ref[bb, rs, sl], v_ref[bb, rs, sl], TN_DIMS), 0.0)

    states = [st_ref[bb, pp] for bb, pp, _ in chains]
    for ci, rs in enumerate(chunks):
        sts = [st.astype(BF16) for st in states]
        xs = [_dot1(at_ref[bb, rs, sl], sts[n]) + p_ref[bb, rs, sl] for n, (bb, pp, sl) in enumerate(chains)]
        us = [_dot1(tc_ref[bb, rs, sl], stack(xs[n])).astype(BF16) for n, (bb, pp, sl) in enumerate(chains)]
        new = [gcols[ci, bb, pp] * states[n]
               + (jnp.where(bdmask, _dot1(bp_ref[bb, rs, sl], us[n], TN_DIMS), 0.0) + kvs[ci, bb, pp])
               for n, (bb, pp, sl) in enumerate(chains)]
        for n, (bb, pp, sl) in enumerate(chains):
            lhs = jnp.concatenate([rt_ref[bb, rs, sl], qc_ref[bb, rs, sl]], axis=1)
            rhs = jnp.concatenate([sts[n], stack(us[n])], axis=0)
            o_ref[bb, rs, sl] = jnp.dot(lhs, rhs, preferred_element_type=F32) + yk_ref[bb, rs, sl]
        states = new
    for n, (bb, pp, _) in enumerate(chains):
        st_ref[bb, pp] = states[n]

    bd = _head_block_diag(1.0 / HEAD_DIM)
    for bb, pp, sl in chains:
        y = o_ref[bb, :, sl]
        mu = _dot2x(y, bd)
        d = y - mu
        var = _dot2x(d * d, bd)
        yn = d * lax.rsqrt(var + RWKV_GN_EPS) * lnx_ref[0:1, sl] + lnx_ref[1:2, sl]
        o_ref[bb, :, sl] = (yn + bonus_ref[bb, :, sl]) * g_ref[bb, :, sl]


def _rwkv_scan(pre, lnx, batch, seq, ts=256, nseq=2):
    big = pl.BlockSpec((nseq, ts, WIDTH), lambda b, t: (b, t, 0))
    return pl.pallas_call(
        functools.partial(_rwkv_scan_kernel, ts=ts),
        grid=(batch // nseq, seq // ts),
        in_specs=[big] * 12 + [pl.BlockSpec((8, WIDTH), lambda b, t: (0, 0))],
        out_specs=big,
        out_shape=jax.ShapeDtypeStruct((batch, seq, WIDTH), F32),
        scratch_shapes=[pltpu.VMEM((nseq, N_PAIRS, LANES, LANES), F32)],
        compiler_params=_cparams(("parallel", "arbitrary")),
        name="rwkv_scan",
    )(*pre, lnx)


def _outproj_kernel(x_ref, oa_ref, orw_ref, w_ref, ln_ref, o_ref):
    y = (jnp.dot(oa_ref[...].astype(BF16), w_ref[0:WIDTH, :], preferred_element_type=F32)
         + jnp.dot(orw_ref[...].astype(BF16), w_ref[WIDTH:2 * WIDTH, :], preferred_element_type=F32))
    o_ref[...] = _layer_norm(ALPHA * x_ref[...] + y, ln_ref[0:1, :], ln_ref[1:2, :])


def _outproj(x2d, o_att, o_rwkv, w_out, ln, tm=512):
    n = x2d.shape[0]
    return pl.pallas_call(
        _outproj_kernel,
        grid=(n // tm,),
        in_specs=[pl.BlockSpec((tm, D_MODEL), lambda i: (i, 0)),
                  pl.BlockSpec((tm, WIDTH), lambda i: (i, 0)),
                  pl.BlockSpec((tm, WIDTH), lambda i: (i, 0)),
                  pl.BlockSpec((2 * WIDTH, D_MODEL), lambda i: (0, 0)),
                  pl.BlockSpec((8, D_MODEL), lambda i: (0, 0))],
        out_specs=pl.BlockSpec((tm, D_MODEL), lambda i: (i, 0)),
        out_shape=jax.ShapeDtypeStruct((n, D_MODEL), F32),
        compiler_params=_cparams(("parallel",)),
        name="outproj_ln",
    )(x2d, o_att, o_rwkv, w_out, ln)


def _kvproj_kernel(m_ref, w_ref, o_ref):
    o_ref[...] = jnp.dot(m_ref[...].astype(BF16), w_ref[...], preferred_element_type=F32).astype(BF16)


def _kvproj(mem2d, w_kv, tm=256):
    n = mem2d.shape[0]
    return pl.pallas_call(
        _kvproj_kernel,
        grid=(n // tm,),
        in_specs=[pl.BlockSpec((tm, D_MODEL), lambda i: (i, 0)),
                  pl.BlockSpec((D_MODEL, 2 * D_MODEL), lambda i: (0, 0))],
        out_specs=pl.BlockSpec((tm, 2 * D_MODEL), lambda i: (i, 0)),
        out_shape=jax.ShapeDtypeStruct((n, 2 * D_MODEL), BF16),
        compiler_params=_cparams(("parallel",)),
        name="kvproj",
    )(mem2d, w_kv)


def _route_lanes(logits, b_router):
    lane = lax.broadcasted_iota(I32, logits.shape, 1)
    lane_f = lane.astype(F32)
    valid = lane < N_EXPERTS
    s = _sigmoid(logits)
    sel = jnp.where(valid, s + b_router, NEG)

    def partner(x, bit):
        return jnp.where((lane & bit) == 0, pltpu.roll(x, LANES - bit, 1), pltpu.roll(x, bit, 1))

    p1 = partner(sel, 1)
    hi1, lo1 = jnp.maximum(sel, p1), jnp.minimum(sel, p1)
    hi2, lo2 = partner(hi1, 2), partner(lo1, 2)
    gsum = jnp.maximum(hi1, hi2) + jnp.maximum(jnp.minimum(hi1, hi2), jnp.maximum(lo1, lo2))
    gmax = jnp.max(gsum, -1, keepdims=True)
    grp_f = lax.shift_right_logical(lane, 2).astype(F32)
    g_best = jnp.min(jnp.where((gsum == gmax) & valid, grp_f, 1e9), -1, keepdims=True)
    in_grp = (grp_f == g_best) & valid
    masked = jnp.where(in_grp, sel, NEG)
    v1 = jnp.max(masked, -1, keepdims=True)
    e1 = jnp.min(jnp.where(in_grp & (masked == v1), lane_f, 1e9), -1, keepdims=True)
    rest = in_grp & (lane_f != e1)
    masked2 = jnp.where(rest, sel, NEG)
    v2 = jnp.max(masked2, -1, keepdims=True)
    e2 = jnp.min(jnp.where(rest & (masked2 == v2), lane_f, 1e9), -1, keepdims=True)
    w1 = jnp.sum(jnp.where(lane_f == e1, s, 0.0), -1, keepdims=True)
    w2 = jnp.sum(jnp.where(lane_f == e2, s, 0.0), -1, keepdims=True)
    tot = w1 + w2
    out = jnp.where(lane == 0, e1, jnp.where(lane == 1, e2, jnp.where(lane == 2, w1 / tot, w2 / tot)))
    return jnp.where(lane < 4, out, 0.0)


def _memattn_kernel(x_ref, kv_ref, wq_ref, wo_ref, ln_ref, wr_ref, br_ref, o_ref, route_ref):
    x = x_ref[...]
    q = jnp.dot(x.astype(BF16), wq_ref[...], preferred_element_type=F32) * (MEM_HEAD_DIM ** -0.5)
    heads = []
    for h in range(N_MEM_HEADS):
        sl = slice(h * MEM_HEAD_DIM, (h + 1) * MEM_HEAD_DIM)
        kh = kv_ref[:, sl]
        vh = kv_ref[:, D_MODEL + h * MEM_HEAD_DIM:D_MODEL + (h + 1) * MEM_HEAD_DIM]
        s = lax.dot_general(q[:, sl].astype(BF16), kh, NT_DIMS, preferred_element_type=F32)
        p = jnp.exp(s - jnp.max(s, -1, keepdims=True))
        o = jnp.dot(p.astype(BF16), vh, preferred_element_type=F32)
        heads.append(o / jnp.sum(p, -1, keepdims=True))
    o = jnp.concatenate(heads, axis=-1).astype(BF16)
    y = jnp.dot(o, wo_ref[...], preferred_element_type=F32)
    x2 = _layer_norm(ALPHA * x + y, ln_ref[0:1, :], ln_ref[1:2, :])
    o_ref[...] = x2
    logits = _dot1(x2, wr_ref[...])
    route_ref[...] = _route_lanes(logits, br_ref[0:1, :])


def _memattn(x3d, kv3d, wq, wo, ln, w_router_pad, b_router_pad, tm=512):
    batch, seq, _ = x3d.shape
    mlen = kv3d.shape[1]
    full = lambda shape: pl.BlockSpec(shape, lambda b, t: (0,) * len(shape))
    return pl.pallas_call(
        _memattn_kernel,
        grid=(batch, seq // tm),
        in_specs=[pl.BlockSpec((None, tm, D_MODEL), lambda b, t: (b, t, 0)),
                  pl.BlockSpec((None, mlen, 2 * D_MODEL), lambda b, t: (b, 0, 0)),
                  full((D_MODEL, D_MODEL)), full((D_MODEL, D_MODEL)), full((8, D_MODEL)),
                  full((D_MODEL, LANES)), full((8, LANES))],
        out_specs=[pl.BlockSpec((None, tm, D_MODEL), lambda b, t: (b, t, 0)),
                   pl.BlockSpec((None, tm, LANES), lambda b, t: (b, t, 0))],
        out_shape=[jax.ShapeDtypeStruct((batch, seq, D_MODEL), F32),
                   jax.ShapeDtypeStruct((batch, seq, LANES), F32)],
        compiler_params=_cparams(("parallel", "parallel")),
        name="memattn_ln",
    )(x3d, kv3d, wq, wo, ln, w_router_pad, b_router_pad)


def _plan_kernel(route_ref, dest_ref, stat_ref, acc_ref):
    phase = pl.program_id(0)
    i = pl.program_id(1)
    tm = route_ref.shape[0]
    lane = lax.broadcasted_iota(I32, (tm, LANES), 1).astype(F32)
    e0 = route_ref[:, 0:1]
    e1 = route_ref[:, 1:2]
    hot = jnp.where((lane == e0) | (lane == e1), 1.0, 0.0)

    @pl.when((phase == 0) & (i == 0))
    def _():
        acc_ref[...] = jnp.zeros_like(acc_ref)

    @pl.when(phase == 0)
    def _():
        acc_ref[0:1, :] = acc_ref[0:1, :] + jnp.sum(hot, axis=0, keepdims=True)

    @pl.when((phase == 1) & (i == 0))
    def _():
        r = lax.broadcasted_iota(I32, (LANES, LANES), 0)
        c = lax.broadcasted_iota(I32, (LANES, LANES), 1)
        before = jnp.where(r < c, 1.0, 0.0)
        cnt8 = jnp.broadcast_to(acc_ref[0:1, :], (8, LANES))
        acc_ref[1:2, :] = _dot2x(cnt8, before.astype(BF16))[0:1, :]

    @pl.when(phase == 1)
    def _():
        r = lax.broadcasted_iota(I32, (tm, tm), 0)
        c = lax.broadcasted_iota(I32, (tm, tm), 1)
        earlier = jnp.where(c < r, 1.0, 0.0).astype(BF16)
        pos = (jnp.dot(earlier, hot.astype(BF16), preferred_element_type=F32)
               + acc_ref[1:2, :] + acc_ref[2:3, :])
        d0 = jnp.sum(jnp.where(lane == e0, pos, 0.0), -1, keepdims=True)
        d1 = jnp.sum(jnp.where(lane == e1, pos, 0.0), -1, keepdims=True)
        dest_ref[...] = jnp.where(lane == 0.0, d0, jnp.where(lane == 1.0, d1, 0.0)).astype(I32)
        acc_ref[2:3, :] = acc_ref[2:3, :] + jnp.sum(hot, axis=0, keepdims=True)
        stat_ref[...] = acc_ref[...]


def _plan(route, tm=256):
    n = route.shape[0]
    return pl.pallas_call(
        _plan_kernel,
        grid=(2, n // tm),
        in_specs=[pl.BlockSpec((tm, LANES), lambda p, i: (i, 0))],
        out_specs=[pl.BlockSpec((tm, LANES), lambda p, i: (i * p, 0)),
                   pl.BlockSpec((8, LANES), lambda p, i: (0, 0))],
        out_shape=[jax.ShapeDtypeStruct((n, LANES), I32), jax.ShapeDtypeStruct((8, LANES), F32)],
        scratch_shapes=[pltpu.VMEM((8, LANES), F32)],
        compiler_params=_cparams(("arbitrary", "arbitrary")),
        name="moe_plan",
    )(route)


def _dispatch_kernel(dest_ref, prev_ref, x_hbm, xs_hbm, sems):
    i = pl.program_id(0)
    tm = dest_ref.shape[2] // 2

    def copy(step, dref, t, slot):
        src = x_hbm.at[pl.ds(step * tm + t, 1)]
        return pltpu.make_async_copy(src, xs_hbm.at[pl.ds(dref[0, 0, slot * tm + t], 1)], sems.at[step % 2])

    def start(t, c):
        copy(i, dest_ref, t, 0).start()
        copy(i, dest_ref, t, 1).start()
        return c

    def wait_step(step, dref):
        def wait(t, c):
            copy(step, dref, t, 0).wait()
            copy(step, dref, t, 1).wait()
            return c
        lax.fori_loop(0, tm, wait, 0, unroll=8)

    lax.fori_loop(0, tm, start, 0, unroll=8)

    @pl.when(i > 0)
    def _():
        wait_step(i - 1, prev_ref)

    @pl.when(i == pl.num_programs(0) - 1)
    def _():
        wait_step(i, dest_ref)


def _dispatch(x2d, dest3, n_rows):
    n_steps = dest3.shape[0]
    blk = (1, 1, dest3.shape[2])
    return pl.pallas_call(
        _dispatch_kernel,
        grid=(n_steps,),
        in_specs=[pl.BlockSpec(blk, lambda i: (i, 0, 0), memory_space=pltpu.SMEM),
                  pl.BlockSpec(blk, lambda i: (jnp.maximum(i - 1, 0), 0, 0), memory_space=pltpu.SMEM),
                  pl.BlockSpec(memory_space=pl.ANY)],
        out_specs=pl.BlockSpec(memory_space=pl.ANY),
        out_shape=jax.ShapeDtypeStruct((n_rows, D_MODEL), F32),
        scratch_shapes=[pltpu.SemaphoreType.DMA((2,))],
        compiler_params=_cparams(("arbitrary",)),
        name="moe_dispatch",
    )(dest3, dest3, x2d)


def _experts_kernel(rb_ref, ex_ref, nit_ref, off_ref, cnt_ref, x_ref, wg_ref, wu_ref, wd_ref, o_ref,
                    wgb_ref, wub_ref, wdb_ref):
    w = pl.program_id(0)
    rows = x_ref.shape[0]

    @pl.when(w < nit_ref[0])
    def _():
        e = ex_ref[w]
        rb = rb_ref[w]
        prev = jnp.maximum(w - 1, 0)

        @pl.when((w == 0) | (ex_ref[prev] != e))
        def _():
            wgb_ref[...] = wg_ref[...].astype(BF16)
            wub_ref[...] = wu_ref[...].astype(BF16)
            wdb_ref[...] = wd_ref[...].astype(BF16)

        xb = x_ref[...].astype(BF16)
        hg = jnp.dot(xb, wgb_ref[...], preferred_element_type=F32)
        hu = jnp.dot(xb, wub_ref[...], preferred_element_type=F32)
        h = hg * _sigmoid(hg) * hu
        y = jnp.dot(h.astype(BF16), wdb_ref[...], preferred_element_type=F32)
        row = rb * rows + lax.broadcasted_iota(I32, (rows, 1), 0)
        mine = (row >= off_ref[e]) & (row < off_ref[e] + cnt_ref[e])
        first = (w == 0) | (rb_ref[prev] != rb)

        @pl.when(first)
        def _():
            o_ref[...] = jnp.where(mine, y, 0.0)

        @pl.when(jnp.logical_not(first))
        def _():
            o_ref[...] = jnp.where(mine, y, o_ref[...])


def _experts(xs, rb, ex, n_items, off, cnt, w_gate, w_up, w_down, rows):
    n_work = rb.shape[0]
    item = lambda w, nit: jnp.minimum(w, nit[0] - 1)
    row_map = lambda w, rb, ex, nit, off, cnt: (rb[item(w, nit)], 0)
    w_map = lambda w, rb, ex, nit, off, cnt: (ex[item(w, nit)], 0, 0)
    grid_spec = pltpu.PrefetchScalarGridSpec(
        num_scalar_prefetch=5,
        grid=(n_work,),
        in_specs=[pl.BlockSpec((rows, D_MODEL), row_map),
                  pl.BlockSpec((None, D_MODEL, D_EXPERT), w_map),
                  pl.BlockSpec((None, D_MODEL, D_EXPERT), w_map),
                  pl.BlockSpec((None, D_EXPERT, D_MODEL), w_map)],
        out_specs=pl.BlockSpec((rows, D_MODEL), row_map),
        scratch_shapes=[pltpu.VMEM((D_MODEL, D_EXPERT), BF16), pltpu.VMEM((D_MODEL, D_EXPERT), BF16),
                        pltpu.VMEM((D_EXPERT, D_MODEL), BF16)],
    )
    return pl.pallas_call(
        _experts_kernel,
        grid_spec=grid_spec,
        out_shape=jax.ShapeDtypeStruct(xs.shape, F32),
        compiler_params=_cparams(("arbitrary",)),
        name="moe_experts",
    )(rb, ex, n_items, off, cnt, xs, w_gate, w_up, w_down)


def _combine_kernel(dest_ref, next_ref, x_ref, gate_ref, ln_ref, y_hbm, o_ref, buf_ref, sems):
    i = pl.program_id(0)
    tm = x_ref.shape[0]

    def copy(step, dref, slot, t):
        src = y_hbm.at[pl.ds(dref[0, 0, slot * tm + t], 1)]
        return pltpu.make_async_copy(src, buf_ref.at[step % 2, slot, pl.ds(t, 1)], sems.at[step % 2])

    def start_step(step, dref):
        def start(t, c):
            copy(step, dref, 0, t).start()
            copy(step, dref, 1, t).start()
            return c
        lax.fori_loop(0, tm, start, 0, unroll=8)

    @pl.when(i == 0)
    def _():
        start_step(0, dest_ref)

    @pl.when(i + 1 < pl.num_programs(0))
    def _():
        start_step(i + 1, next_ref)

    def wait(t, c):
        copy(i, dest_ref, 0, t).wait()
        copy(i, dest_ref, 1, t).wait()
        return c

    lax.fori_loop(0, tm, wait, 0, unroll=8)
    cur = i % 2
    y = gate_ref[:, 0:1] * buf_ref[cur, 0] + gate_ref[:, 1:2] * buf_ref[cur, 1]
    o_ref[...] = _layer_norm(ALPHA * x_ref[...] + y, ln_ref[0:1, :], ln_ref[1:2, :])


def _combine(x2d, gates, dest3, ys, ln, tm=256):
    n = x2d.shape[0]
    n_steps = n // tm
    blk = (1, 1, 2 * tm)
    return pl.pallas_call(
        _combine_kernel,
        grid=(n_steps,),
        in_specs=[pl.BlockSpec(blk, lambda i: (i, 0, 0), memory_space=pltpu.SMEM),
                  pl.BlockSpec(blk, lambda i: (jnp.minimum(i + 1, n_steps - 1), 0, 0), memory_space=pltpu.SMEM),
                  pl.BlockSpec((tm, D_MODEL), lambda i: (i, 0)),
                  pl.BlockSpec((tm, 2), lambda i: (i, 0)),
                  pl.BlockSpec((8, D_MODEL), lambda i: (0, 0)),
                  pl.BlockSpec(memory_space=pl.ANY)],
        out_specs=pl.BlockSpec((tm, D_MODEL), lambda i: (i, 0)),
        out_shape=jax.ShapeDtypeStruct((n, D_MODEL), F32),
        scratch_shapes=[pltpu.VMEM((2, 2, tm, D_MODEL), F32), pltpu.SemaphoreType.DMA((2,))],
        compiler_params=_cparams(("arbitrary",)),
        name="moe_combine_ln",
    )(dest3, dest3, x2d, gates, ln, ys)


def _work_items(counts, off, n_rows, rows):
    n_work = n_rows // rows + N_EXPERTS
    first = off // rows
    last = (off + counts - 1) // rows
    per_expert = jnp.where(counts > 0, last - first + 1, 0)
    item_end = jnp.cumsum(per_expert)
    item_start = item_end - per_expert
    w = jnp.arange(n_work, dtype=I32)
    ex = jnp.minimum(jnp.searchsorted(item_end, w, side='right'), N_EXPERTS - 1).astype(I32)
    rb = jnp.clip(first[ex] + (w - item_start[ex]), 0, n_rows // rows - 1).astype(I32)
    return rb, ex, item_end[-1:].astype(I32)


def _moe(x2, route, w_gate, w_up, w_down, ln, rows=MOE_ROWS, tm=256):
    n = x2.shape[0]
    n_rows = 2 * n
    gates = route[:, 2:4]
    dest_tile, stat = _plan(route)
    counts = stat[0, :N_EXPERTS].astype(I32)
    off = stat[1, :N_EXPERTS].astype(I32)
    rb, ex, n_items = _work_items(counts, off, n_rows, rows)
    dest3 = dest_tile[:, 0:2].reshape(n // tm, tm, 2).transpose(0, 2, 1).reshape(n // tm, 1, 2 * tm)
    xs = _dispatch(x2, dest3, n_rows)
    ys = _experts(xs, rb, ex, n_items, off, counts, w_gate, w_up, w_down, rows)
    return _combine(x2, gates, dest3, ys, ln, tm)


def _pad_rows(w, rows_before, total):
    return jnp.pad(w, ((rows_before, total - rows_before - w.shape[0]), (0, 0)))


def kernel(x, mem, w_in, w_in_vres, mu_shift, mu_vres, w0, w_decay_up, a0, w_iclr_up, v0, w_vres_up,
           w_gate_up, k_k, k_a, r_k, lnx_g, lnx_b, attn_out_g, w_out, w_mem_q, w_mem_kv, w_mem_o,
           w_router, b_router, w_exp_gate, w_exp_up, w_exp_down, ln_g, ln_b):
    batch, seq, d = x.shape
    n = batch * seq
    mlen = mem.shape[1]
    zeros = jnp.zeros
    slopes = 2.0 ** (-8.0 * jnp.arange(1, N_HEADS + 1, dtype=F32) / N_HEADS)
    w_router_pad = jnp.pad(w_router, ((0, 0), (0, LANES - N_EXPERTS)))
    b_router_pad = jnp.pad(b_router.reshape(1, N_EXPERTS), ((0, 7), (0, LANES - N_EXPERTS)))
    mem2d = mem.reshape(batch * mlen, d)
    n_shift = w_in.shape[2] - 3 * WIDTH
    n_lora = n_shift - 3 * WIDTH
    n_vres = w_in_vres.shape[2]
    x2d = x.reshape(n, d)
    v_first = None
    for l in range(DEPTH):
        has_vres = l > 0
        w_extra = w_in_vres[l - 1] if has_vres else zeros((d, n_vres), F32)
        w_pad = jnp.concatenate(
            [w_in[l], w_extra, zeros((d, IN_COLS_PAD - w_in.shape[2] - n_vres), F32)], axis=1).astype(BF16)
        z_att, z_rkv, z_lora = _inproj(x2d, w_pad)
        o_att = _moba(z_att.reshape(batch, seq, 3 * WIDTH), slopes, attn_out_g[l].reshape(1, WIDTH), batch, seq)

        mu_rkv = mu_shift[l, :3 * WIDTH].reshape(1, 3 * WIDTH)
        mu_extra = mu_vres[l - 1] if has_vres else zeros((n_vres,), F32)
        mu_lora = jnp.concatenate(
            [mu_shift[l, 3 * WIDTH:], mu_extra, zeros((LORA_COLS - n_lora - n_vres,), F32)]).reshape(1, LORA_COLS)
        v0_l = v0[l - 1] if has_vres else zeros((WIDTH,), F32)
        vecs = jnp.stack([w0[l], a0[l], v0_l, k_k[l], k_a[l], r_k[l].reshape(WIDTH),
                          zeros((WIDTH,), F32), zeros((WIDTH,), F32)])
        n_dec, n_iclr = w_decay_up.shape[1], w_iclr_up.shape[1]
        wd = _pad_rows(w_decay_up[l], 0, LANES).astype(BF16)
        wa = _pad_rows(w_iclr_up[l], n_dec, LANES).astype(BF16)
        wg = w_gate_up[l].astype(BF16)
        wv = _pad_rows(w_vres_up[l - 1], 0, LANES).astype(BF16) if has_vres else None
        pre = _rwkv_pre(z_rkv.reshape(batch, seq, 3 * WIDTH), z_lora.reshape(batch, seq, LORA_COLS),
                        v_first, mu_rkv, mu_lora, vecs, wd, wa, wg, wv, batch, seq)
        if not has_vres:
            v_first = pre[5]
        lnx = jnp.concatenate([lnx_g[l][None], lnx_b[l][None], zeros((6, WIDTH), F32)])
        o_rwkv = _rwkv_scan(pre, lnx, batch, seq)

        ln = lambda j: jnp.concatenate([ln_g[l, j][None], ln_b[l, j][None], zeros((6, d), F32)])
        x1 = _outproj(x2d, o_att.reshape(n, WIDTH), o_rwkv.reshape(n, WIDTH), w_out[l].astype(BF16), ln(0))

        kv = _kvproj(mem2d, w_mem_kv[l].astype(BF16)).reshape(batch, mlen, 2 * d)
        x2, route = _memattn(x1.reshape(batch, seq, d), kv, w_mem_q[l].astype(BF16),
                             w_mem_o[l].astype(BF16), ln(1), w_router_pad, b_router_pad)
        x2d = _moe(x2.reshape(n, d), route.reshape(n, LANES), w_exp_gate[l], w_exp_up[l], w_exp_down[l], ln(2))
    return x2d.reshape(batch, seq, d)
```

```python
import functools

import jax
import jax.numpy as jnp
from jax import lax
from jax.experimental import pallas as pl
from jax.experimental.pallas import tpu as pltpu

F32 = jnp.float32
BF16 = jnp.bfloat16
I32 = jnp.int32

D_MODEL = 1024
DEPTH = 2
N_HEADS = 8
HEAD_DIM = 64
WIDTH = N_HEADS * HEAD_DIM
N_PAIRS = N_HEADS // 2
LANES = 128
MOBA_BLOCK = 256
MOBA_TOP = 3
GATE_ROWS = 16
POS_BLK = 16
POS_IN = 17
LORA_COLS = 384
IN_COLS_PAD = 3 * WIDTH + 3 * WIDTH + LORA_COLS
RWKV_GN_EPS = 64e-5
N_MEM_HEADS = 4
MEM_HEAD_DIM = D_MODEL // N_MEM_HEADS
N_EXPERTS = 32
N_GROUPS = 8
D_EXPERT = 512
LN_EPS = 1e-5
RMS_EPS = 1e-6
NEG = -1e30
ALPHA = (2 * DEPTH) ** 0.25
CHUNK = 64
MOE_ROWS = 256

VMEM_LIMIT = 48 * 1024 * 1024

NT_DIMS = (((1,), (1,)), ((), ()))
TN_DIMS = (((0,), (0,)), ((), ()))


def _cparams(sem):
    return pltpu.CompilerParams(dimension_semantics=sem, vmem_limit_bytes=VMEM_LIMIT)


def _split(a):
    hi = a.astype(BF16)
    lo = (a - hi.astype(F32)).astype(BF16)
    return hi, lo


def _dot1(a, b, dims=(((1,), (0,)), ((), ()))):
    return lax.dot_general(a.astype(BF16), b.astype(BF16), dims, preferred_element_type=F32)


def _layer_norm(y, g, b):
    mu = jnp.mean(y, -1, keepdims=True)
    d = y - mu
    var = jnp.mean(d * d, -1, keepdims=True)
    return d * lax.rsqrt(var + LN_EPS) * g + b


def _sigmoid(x):
    return 1.0 / (1.0 + jnp.exp(-x))


def _inproj_kernel(x_ref, w_ref, att_ref, rkv_ref, lora_ref):
    xb = x_ref[...].astype(BF16)
    att_ref[...] = jnp.dot(xb, w_ref[:, 0:3 * WIDTH], preferred_element_type=F32)
    rkv_ref[...] = jnp.dot(xb, w_ref[:, 3 * WIDTH:6 * WIDTH], preferred_element_type=F32)
    lora_ref[...] = jnp.dot(xb, w_ref[:, 6 * WIDTH:IN_COLS_PAD], preferred_element_type=F32)


def _inproj(x2d, w_pad, tm=256):
    n = x2d.shape[0]
    return pl.pallas_call(
        _inproj_kernel,
        grid=(n // tm,),
        in_specs=[pl.BlockSpec((tm, D_MODEL), lambda i: (i, 0)),
                  pl.BlockSpec((D_MODEL, IN_COLS_PAD), lambda i: (0, 0))],
        out_specs=[pl.BlockSpec((tm, 3 * WIDTH), lambda i: (i, 0)),
                   pl.BlockSpec((tm, 3 * WIDTH), lambda i: (i, 0)),
                   pl.BlockSpec((tm, LORA_COLS), lambda i: (i, 0))],
        out_shape=[jax.ShapeDtypeStruct((n, 3 * WIDTH), F32),
                   jax.ShapeDtypeStruct((n, 3 * WIDTH), F32),
                   jax.ShapeDtypeStruct((n, LORA_COLS), F32)],
        compiler_params=_cparams(("parallel",)),
        name="inproj",
    )(x2d, w_pad)


def _moba_kernel(slopes_ref, q_ref, k_ref, v_ref, gain_ref, o_ref, kaug_ref, vb_ref, kmp_ref, *, seq):
    pair = pl.program_id(1)
    i = pl.program_id(2)
    nb = seq // MOBA_BLOCK
    blk = MOBA_BLOCK

    @pl.when(i == 0)
    def _():
        k = k_ref[...]
        rowi = lax.broadcasted_iota(I32, (seq, LANES), 0)
        rowblk = lax.shift_right_logical(rowi, 8)
        rowin = rowi & (blk - 1)
        lane_s = lax.broadcasted_iota(I32, (seq, LANES), 1)
        kmean = jnp.sum(k.reshape(nb, blk, LANES), axis=1) * (1.0 / blk)
        for hh in range(2):
            slope = slopes_ref[2 * pair + hh]
            ob = HEAD_DIM * (1 - hh)
            inhead = (lane_s >= HEAD_DIM * hh) & (lane_s < HEAD_DIM * (hh + 1))
            c = lane_s - ob
            aug = jnp.where(c == rowblk, 1.0, 0.0)
            aug = jnp.where(c == POS_BLK, (slope * blk) * rowblk.astype(F32), aug)
            aug = jnp.where(c == POS_IN, slope * rowin.astype(F32), aug)
            kaug_ref[hh] = jnp.where(inhead, k, aug).astype(BF16)
            kmp_ref[hh] = jnp.zeros((LANES, LANES), F32)
            kmp_ref[hh, ob:ob + nb, :] = kmean
        vb_ref[...] = v_ref[...].astype(BF16)

    def query_block(ii):
        q = q_ref[...] * (HEAD_DIM ** -0.5)
        causal = (lax.broadcasted_iota(I32, (blk, blk), 0) >= lax.broadcasted_iota(I32, (blk, blk), 1))
        lane = lax.broadcasted_iota(I32, (blk, LANES), 1)
        gain = gain_ref[...]
        nk = (ii + 1) * blk
        inheads = [(lane >= HEAD_DIM * hh) & (lane < HEAD_DIM * (hh + 1)) for hh in range(2)]
        qas = []
        for hh in range(2):
            ob = HEAD_DIM * (1 - hh)
            inhead = inheads[hh]
            c = lane - ob
            qaug = jnp.where((c == POS_BLK) | (c == POS_IN), 1.0, 0.0)
            if ii > MOBA_TOP:
                qm = jnp.where(inhead, q, 0.0)
                gate = _dot1(kmp_ref[hh], qm, NT_DIMS)[ob:ob + GATE_ROWS, :]
                jrow = lax.broadcasted_iota(I32, (GATE_ROWS, blk), 0)
                cnt = jnp.zeros((GATE_ROWS, blk), I32)
                for jp in range(ii):
                    rowv = gate[jp:jp + 1, :]
                    beats = (rowv > gate) | ((rowv == gate) & (jp < jrow))
                    cnt = cnt + jnp.where(beats, 1, 0)
                drop = jnp.where((jrow < ii) & (cnt >= MOBA_TOP), NEG, 0.0).astype(BF16)
                place = (lax.broadcasted_iota(I32, (GATE_ROWS, LANES), 1) - ob
                         == lax.broadcasted_iota(I32, (GATE_ROWS, LANES), 0))
                qaug = qaug + lax.dot_general(drop, jnp.where(place, 1.0, 0.0).astype(BF16), TN_DIMS,
                                              preferred_element_type=F32)
            qas.append(jnp.where(inhead, q, qaug).astype(BF16))

        scores = [lax.dot_general(qas[hh], kaug_ref[hh, 0:nk, :], NT_DIMS, preferred_element_type=F32)
                  for hh in range(2)]
        probs, sums = [], []
        for s in scores:
            parts = [s[:, j * blk:(j + 1) * blk] for j in range(ii)]
            parts.append(jnp.where(causal, s[:, ii * blk:nk], NEG))
            m = jnp.max(functools.reduce(jnp.maximum, parts), -1, keepdims=True)
            ps = [jnp.exp(x - m) for x in parts]
            sums.append(jnp.sum(functools.reduce(lambda a, b: a + b, ps), -1, keepdims=True))
            probs.append(jnp.concatenate([x.astype(BF16) for x in ps], axis=1))
        outs = []
        for hh in range(2):
            out = jnp.dot(probs[hh], vb_ref[0:nk, :], preferred_element_type=F32) / sums[hh]
            ms = jnp.sum(jnp.where(inheads[hh], out * out, 0.0), -1, keepdims=True) * (1.0 / HEAD_DIM)
            outs.append(out * lax.rsqrt(ms + RMS_EPS) * gain)
        o_ref[...] = jnp.where(lane < HEAD_DIM, outs[0], outs[1])

    for ii in range(nb):
        pl.when(i == ii)(functools.partial(query_block, ii))


def _moba(z_att, slopes, gain, batch, seq):
    nb = seq // MOBA_BLOCK
    grid_spec = pltpu.PrefetchScalarGridSpec(
        num_scalar_prefetch=1,
        grid=(batch, N_PAIRS, nb),
        in_specs=[pl.BlockSpec((None, MOBA_BLOCK, LANES), lambda b, p, i, s: (b, i, p)),
                  pl.BlockSpec((None, seq, LANES), lambda b, p, i, s: (b, 0, N_PAIRS + p)),
                  pl.BlockSpec((None, seq, LANES), lambda b, p, i, s: (b, 0, 2 * N_PAIRS + p)),
                  pl.BlockSpec((1, LANES), lambda b, p, i, s: (0, p))],
        out_specs=pl.BlockSpec((None, MOBA_BLOCK, LANES), lambda b, p, i, s: (b, i, p)),
        scratch_shapes=[pltpu.VMEM((2, seq, LANES), BF16),
                        pltpu.VMEM((seq, LANES), BF16),
                        pltpu.VMEM((2, LANES, LANES), F32)],
    )
    return pl.pallas_call(
        functools.partial(_moba_kernel, seq=seq),
        grid_spec=grid_spec,
        out_shape=jax.ShapeDtypeStruct((batch, seq, WIDTH), F32),
        compiler_params=_cparams(("parallel", "parallel", "arbitrary")),
        name="moba",
    )(slopes, z_att, z_att, z_att, gain)


def _dot2x(a, b_exact):
    ah, al = _split(a)
    return (jnp.dot(ah, b_exact, preferred_element_type=F32)
            + jnp.dot(al, b_exact, preferred_element_type=F32))


def _head_block_diag(scale):
    r = lax.shift_right_logical(lax.broadcasted_iota(I32, (LANES, LANES), 0), 6)
    c = lax.shift_right_logical(lax.broadcasted_iota(I32, (LANES, LANES), 1), 6)
    return jnp.where(r == c, scale, 0.0).astype(BF16)


def _softplus(x):
    return jnp.maximum(x, 0.0) + jnp.log(1.0 + jnp.exp(-jnp.abs(x)))


def _rwkv_chunks(items):
    c_ = CHUNK
    row = lax.broadcasted_iota(I32, (c_, c_), 0)
    col = lax.broadcasted_iota(I32, (c_, c_), 1)
    incl = row >= col
    strict = row > col
    lane = lax.broadcasted_iota(I32, (c_, LANES), 1)
    m0 = lane < HEAD_DIM
    tri = jnp.where(incl, 1.0, 0.0).astype(BF16)
    eye = jnp.where(row == col, 1.0, 0.0)
    splits = [_split(it[0]) for it in items]
    cs = [jnp.dot(tri, hi, preferred_element_type=F32) + jnp.dot(tri, lo, preferred_element_type=F32)
          for hi, lo in splits]
    base = []
    for (lw, r, k2, v, a, b), c in zip(items, cs):
        e_pos = jnp.exp(c)
        e_neg = jnp.exp(-c)
        at = a * jnp.exp(c - lw)
        bt = b * e_neg
        kt = k2 * e_neg
        rt = r * e_pos
        gc = e_pos[c_ - 1:c_, :]
        vs = jnp.concatenate([jnp.where(m0, v, 0.0), jnp.where(m0, 0.0, v)], axis=0)
        base.append(dict(at=at, bt=bt, kt=kt, rt=rt, gc=gc, bp=bt * gc, kp=kt * gc, vs=vs))
    heads = []
    for d in base:
        for hh in range(2):
            mh = m0 if hh == 0 else jnp.logical_not(m0)
            am = jnp.where(mh, d['at'], 0.0)
            rm = jnp.where(mh, d['rt'], 0.0)
            heads.append(dict(
                lab=jnp.where(strict, _dot1(am, d['bt'], NT_DIMS), 0.0),
                lak=jnp.where(strict, _dot1(am, d['kt'], NT_DIMS), 0.0),
                qb=jnp.where(incl, _dot1(rm, d['bt'], NT_DIMS), 0.0),
                qk=jnp.where(incl, _dot1(rm, d['kt'], NT_DIMS), 0.0)))
    ts = [eye + h['lab'] for h in heads]
    lps = [h['lab'] for h in heads]
    for _ in range(5):
        lps = [_dot1(lp, lp) for lp in lps]
        ts = [t + _dot1(t, lp) for t, lp in zip(ts, lps)]
    outs = []
    for i, d in enumerate(base):
        h0, h1 = heads[2 * i], heads[2 * i + 1]
        tcat = jnp.concatenate([ts[2 * i], ts[2 * i + 1]], axis=1)
        qcat = jnp.concatenate([h0['qb'], h1['qb']], axis=1)
        p = _dot1(jnp.concatenate([h0['lak'], h1['lak']], axis=1), d['vs'])
        yk = _dot1(jnp.concatenate([h0['qk'], h1['qk']], axis=1), d['vs'])
        grow = jnp.broadcast_to(d['gc'], (c_, LANES))
        outs.append((d['at'].astype(BF16), p, tcat.astype(BF16), d['bp'].astype(BF16), d['kp'].astype(BF16),
                     grow, d['rt'].astype(BF16), qcat.astype(BF16), yk))
    return outs


def _rwkv_pre_kernel(*refs, has_vres, tb):
    if has_vres:
        (rkv_ref, lora_ref, vf_ref, mu_rkv_ref, mu_lora_ref, vec_ref, wd_ref, wa_ref, wg_ref, wv_ref,
         at_ref, p_ref, tc_ref, bp_ref, kp_ref, v_ref, gr_ref, rt_ref, qc_ref, yk_ref, bonus_ref, g_ref,
         carry_ref, carryl_ref) = refs
    else:
        (rkv_ref, lora_ref, mu_rkv_ref, mu_lora_ref, vec_ref, wd_ref, wa_ref, wg_ref,
         at_ref, p_ref, tc_ref, bp_ref, kp_ref, v_ref, gr_ref, rt_ref, qc_ref, yk_ref, bonus_ref, g_ref,
         carry_ref, carryl_ref) = refs
    t_idx = pl.program_id(1)

    @pl.when(t_idx == 0)
    def _():
        carry_ref[...] = jnp.zeros_like(carry_ref)
        carryl_ref[...] = jnp.zeros_like(carryl_ref)

    def token_shift(z, cref, mu):
        first = lax.broadcasted_iota(I32, z.shape, 0) == 0
        prev = jnp.where(first, cref[0:1, :], pltpu.roll(z, 1, 0))
        cref[0:1, :] = z[tb - 1:tb, :]
        return z + (prev - z) * mu

    zs = token_shift(rkv_ref[...], carry_ref, mu_rkv_ref[...])
    zl = token_shift(lora_ref[...], carryl_ref, mu_lora_ref[...])
    l0 = zl[:, 0:LANES]
    w0, a0, v0 = vec_ref[0:1, :], vec_ref[1:2, :], vec_ref[2:3, :]
    k_k, k_a, r_k = vec_ref[3:4, :], vec_ref[4:5, :], vec_ref[5:6, :]
    dw = _dot1(jnp.tanh(l0), wd_ref[...])
    lw_all = -jnp.exp(-_softplus(-(w0 + dw)) - 0.5)
    a_lr = _sigmoid(a0 + _dot1(l0, wa_ref[...]))
    g_ref[...] = _dot1(_sigmoid(zl[:, LANES:2 * LANES]), wg_ref[...])
    r_all = zs[:, 0:WIDTH]
    k_all = zs[:, WIDTH:2 * WIDTH]
    v_all = zs[:, 2 * WIDTH:3 * WIDTH]
    if has_vres:
        mix = _sigmoid(v0 + _dot1(zl[:, 2 * LANES:3 * LANES], wv_ref[...]))
        v_all = v_all + (vf_ref[...] - v_all) * mix
    v_ref[...] = v_all
    bd = _head_block_diag(1.0)
    items, where = [], []
    for pp in range(N_PAIRS):
        sl = slice(pp * LANES, (pp + 1) * LANES)
        r, k, v = r_all[:, sl], k_all[:, sl], v_all[:, sl]
        alr = a_lr[:, sl]
        kk = k * k_k[:, sl]
        ss = _dot2x(kk * kk, bd)
        kk = kk * lax.rsqrt(jnp.maximum(ss, 1e-24))
        k2 = k * (1.0 + (alr - 1.0) * k_a[:, sl])
        bonus_ref[:, sl] = _dot2x(r * k2 * r_k[:, sl], bd) * v
        a = -kk
        b = kk * alr
        lw = lw_all[:, sl]
        for ci in range(tb // CHUNK):
            rs = slice(ci * CHUNK, (ci + 1) * CHUNK)
            items.append((lw[rs], r[rs], k2[rs], v[rs], a[rs], b[rs]))
            where.append((rs, sl))
    for (rs, sl), outs in zip(where, _rwkv_chunks(items)):
        for ref, val in zip((at_ref, p_ref, tc_ref, bp_ref, kp_ref, gr_ref, rt_ref, qc_ref, yk_ref), outs):
            ref[rs, sl] = val


def _rwkv_pre(z_rkv, z_lora, v_first, mu_rkv, mu_lora, vecs, wd, wa, wg, wv, batch, seq, tb=128):
    has_vres = v_first is not None
    big = lambda w: pl.BlockSpec((None, tb, w), lambda b, t: (b, t, 0))
    full = lambda shape: pl.BlockSpec(shape, lambda b, t: (0,) * len(shape))
    in_specs = [big(3 * WIDTH), big(LORA_COLS)] + ([big(WIDTH)] if has_vres else [])
    in_specs += [full((1, 3 * WIDTH)), full((1, LORA_COLS)), full((8, WIDTH)),
                 full((LANES, WIDTH)), full((LANES, WIDTH)), full((LANES, WIDTH))]
    args = [z_rkv, z_lora] + ([v_first] if has_vres else []) + [mu_rkv, mu_lora, vecs, wd, wa, wg]
    if has_vres:
        in_specs.append(full((LANES, WIDTH)))
        args.append(wv)
    out_dtypes = [BF16, F32, BF16, BF16, BF16, F32, F32, BF16, BF16, F32, F32, F32]
    return pl.pallas_call(
        functools.partial(_rwkv_pre_kernel, has_vres=has_vres, tb=tb),
        grid=(batch, seq // tb),
        in_specs=in_specs,
        out_specs=[big(WIDTH)] * len(out_dtypes),
        out_shape=[jax.ShapeDtypeStruct((batch, seq, WIDTH), dt) for dt in out_dtypes],
        scratch_shapes=[pltpu.VMEM((8, 3 * WIDTH), F32), pltpu.VMEM((8, LORA_COLS), F32)],
        compiler_params=_cparams(("parallel", "arbitrary")),
        name="rwkv_pre",
    )(*args)


def _rwkv_scan_kernel(at_ref, p_ref, tc_ref, bp_ref, kp_ref, v_ref, gr_ref, rt_ref, qc_ref, yk_ref,
                      bonus_ref, g_ref, lnx_ref, o_ref, st_ref, *, ts):
    t_idx = pl.program_id(1)

    @pl.when(t_idx == 0)
    def _():
        st_ref[...] = jnp.zeros_like(st_ref)

    lane = lax.broadcasted_iota(I32, (CHUNK, LANES), 1)
    m0 = lane < HEAD_DIM
    bdmask = (lax.shift_right_logical(lax.broadcasted_iota(I32, (LANES, LANES), 0), 6)
              == lax.shift_right_logical(lax.broadcasted_iota(I32, (LANES, LANES), 1), 6))
    ones_t = jnp.full((CHUNK, LANES), 1.0 / CHUNK, BF16)

    def stack(x):
        return jnp.concatenate([jnp.where(m0, x, 0.0), jnp.where(m0, 0.0, x)], axis=0)

    tn = functools.partial(lax.dot_general, dimension_numbers=TN_DIMS, preferred_element_type=F32)
    nseq = at_ref.shape[0]
    chains = [(bb, pp, slice(pp * LANES, (pp + 1) * LANES)) for bb in range(nseq) for pp in range(N_PAIRS)]
    chunks = [slice(ci * CHUNK, (ci + 1) * CHUNK) for ci in range(ts // CHUNK)]

    gcols, kvs = {}, {}
    for ci, rs in enumerate(chunks):
        for bb, pp, sl in chains:
            gh, gl = _split(gr_ref[bb, rs, sl])
            gcols[ci, bb, pp] = tn(gh, ones_t) + tn(gl, ones_t)
            kvs[ci, bb, pp] = jnp.where(bdmask, _dot1(kp_ref[bb, rs, sl], v_ref[bb, rs, sl], TN_DIMS), 0.0)

    states = [st_ref[bb, pp] for bb, pp, _ in chains]
    for ci, rs in enumerate(chunks):
        sts = [st.astype(BF16) for st in states]
        xs = [_dot1(at_ref[bb, rs, sl], sts[n]) + p_ref[bb, rs, sl] for n, (bb, pp, sl) in enumerate(chains)]
        us = [_dot1(tc_ref[bb, rs, sl], stack(xs[n])).astype(BF16) for n, (bb, pp, sl) in enumerate(chains)]
        new = [gcols[ci, bb, pp] * states[n]
               + (jnp.where(bdmask, _dot1(bp_ref[bb, rs, sl], us[n], TN_DIMS), 0.0) + kvs[ci, bb, pp])
               for n, (bb, pp, sl) in enumerate(chains)]
        for n, (bb, pp, sl) in enumerate(chains):
            lhs = jnp.concatenate([rt_ref[bb, rs, sl], qc_ref[bb, rs, sl]], axis=1)
            rhs = jnp.concatenate([sts[n], stack(us[n])], axis=0)
            o_ref[bb, rs, sl] = jnp.dot(lhs, rhs, preferred_element_type=F32) + yk_ref[bb, rs, sl]
        states = new
    for n, (bb, pp, _) in enumerate(chains):
        st_ref[bb, pp] = states[n]

    bd = _head_block_diag(1.0 / HEAD_DIM)
    for bb, pp, sl in chains:
        y = o_ref[bb, :, sl]
        mu = _dot2x(y, bd)
        d = y - mu
        var = _dot2x(d * d, bd)
        yn = d * lax.rsqrt(var + RWKV_GN_EPS) * lnx_ref[0:1, sl] + lnx_ref[1:2, sl]
        o_ref[bb, :, sl] = (yn + bonus_ref[bb, :, sl]) * g_ref[bb, :, sl]


def _rwkv_scan(pre, lnx, batch, seq, ts=128):
    nseq = next(c for c in (4, 2, 1) if batch % c == 0)
    big = pl.BlockSpec((nseq, ts, WIDTH), lambda b, t: (b, t, 0))
    return pl.pallas_call(
        functools.partial(_rwkv_scan_kernel, ts=ts),
        grid=(batch // nseq, seq // ts),
        in_specs=[big] * 12 + [pl.BlockSpec((8, WIDTH), lambda b, t: (0, 0))],
        out_specs=big,
        out_shape=jax.ShapeDtypeStruct((batch, seq, WIDTH), F32),
        scratch_shapes=[pltpu.VMEM((nseq, N_PAIRS, LANES, LANES), F32)],
        compiler_params=_cparams(("parallel", "arbitrary")),
        name="rwkv_scan",
    )(*pre, lnx)


def _outproj_kernel(x_ref, oa_ref, orw_ref, w_ref, ln_ref, o_ref):
    y = (jnp.dot(oa_ref[...].astype(BF16), w_ref[0:WIDTH, :], preferred_element_type=F32)
         + jnp.dot(orw_ref[...].astype(BF16), w_ref[WIDTH:2 * WIDTH, :], preferred_element_type=F32))
    o_ref[...] = _layer_norm(ALPHA * x_ref[...] + y, ln_ref[0:1, :], ln_ref[1:2, :])


def _outproj(x2d, o_att, o_rwkv, w_out, ln, tm=512):
    n = x2d.shape[0]
    return pl.pallas_call(
        _outproj_kernel,
        grid=(n // tm,),
        in_specs=[pl.BlockSpec((tm, D_MODEL), lambda i: (i, 0)),
                  pl.BlockSpec((tm, WIDTH), lambda i: (i, 0)),
                  pl.BlockSpec((tm, WIDTH), lambda i: (i, 0)),
                  pl.BlockSpec((2 * WIDTH, D_MODEL), lambda i: (0, 0)),
                  pl.BlockSpec((8, D_MODEL), lambda i: (0, 0))],
        out_specs=pl.BlockSpec((tm, D_MODEL), lambda i: (i, 0)),
        out_shape=jax.ShapeDtypeStruct((n, D_MODEL), F32),
        compiler_params=_cparams(("parallel",)),
        name="outproj_ln",
    )(x2d, o_att, o_rwkv, w_out, ln)


def _kvproj_kernel(m_ref, w_ref, o_ref):
    o_ref[...] = jnp.dot(m_ref[...].astype(BF16), w_ref[...], preferred_element_type=F32).astype(BF16)


def _kvproj(mem2d, w_kv, tm=256):
    n = mem2d.shape[0]
    return pl.pallas_call(
        _kvproj_kernel,
        grid=(n // tm,),
        in_specs=[pl.BlockSpec((tm, D_MODEL), lambda i: (i, 0)),
                  pl.BlockSpec((D_MODEL, 2 * D_MODEL), lambda i: (0, 0))],
        out_specs=pl.BlockSpec((tm, 2 * D_MODEL), lambda i: (i, 0)),
        out_shape=jax.ShapeDtypeStruct((n, 2 * D_MODEL), BF16),
        compiler_params=_cparams(("parallel",)),
        name="kvproj",
    )(mem2d, w_kv)


def _route_lanes(logits, b_router):
    lane = lax.broadcasted_iota(I32, logits.shape, 1)
    lane_f = lane.astype(F32)
    valid = lane < N_EXPERTS
    s = _sigmoid(logits)
    sel = jnp.where(valid, s + b_router, NEG)

    def partner(x, bit):
        return jnp.where((lane & bit) == 0, pltpu.roll(x, LANES - bit, 1), pltpu.roll(x, bit, 1))

    p1 = partner(sel, 1)
    hi1, lo1 = jnp.maximum(sel, p1), jnp.minimum(sel, p1)
    hi2, lo2 = partner(hi1, 2), partner(lo1, 2)
    gsum = jnp.maximum(hi1, hi2) + jnp.maximum(jnp.minimum(hi1, hi2), jnp.maximum(lo1, lo2))
    gmax = jnp.max(gsum, -1, keepdims=True)
    grp_f = lax.shift_right_logical(lane, 2).astype(F32)
    g_best = jnp.min(jnp.where((gsum == gmax) & valid, grp_f, 1e9), -1, keepdims=True)
    in_grp = (grp_f == g_best) & valid
    masked = jnp.where(in_grp, sel, NEG)
    v1 = jnp.max(masked, -1, keepdims=True)
    e1 = jnp.min(jnp.where(in_grp & (masked == v1), lane_f, 1e9), -1, keepdims=True)
    rest = in_grp & (lane_f != e1)
    masked2 = jnp.where(rest, sel, NEG)
    v2 = jnp.max(masked2, -1, keepdims=True)
    e2 = jnp.min(jnp.where(rest & (masked2 == v2), lane_f, 1e9), -1, keepdims=True)
    w1 = jnp.sum(jnp.where(lane_f == e1, s, 0.0), -1, keepdims=True)
    w2 = jnp.sum(jnp.where(lane_f == e2, s, 0.0), -1, keepdims=True)
    tot = w1 + w2
    out = jnp.where(lane == 0, e1, jnp.where(lane == 1, e2, jnp.where(lane == 2, w1 / tot, w2 / tot)))
    return jnp.where(lane < 4, out, 0.0)


def _memattn_kernel(x_ref, kv_ref, wq_ref, wo_ref, ln_ref, wr_ref, br_ref, o_ref, route_ref):
    x = x_ref[...]
    q = jnp.dot(x.astype(BF16), wq_ref[...], preferred_element_type=F32) * (MEM_HEAD_DIM ** -0.5)
    heads = []
    for h in range(N_MEM_HEADS):
        sl = slice(h * MEM_HEAD_DIM, (h + 1) * MEM_HEAD_DIM)
        kh = kv_ref[:, sl]
        vh = kv_ref[:, D_MODEL + h * MEM_HEAD_DIM:D_MODEL + (h + 1) * MEM_HEAD_DIM]
        s = lax.dot_general(q[:, sl].astype(BF16), kh, NT_DIMS, preferred_element_type=F32)
        p = jnp.exp(s - jnp.max(s, -1, keepdims=True))
        o = jnp.dot(p.astype(BF16), vh, preferred_element_type=F32)
        heads.append(o / jnp.sum(p, -1, keepdims=True))
    o = jnp.concatenate(heads, axis=-1).astype(BF16)
    y = jnp.dot(o, wo_ref[...], preferred_element_type=F32)
    x2 = _layer_norm(ALPHA * x + y, ln_ref[0:1, :], ln_ref[1:2, :])
    o_ref[...] = x2
    logits = _dot1(x2, wr_ref[...])
    route_ref[...] = _route_lanes(logits, br_ref[0:1, :])


def _memattn(x3d, kv3d, wq, wo, ln, w_router_pad, b_router_pad, tm=512):
    batch, seq, _ = x3d.shape
    mlen = kv3d.shape[1]
    full = lambda shape: pl.BlockSpec(shape, lambda b, t: (0,) * len(shape))
    return pl.pallas_call(
        _memattn_kernel,
        grid=(batch, seq // tm),
        in_specs=[pl.BlockSpec((None, tm, D_MODEL), lambda b, t: (b, t, 0)),
                  pl.BlockSpec((None, mlen, 2 * D_MODEL), lambda b, t: (b, 0, 0)),
                  full((D_MODEL, D_MODEL)), full((D_MODEL, D_MODEL)), full((8, D_MODEL)),
                  full((D_MODEL, LANES)), full((8, LANES))],
        out_specs=[pl.BlockSpec((None, tm, D_MODEL), lambda b, t: (b, t, 0)),
                   pl.BlockSpec((None, tm, LANES), lambda b, t: (b, t, 0))],
        out_shape=[jax.ShapeDtypeStruct((batch, seq, D_MODEL), F32),
                   jax.ShapeDtypeStruct((batch, seq, LANES), F32)],
        compiler_params=_cparams(("parallel", "parallel")),
        name="memattn_ln",
    )(x3d, kv3d, wq, wo, ln, w_router_pad, b_router_pad)


def _plan_kernel(route_ref, dest_ref, stat_ref, acc_ref):
    phase = pl.program_id(0)
    i = pl.program_id(1)
    tm = route_ref.shape[0]
    lane = lax.broadcasted_iota(I32, (tm, LANES), 1).astype(F32)
    e0 = route_ref[:, 0:1]
    e1 = route_ref[:, 1:2]
    hot = jnp.where((lane == e0) | (lane == e1), 1.0, 0.0)

    @pl.when((phase == 0) & (i == 0))
    def _():
        acc_ref[...] = jnp.zeros_like(acc_ref)

    @pl.when(phase == 0)
    def _():
        acc_ref[0:1, :] = acc_ref[0:1, :] + jnp.sum(hot, axis=0, keepdims=True)

    @pl.when((phase == 1) & (i == 0))
    def _():
        r = lax.broadcasted_iota(I32, (LANES, LANES), 0)
        c = lax.broadcasted_iota(I32, (LANES, LANES), 1)
        before = jnp.where(r < c, 1.0, 0.0)
        cnt8 = jnp.broadcast_to(acc_ref[0:1, :], (8, LANES))
        acc_ref[1:2, :] = _dot2x(cnt8, before.astype(BF16))[0:1, :]

    @pl.when(phase == 1)
    def _():
        r = lax.broadcasted_iota(I32, (tm, tm), 0)
        c = lax.broadcasted_iota(I32, (tm, tm), 1)
        earlier = jnp.where(c < r, 1.0, 0.0).astype(BF16)
        pos = (jnp.dot(earlier, hot.astype(BF16), preferred_element_type=F32)
               + acc_ref[1:2, :] + acc_ref[2:3, :])
        d0 = jnp.sum(jnp.where(lane == e0, pos, 0.0), -1, keepdims=True)
        d1 = jnp.sum(jnp.where(lane == e1, pos, 0.0), -1, keepdims=True)
        dest_ref[...] = jnp.where(lane == 0.0, d0, jnp.where(lane == 1.0, d1, 0.0)).astype(I32)
        acc_ref[2:3, :] = acc_ref[2:3, :] + jnp.sum(hot, axis=0, keepdims=True)
        stat_ref[...] = acc_ref[...]


def _plan(route, tm=256):
    n = route.shape[0]
    return pl.pallas_call(
        _plan_kernel,
        grid=(2, n // tm),
        in_specs=[pl.BlockSpec((tm, LANES), lambda p, i: (i, 0))],
        out_specs=[pl.BlockSpec((tm, LANES), lambda p, i: (i * p, 0)),
                   pl.BlockSpec((8, LANES), lambda p, i: (0, 0))],
        out_shape=[jax.ShapeDtypeStruct((n, LANES), I32), jax.ShapeDtypeStruct((8, LANES), F32)],
        scratch_shapes=[pltpu.VMEM((8, LANES), F32)],
        compiler_params=_cparams(("arbitrary", "arbitrary")),
        name="moe_plan",
    )(route)


def _dispatch_kernel(dest_ref, prev_ref, x_ref, xs_hbm, stage_ref, sems):
    i = pl.program_id(0)
    tm = dest_ref.shape[2] // 2
    stage_ref[i % 2] = x_ref[...]

    def copy(step, dref, t, slot):
        src = stage_ref.at[step % 2, pl.ds(t, 1)]
        return pltpu.make_async_copy(src, xs_hbm.at[pl.ds(dref[0, 0, slot * tm + t], 1)], sems.at[step % 2])

    def start(t, c):
        copy(i, dest_ref, t, 0).start()
        copy(i, dest_ref, t, 1).start()
        return c

    def wait_step(step, dref):
        def wait(t, c):
            copy(step, dref, t, 0).wait()
            copy(step, dref, t, 1).wait()
            return c
        lax.fori_loop(0, tm, wait, 0, unroll=8)

    lax.fori_loop(0, tm, start, 0, unroll=8)

    @pl.when(i > 0)
    def _():
        wait_step(i - 1, prev_ref)

    @pl.when(i == pl.num_programs(0) - 1)
    def _():
        wait_step(i, dest_ref)


def _dispatch(x2d, dest3, n_rows):
    n_steps = dest3.shape[0]
    tm = dest3.shape[2] // 2
    blk = (1, 1, 2 * tm)
    return pl.pallas_call(
        _dispatch_kernel,
        grid=(n_steps,),
        in_specs=[pl.BlockSpec(blk, lambda i: (i, 0, 0), memory_space=pltpu.SMEM),
                  pl.BlockSpec(blk, lambda i: (jnp.maximum(i - 1, 0), 0, 0), memory_space=pltpu.SMEM),
                  pl.BlockSpec((tm, D_MODEL), lambda i: (i, 0))],
        out_specs=pl.BlockSpec(memory_space=pl.ANY),
        out_shape=jax.ShapeDtypeStruct((n_rows, D_MODEL), F32),
        scratch_shapes=[pltpu.VMEM((2, tm, D_MODEL), F32), pltpu.SemaphoreType.DMA((2,))],
        compiler_params=_cparams(("arbitrary",)),
        name="moe_dispatch",
    )(dest3, dest3, x2d)


def _experts_kernel(rb_ref, ex_ref, nit_ref, off_ref, cnt_ref, x_ref, wg_ref, wu_ref, wd_ref, o_ref,
                    wgb_ref, wub_ref, wdb_ref):
    w = pl.program_id(0)
    rows = x_ref.shape[0]

    @pl.when(w < nit_ref[0])
    def _():
        e = ex_ref[w]
        rb = rb_ref[w]
        prev = jnp.maximum(w - 1, 0)

        @pl.when((w == 0) | (ex_ref[prev] != e))
        def _():
            wgb_ref[...] = wg_ref[...].astype(BF16)
            wub_ref[...] = wu_ref[...].astype(BF16)
            wdb_ref[...] = wd_ref[...].astype(BF16)

        xb = x_ref[...].astype(BF16)
        hg = jnp.dot(xb, wgb_ref[...], preferred_element_type=F32)
        hu = jnp.dot(xb, wub_ref[...], preferred_element_type=F32)
        h = hg * _sigmoid(hg) * hu
        y = jnp.dot(h.astype(BF16), wdb_ref[...], preferred_element_type=F32)
        row = rb * rows + lax.broadcasted_iota(I32, (rows, 1), 0)
        mine = (row >= off_ref[e]) & (row < off_ref[e] + cnt_ref[e])
        first = (w == 0) | (rb_ref[prev] != rb)

        @pl.when(first)
        def _():
            o_ref[...] = jnp.where(mine, y, 0.0)

        @pl.when(jnp.logical_not(first))
        def _():
            o_ref[...] = jnp.where(mine, y, o_ref[...])


def _experts(xs, rb, ex, n_items, off, cnt, w_gate, w_up, w_down, layer, rows):
    n_work = rb.shape[0]
    item = lambda w, nit: jnp.minimum(w, nit[0] - 1)
    row_map = lambda w, rb, ex, nit, off, cnt: (rb[item(w, nit)], 0)
    w_map = lambda w, rb, ex, nit, off, cnt: (layer, ex[item(w, nit)], 0, 0)
    grid_spec = pltpu.PrefetchScalarGridSpec(
        num_scalar_prefetch=5,
        grid=(n_work,),
        in_specs=[pl.BlockSpec((rows, D_MODEL), row_map),
                  pl.BlockSpec((None, None, D_MODEL, D_EXPERT), w_map),
                  pl.BlockSpec((None, None, D_MODEL, D_EXPERT), w_map),
                  pl.BlockSpec((None, None, D_EXPERT, D_MODEL), w_map)],
        out_specs=pl.BlockSpec((rows, D_MODEL), row_map),
        scratch_shapes=[pltpu.VMEM((D_MODEL, D_EXPERT), BF16), pltpu.VMEM((D_MODEL, D_EXPERT), BF16),
                        pltpu.VMEM((D_EXPERT, D_MODEL), BF16)],
    )
    return pl.pallas_call(
        _experts_kernel,
        grid_spec=grid_spec,
        out_shape=jax.ShapeDtypeStruct(xs.shape, F32),
        compiler_params=_cparams(("arbitrary",)),
        name="moe_experts",
    )(rb, ex, n_items, off, cnt, xs, w_gate, w_up, w_down)


def _combine_kernel(dest_ref, next_ref, x_ref, gate_ref, ln_ref, y_hbm, o_ref, buf_ref, sems):
    i = pl.program_id(0)
    tm = x_ref.shape[0]

    def copy(step, dref, slot, t):
        src = y_hbm.at[pl.ds(dref[0, 0, slot * tm + t], 1)]
        return pltpu.make_async_copy(src, buf_ref.at[step % 2, slot, pl.ds(t, 1)], sems.at[step % 2])

    def start_step(step, dref):
        def start(t, c):
            copy(step, dref, 0, t).start()
            copy(step, dref, 1, t).start()
            return c
        lax.fori_loop(0, tm, start, 0, unroll=8)

    @pl.when(i == 0)
    def _():
        start_step(0, dest_ref)

    @pl.when(i + 1 < pl.num_programs(0))
    def _():
        start_step(i + 1, next_ref)

    def wait(t, c):
        copy(i, dest_ref, 0, t).wait()
        copy(i, dest_ref, 1, t).wait()
        return c

    lax.fori_loop(0, tm, wait, 0, unroll=8)
    cur = i % 2
    y = gate_ref[:, 0:1] * buf_ref[cur, 0] + gate_ref[:, 1:2] * buf_ref[cur, 1]
    o_ref[...] = _layer_norm(ALPHA * x_ref[...] + y, ln_ref[0:1, :], ln_ref[1:2, :])


def _combine(x2d, gates, dest3, ys, ln, tm=256):
    n = x2d.shape[0]
    n_steps = n // tm
    blk = (1, 1, 2 * tm)
    return pl.pallas_call(
        _combine_kernel,
        grid=(n_steps,),
        in_specs=[pl.BlockSpec(blk, lambda i: (i, 0, 0), memory_space=pltpu.SMEM),
                  pl.BlockSpec(blk, lambda i: (jnp.minimum(i + 1, n_steps - 1), 0, 0), memory_space=pltpu.SMEM),
                  pl.BlockSpec((tm, D_MODEL), lambda i: (i, 0)),
                  pl.BlockSpec((tm, 2), lambda i: (i, 0)),
                  pl.BlockSpec((8, D_MODEL), lambda i: (0, 0)),
                  pl.BlockSpec(memory_space=pl.ANY)],
        out_specs=pl.BlockSpec((tm, D_MODEL), lambda i: (i, 0)),
        out_shape=jax.ShapeDtypeStruct((n, D_MODEL), F32),
        scratch_shapes=[pltpu.VMEM((2, 2, tm, D_MODEL), F32), pltpu.SemaphoreType.DMA((2,))],
        compiler_params=_cparams(("arbitrary",)),
        name="moe_combine_ln",
    )(dest3, dest3, x2d, gates, ln, ys)


def _work_items(counts, off, n_rows, rows):
    n_work = n_rows // rows + N_EXPERTS
    first = off // rows
    last = (off + counts - 1) // rows
    per_expert = jnp.where(counts > 0, last - first + 1, 0)
    item_end = jnp.cumsum(per_expert)
    item_start = item_end - per_expert
    w = jnp.arange(n_work, dtype=I32)
    ex = jnp.minimum(jnp.sum((item_end[None, :] <= w[:, None]).astype(I32), axis=1), N_EXPERTS - 1)
    rb = jnp.clip(first[ex] + (w - item_start[ex]), 0, n_rows // rows - 1).astype(I32)
    return rb, ex, item_end[-1:].astype(I32)


def _moe(x2, route, w_gate, w_up, w_down, layer, ln, rows=MOE_ROWS, tm=256):
    n = x2.shape[0]
    n_rows = 2 * n
    gates = route[:, 2:4]
    dest_tile, stat = _plan(route)
    counts = stat[0, :N_EXPERTS].astype(I32)
    off = stat[1, :N_EXPERTS].astype(I32)
    rb, ex, n_items = _work_items(counts, off, n_rows, rows)
    dest3 = dest_tile[:, 0:2].reshape(n // tm, tm, 2).transpose(0, 2, 1).reshape(n // tm, 1, 2 * tm)
    xs = _dispatch(x2, dest3, n_rows)
    ys = _experts(xs, rb, ex, n_items, off, counts, w_gate, w_up, w_down, layer, rows)
    return _combine(x2, gates, dest3, ys, ln, tm)


def _pad_rows(w, rows_before, total):
    return jnp.pad(w, ((rows_before, total - rows_before - w.shape[0]), (0, 0)))


def kernel(x, mem, w_in, w_in_vres, mu_shift, mu_vres, w0, w_decay_up, a0, w_iclr_up, v0, w_vres_up,
           w_gate_up, k_k, k_a, r_k, lnx_g, lnx_b, attn_out_g, w_out, w_mem_q, w_mem_kv, w_mem_o,
           w_router, b_router, w_exp_gate, w_exp_up, w_exp_down, ln_g, ln_b):
    batch, seq, d = x.shape
    n = batch * seq
    mlen = mem.shape[1]
    zeros = jnp.zeros
    slopes = 2.0 ** (-8.0 * jnp.arange(1, N_HEADS + 1, dtype=F32) / N_HEADS)
    w_router_pad = jnp.pad(w_router, ((0, 0), (0, LANES - N_EXPERTS)))
    b_router_pad = jnp.pad(b_router.reshape(1, N_EXPERTS), ((0, 7), (0, LANES - N_EXPERTS)))
    mem2d = mem.reshape(batch * mlen, d)
    n_shift = w_in.shape[2] - 3 * WIDTH
    n_lora = n_shift - 3 * WIDTH
    n_vres = w_in_vres.shape[2]
    x2d = x.reshape(n, d)
    v_first = None
    for l in range(DEPTH):
        has_vres = l > 0
        w_extra = w_in_vres[l - 1] if has_vres else zeros((d, n_vres), F32)
        w_pad = jnp.concatenate(
            [w_in[l], w_extra, zeros((d, IN_COLS_PAD - w_in.shape[2] - n_vres), F32)], axis=1).astype(BF16)
        z_att, z_rkv, z_lora = _inproj(x2d, w_pad)
        o_att = _moba(z_att.reshape(batch, seq, 3 * WIDTH), slopes, attn_out_g[l].reshape(1, WIDTH), batch, seq)

        mu_rkv = mu_shift[l, :3 * WIDTH].reshape(1, 3 * WIDTH)
        mu_extra = mu_vres[l - 1] if has_vres else zeros((n_vres,), F32)
        mu_lora = jnp.concatenate(
            [mu_shift[l, 3 * WIDTH:], mu_extra, zeros((LORA_COLS - n_lora - n_vres,), F32)]).reshape(1, LORA_COLS)
        v0_l = v0[l - 1] if has_vres else zeros((WIDTH,), F32)
        vecs = jnp.stack([w0[l], a0[l], v0_l, k_k[l], k_a[l], r_k[l].reshape(WIDTH),
                          zeros((WIDTH,), F32), zeros((WIDTH,), F32)])
        n_dec, n_iclr = w_decay_up.shape[1], w_iclr_up.shape[1]
        wd = _pad_rows(w_decay_up[l], 0, LANES).astype(BF16)
        wa = _pad_rows(w_iclr_up[l], n_dec, LANES).astype(BF16)
        wg = w_gate_up[l].astype(BF16)
        wv = _pad_rows(w_vres_up[l - 1], 0, LANES).astype(BF16) if has_vres else None
        pre = _rwkv_pre(z_rkv.reshape(batch, seq, 3 * WIDTH), z_lora.reshape(batch, seq, LORA_COLS),
                        v_first, mu_rkv, mu_lora, vecs, wd, wa, wg, wv, batch, seq)
        if not has_vres:
            v_first = pre[5]
        lnx = jnp.concatenate([lnx_g[l][None], lnx_b[l][None], zeros((6, WIDTH), F32)])
        o_rwkv = _rwkv_scan(pre, lnx, batch, seq)

        ln = lambda j: jnp.concatenate([ln_g[l, j][None], ln_b[l, j][None], zeros((6, d), F32)])
        x1 = _outproj(x2d, o_att.reshape(n, WIDTH), o_rwkv.reshape(n, WIDTH), w_out[l].astype(BF16), ln(0))

        kv = _kvproj(mem2d, w_mem_kv[l].astype(BF16)).reshape(batch, mlen, 2 * d)
        x2, route = _memattn(x1.reshape(batch, seq, d), kv, w_mem_q[l].astype(BF16),
                             w_mem_o[l].astype(BF16), ln(1), w_router_pad, b_router_pad)
        x2d = _moe(x2.reshape(n, d), route.reshape(n, LANES), w_exp_gate, w_exp_up, w_exp_down, l, ln(2))
    return x2d.reshape(batch, seq, d)
```

```python
import functools

import jax
import jax.numpy as jnp
from jax import lax
from jax.experimental import pallas as pl
from jax.experimental.pallas import tpu as pltpu

F32 = jnp.float32
BF16 = jnp.bfloat16
I32 = jnp.int32

D_MODEL = 1024
DEPTH = 2
N_HEADS = 8
HEAD_DIM = 64
WIDTH = N_HEADS * HEAD_DIM
N_PAIRS = N_HEADS // 2
LANES = 128
MOBA_BLOCK = 256
MOBA_TOP = 3
GATE_ROWS = 16
POS_BLK = 16
POS_IN = 17
LORA_COLS = 384
IN_COLS_PAD = 3 * WIDTH + 3 * WIDTH + LORA_COLS
RWKV_GN_EPS = 64e-5
N_MEM_HEADS = 4
MEM_HEAD_DIM = D_MODEL // N_MEM_HEADS
N_EXPERTS = 32
N_GROUPS = 8
D_EXPERT = 512
LN_EPS = 1e-5
RMS_EPS = 1e-6
NEG = -1e30
ALPHA = (2 * DEPTH) ** 0.25
CHUNK = 64
MOE_ROWS = 256

VMEM_LIMIT = 48 * 1024 * 1024

NT_DIMS = (((1,), (1,)), ((), ()))
TN_DIMS = (((0,), (0,)), ((), ()))


def _cparams(sem):
    return pltpu.CompilerParams(dimension_semantics=sem, vmem_limit_bytes=VMEM_LIMIT)


def _split(a):
    hi = a.astype(BF16)
    lo = (a - hi.astype(F32)).astype(BF16)
    return hi, lo


def _dot1(a, b, dims=(((1,), (0,)), ((), ()))):
    return lax.dot_general(a.astype(BF16), b.astype(BF16), dims, preferred_element_type=F32)


def _layer_norm(y, g, b):
    mu = jnp.mean(y, -1, keepdims=True)
    d = y - mu
    var = jnp.mean(d * d, -1, keepdims=True)
    return d * lax.rsqrt(var + LN_EPS) * g + b


def _sigmoid(x):
    return 1.0 / (1.0 + jnp.exp(-x))


def _inproj_kernel(x_ref, w_ref, att_ref, rkv_ref, lora_ref):
    xb = x_ref[...].astype(BF16)
    att_ref[...] = jnp.dot(xb, w_ref[:, 0:3 * WIDTH], preferred_element_type=F32)
    rkv_ref[...] = jnp.dot(xb, w_ref[:, 3 * WIDTH:6 * WIDTH], preferred_element_type=F32)
    lora_ref[...] = jnp.dot(xb, w_ref[:, 6 * WIDTH:IN_COLS_PAD], preferred_element_type=F32)


def _inproj(x2d, w_pad, tm=256):
    n = x2d.shape[0]
    return pl.pallas_call(
        _inproj_kernel,
        grid=(n // tm,),
        in_specs=[pl.BlockSpec((tm, D_MODEL), lambda i: (i, 0)),
                  pl.BlockSpec((D_MODEL, IN_COLS_PAD), lambda i: (0, 0))],
        out_specs=[pl.BlockSpec((tm, 3 * WIDTH), lambda i: (i, 0)),
                   pl.BlockSpec((tm, 3 * WIDTH), lambda i: (i, 0)),
                   pl.BlockSpec((tm, LORA_COLS), lambda i: (i, 0))],
        out_shape=[jax.ShapeDtypeStruct((n, 3 * WIDTH), F32),
                   jax.ShapeDtypeStruct((n, 3 * WIDTH), F32),
                   jax.ShapeDtypeStruct((n, LORA_COLS), F32)],
        compiler_params=_cparams(("parallel",)),
        name="inproj",
    )(x2d, w_pad)


def _moba_kernel(slopes_ref, q_ref, k_ref, v_ref, gain_ref, o_ref, kaug_ref, vb_ref, kmp_ref, *, seq):
    pair = pl.program_id(1)
    i = pl.program_id(2)
    nb = seq // MOBA_BLOCK
    blk = MOBA_BLOCK

    @pl.when(i == 0)
    def _():
        k = k_ref[...]
        rowi = lax.broadcasted_iota(I32, (seq, LANES), 0)
        rowblk = lax.shift_right_logical(rowi, 8)
        rowin = rowi & (blk - 1)
        lane_s = lax.broadcasted_iota(I32, (seq, LANES), 1)
        kmean = jnp.sum(k.reshape(nb, blk, LANES), axis=1) * (1.0 / blk)
        for hh in range(2):
            slope = slopes_ref[2 * pair + hh]
            ob = HEAD_DIM * (1 - hh)
            inhead = (lane_s >= HEAD_DIM * hh) & (lane_s < HEAD_DIM * (hh + 1))
            c = lane_s - ob
            aug = jnp.where(c == rowblk, 1.0, 0.0)
            aug = jnp.where(c == POS_BLK, (slope * blk) * rowblk.astype(F32), aug)
            aug = jnp.where(c == POS_IN, slope * rowin.astype(F32), aug)
            kaug_ref[hh] = jnp.where(inhead, k, aug).astype(BF16)
            kmp_ref[hh] = jnp.zeros((LANES, LANES), F32)
            kmp_ref[hh, ob:ob + nb, :] = kmean
        vb_ref[...] = v_ref[...].astype(BF16)

    def query_block(ii):
        q = q_ref[...] * (HEAD_DIM ** -0.5)
        causal = (lax.broadcasted_iota(I32, (blk, blk), 0) >= lax.broadcasted_iota(I32, (blk, blk), 1))
        lane = lax.broadcasted_iota(I32, (blk, LANES), 1)
        gain = gain_ref[...]
        nk = (ii + 1) * blk
        inheads = [(lane >= HEAD_DIM * hh) & (lane < HEAD_DIM * (hh + 1)) for hh in range(2)]
        qas = []
        for hh in range(2):
            ob = HEAD_DIM * (1 - hh)
            inhead = inheads[hh]
            c = lane - ob
            qaug = jnp.where((c == POS_BLK) | (c == POS_IN), 1.0, 0.0)
            if ii > MOBA_TOP:
                qm = jnp.where(inhead, q, 0.0)
                gate = _dot1(kmp_ref[hh], qm, NT_DIMS)[ob:ob + GATE_ROWS, :]
                jrow = lax.broadcasted_iota(I32, (GATE_ROWS, blk), 0)
                cnt = jnp.zeros((GATE_ROWS, blk), I32)
                for jp in range(ii):
                    rowv = gate[jp:jp + 1, :]
                    beats = (rowv > gate) | ((rowv == gate) & (jp < jrow))
                    cnt = cnt + jnp.where(beats, 1, 0)
                drop = jnp.where((jrow < ii) & (cnt >= MOBA_TOP), NEG, 0.0).astype(BF16)
                place = (lax.broadcasted_iota(I32, (GATE_ROWS, LANES), 1) - ob
                         == lax.broadcasted_iota(I32, (GATE_ROWS, LANES), 0))
                qaug = qaug + lax.dot_general(drop, jnp.where(place, 1.0, 0.0).astype(BF16), TN_DIMS,
                                              preferred_element_type=F32)
            qas.append(jnp.where(inhead, q, qaug).astype(BF16))

        scores = [lax.dot_general(qas[hh], kaug_ref[hh, 0:nk, :], NT_DIMS, preferred_element_type=F32)
                  for hh in range(2)]
        probs, sums = [], []
        for s in scores:
            parts = [s[:, j * blk:(j + 1) * blk] for j in range(ii)]
            parts.append(jnp.where(causal, s[:, ii * blk:nk], NEG))
            m = jnp.max(functools.reduce(jnp.maximum, parts), -1, keepdims=True)
            ps = [jnp.exp(x - m) for x in parts]
            sums.append(jnp.sum(functools.reduce(lambda a, b: a + b, ps), -1, keepdims=True))
            probs.append(jnp.concatenate([x.astype(BF16) for x in ps], axis=1))
        outs = []
        for hh in range(2):
            out = jnp.dot(probs[hh], vb_ref[0:nk, :], preferred_element_type=F32) / sums[hh]
            ms = jnp.sum(jnp.where(inheads[hh], out * out, 0.0), -1, keepdims=True) * (1.0 / HEAD_DIM)
            outs.append(out * lax.rsqrt(ms + RMS_EPS) * gain)
        o_ref[...] = jnp.where(lane < HEAD_DIM, outs[0], outs[1])

    for ii in range(nb):
        pl.when(i == ii)(functools.partial(query_block, ii))


def _moba(z_att, slopes, gain, batch, seq):
    nb = seq // MOBA_BLOCK
    grid_spec = pltpu.PrefetchScalarGridSpec(
        num_scalar_prefetch=1,
        grid=(batch, N_PAIRS, nb),
        in_specs=[pl.BlockSpec((None, MOBA_BLOCK, LANES), lambda b, p, i, s: (b, i, p)),
                  pl.BlockSpec((None, seq, LANES), lambda b, p, i, s: (b, 0, N_PAIRS + p)),
                  pl.BlockSpec((None, seq, LANES), lambda b, p, i, s: (b, 0, 2 * N_PAIRS + p)),
                  pl.BlockSpec((1, LANES), lambda b, p, i, s: (0, p))],
        out_specs=pl.BlockSpec((None, MOBA_BLOCK, LANES), lambda b, p, i, s: (b, i, p)),
        scratch_shapes=[pltpu.VMEM((2, seq, LANES), BF16),
                        pltpu.VMEM((seq, LANES), BF16),
                        pltpu.VMEM((2, LANES, LANES), F32)],
    )
    return pl.pallas_call(
        functools.partial(_moba_kernel, seq=seq),
        grid_spec=grid_spec,
        out_shape=jax.ShapeDtypeStruct((batch, seq, WIDTH), F32),
        compiler_params=_cparams(("parallel", "parallel", "arbitrary")),
        name="moba",
    )(slopes, z_att, z_att, z_att, gain)


def _dot2x(a, b_exact):
    ah, al = _split(a)
    return (jnp.dot(ah, b_exact, preferred_element_type=F32)
            + jnp.dot(al, b_exact, preferred_element_type=F32))


def _head_block_diag(scale):
    r = lax.shift_right_logical(lax.broadcasted_iota(I32, (LANES, LANES), 0), 6)
    c = lax.shift_right_logical(lax.broadcasted_iota(I32, (LANES, LANES), 1), 6)
    return jnp.where(r == c, scale, 0.0).astype(BF16)


def _softplus(x):
    return jnp.maximum(x, 0.0) + jnp.log(1.0 + jnp.exp(-jnp.abs(x)))


def _rwkv_chunks(items):
    c_ = CHUNK
    tri = jnp.where(lax.broadcasted_iota(I32, (c_, c_), 0) >= lax.broadcasted_iota(I32, (c_, c_), 1),
                    1.0, 0.0).astype(BF16)
    lane = lax.broadcasted_iota(I32, (c_, LANES), 1)
    m0 = lane < HEAD_DIM
    t_idx = lax.broadcasted_iota(I32, (c_, LANES), 0)
    s_idx = lane & (HEAD_DIM - 1)
    strict = t_idx > s_idx
    eye2 = jnp.where(t_idx == s_idx, 1.0, 0.0)
    keep = jnp.concatenate([strict, t_idx >= s_idx], axis=0)

    def stack(x):
        return jnp.concatenate([jnp.where(m0, x, 0.0), jnp.where(m0, 0.0, x)], axis=0)

    splits = [_split(it[0]) for it in items]
    cs = [jnp.dot(tri, hi, preferred_element_type=F32) + jnp.dot(tri, lo, preferred_element_type=F32)
          for hi, lo in splits]
    base = []
    for (lw, r, k2, v, a, b), c in zip(items, cs):
        e_pos = jnp.exp(c)
        e_neg = jnp.exp(-c)
        at = a * jnp.exp(c - lw)
        bt = b * e_neg
        kt = k2 * e_neg
        rt = r * e_pos
        gc = e_pos[c_ - 1:c_, :]
        base.append(dict(at=at, rt=rt, gc=gc, bp=bt * gc, kp=kt * gc, vs=stack(v).astype(BF16),
                         ar=jnp.concatenate([at, rt], axis=0).astype(BF16),
                         sb=stack(bt).astype(BF16), sk=stack(kt).astype(BF16)))
    gb = [jnp.where(keep, lax.dot_general(d['ar'], d['sb'], NT_DIMS, preferred_element_type=F32), 0.0)
          for d in base]
    gk = [jnp.where(keep, lax.dot_general(d['ar'], d['sk'], NT_DIMS, preferred_element_type=F32), 0.0)
          for d in base]
    ls = [g[0:c_] for g in gb]
    ts = [eye2 + l for l in ls]
    lps = [_dot1(l, stack(l)) for l in ls]
    for _ in range(4):
        xs = [_dot1(jnp.concatenate([t, lp], axis=0), stack(lp)) for t, lp in zip(ts, lps)]
        ts = [t + x[0:c_] for t, x in zip(ts, xs)]
        lps = [x[c_:2 * c_] for x in xs]
    ts = [t + _dot1(t, stack(lp)) for t, lp in zip(ts, lps)]
    outs = []
    for d, g_b, g_k, t in zip(base, gb, gk, ts):
        pyk = jnp.dot(g_k.astype(BF16), d['vs'], preferred_element_type=F32)
        grow = jnp.broadcast_to(d['gc'], (c_, LANES))
        outs.append((d['at'].astype(BF16), pyk[0:c_], t.astype(BF16), d['bp'].astype(BF16),
                     d['kp'].astype(BF16), grow, d['rt'].astype(BF16), g_b[c_:2 * c_].astype(BF16),
                     pyk[c_:2 * c_]))
    return outs


def _rwkv_pre_kernel(*refs, has_vres, tb):
    if has_vres:
        (rkv_ref, lora_ref, vf_ref, mu_rkv_ref, mu_lora_ref, vec_ref, wd_ref, wa_ref, wg_ref, wv_ref,
         at_ref, p_ref, tc_ref, bp_ref, kp_ref, v_ref, gr_ref, rt_ref, qc_ref, yk_ref, bonus_ref, g_ref,
         carry_ref, carryl_ref) = refs
    else:
        (rkv_ref, lora_ref, mu_rkv_ref, mu_lora_ref, vec_ref, wd_ref, wa_ref, wg_ref,
         at_ref, p_ref, tc_ref, bp_ref, kp_ref, v_ref, gr_ref, rt_ref, qc_ref, yk_ref, bonus_ref, g_ref,
         carry_ref, carryl_ref) = refs
    t_idx = pl.program_id(1)

    @pl.when(t_idx == 0)
    def _():
        carry_ref[...] = jnp.zeros_like(carry_ref)
        carryl_ref[...] = jnp.zeros_like(carryl_ref)

    def token_shift(z, cref, mu):
        first = lax.broadcasted_iota(I32, z.shape, 0) == 0
        prev = jnp.where(first, cref[0:1, :], pltpu.roll(z, 1, 0))
        cref[0:1, :] = z[tb - 1:tb, :]
        return z + (prev - z) * mu

    zs = token_shift(rkv_ref[...], carry_ref, mu_rkv_ref[...])
    zl = token_shift(lora_ref[...], carryl_ref, mu_lora_ref[...])
    l0 = zl[:, 0:LANES]
    w0, a0, v0 = vec_ref[0:1, :], vec_ref[1:2, :], vec_ref[2:3, :]
    k_k, k_a, r_k = vec_ref[3:4, :], vec_ref[4:5, :], vec_ref[5:6, :]
    dw = _dot1(jnp.tanh(l0), wd_ref[...])
    lw_all = -jnp.exp(-_softplus(-(w0 + dw)) - 0.5)
    a_lr = _sigmoid(a0 + _dot1(l0, wa_ref[...]))
    g_ref[...] = _dot1(_sigmoid(zl[:, LANES:2 * LANES]), wg_ref[...])
    r_all = zs[:, 0:WIDTH]
    k_all = zs[:, WIDTH:2 * WIDTH]
    v_all = zs[:, 2 * WIDTH:3 * WIDTH]
    if has_vres:
        mix = _sigmoid(v0 + _dot1(zl[:, 2 * LANES:3 * LANES], wv_ref[...]))
        v_all = v_all + (vf_ref[...] - v_all) * mix
    v_ref[...] = v_all
    bd = _head_block_diag(1.0)
    items, where = [], []
    for pp in range(N_PAIRS):
        sl = slice(pp * LANES, (pp + 1) * LANES)
        r, k, v = r_all[:, sl], k_all[:, sl], v_all[:, sl]
        alr = a_lr[:, sl]
        kk = k * k_k[:, sl]
        ss = _dot2x(kk * kk, bd)
        kk = kk * lax.rsqrt(jnp.maximum(ss, 1e-24))
        k2 = k * (1.0 + (alr - 1.0) * k_a[:, sl])
        bonus_ref[:, sl] = _dot2x(r * k2 * r_k[:, sl], bd) * v
        a = -kk
        b = kk * alr
        lw = lw_all[:, sl]
        for ci in range(tb // CHUNK):
            rs = slice(ci * CHUNK, (ci + 1) * CHUNK)
            items.append((lw[rs], r[rs], k2[rs], v[rs], a[rs], b[rs]))
            where.append((rs, sl))
    for (rs, sl), outs in zip(where, _rwkv_chunks(items)):
        for ref, val in zip((at_ref, p_ref, tc_ref, bp_ref, kp_ref, gr_ref, rt_ref, qc_ref, yk_ref), outs):
            ref[rs, sl] = val


def _rwkv_pre(z_rkv, z_lora, v_first, mu_rkv, mu_lora, vecs, wd, wa, wg, wv, batch, seq, tb=128):
    has_vres = v_first is not None
    big = lambda w: pl.BlockSpec((None, tb, w), lambda b, t: (b, t, 0))
    full = lambda shape: pl.BlockSpec(shape, lambda b, t: (0,) * len(shape))
    in_specs = [big(3 * WIDTH), big(LORA_COLS)] + ([big(WIDTH)] if has_vres else [])
    in_specs += [full((1, 3 * WIDTH)), full((1, LORA_COLS)), full((8, WIDTH)),
                 full((LANES, WIDTH)), full((LANES, WIDTH)), full((LANES, WIDTH))]
    args = [z_rkv, z_lora] + ([v_first] if has_vres else []) + [mu_rkv, mu_lora, vecs, wd, wa, wg]
    if has_vres:
        in_specs.append(full((LANES, WIDTH)))
        args.append(wv)
    out_dtypes = [BF16, F32, BF16, BF16, BF16, F32, F32, BF16, BF16, F32, F32, F32]
    return pl.pallas_call(
        functools.partial(_rwkv_pre_kernel, has_vres=has_vres, tb=tb),
        grid=(batch, seq // tb),
        in_specs=in_specs,
        out_specs=[big(WIDTH)] * len(out_dtypes),
        out_shape=[jax.ShapeDtypeStruct((batch, seq, WIDTH), dt) for dt in out_dtypes],
        scratch_shapes=[pltpu.VMEM((8, 3 * WIDTH), F32), pltpu.VMEM((8, LORA_COLS), F32)],
        compiler_params=_cparams(("parallel", "arbitrary")),
        name="rwkv_pre",
    )(*args)


def _rwkv_scan_kernel(at_ref, p_ref, tc_ref, bp_ref, kp_ref, v_ref, gr_ref, rt_ref, qc_ref, yk_ref,
                      bonus_ref, g_ref, lnx_ref, o_ref, st_ref, *, ts):
    t_idx = pl.program_id(1)

    @pl.when(t_idx == 0)
    def _():
        st_ref[...] = jnp.zeros_like(st_ref)

    lane = lax.broadcasted_iota(I32, (CHUNK, LANES), 1)
    m0 = lane < HEAD_DIM
    bdmask = (lax.shift_right_logical(lax.broadcasted_iota(I32, (LANES, LANES), 0), 6)
              == lax.shift_right_logical(lax.broadcasted_iota(I32, (LANES, LANES), 1), 6))
    ones_t = jnp.full((CHUNK, LANES), 1.0 / CHUNK, BF16)

    def stack(x):
        return jnp.concatenate([jnp.where(m0, x, 0.0), jnp.where(m0, 0.0, x)], axis=0)

    tn = functools.partial(lax.dot_general, dimension_numbers=TN_DIMS, preferred_element_type=F32)
    nseq = at_ref.shape[0]
    chains = [(bb, pp, slice(pp * LANES, (pp + 1) * LANES)) for bb in range(nseq) for pp in range(N_PAIRS)]
    chunks = [slice(ci * CHUNK, (ci + 1) * CHUNK) for ci in range(ts // CHUNK)]

    gcols, kvs = {}, {}
    for ci, rs in enumerate(chunks):
        for bb, pp, sl in chains:
            gh, gl = _split(gr_ref[bb, rs, sl])
            gcols[ci, bb, pp] = tn(gh, ones_t) + tn(gl, ones_t)
            kvs[ci, bb, pp] = jnp.where(bdmask, _dot1(kp_ref[bb, rs, sl], v_ref[bb, rs, sl], TN_DIMS), 0.0)

    states = [st_ref[bb, pp] for bb, pp, _ in chains]
    for ci, rs in enumerate(chunks):
        sts = [st.astype(BF16) for st in states]
        xs = [_dot1(at_ref[bb, rs, sl], sts[n]) + p_ref[bb, rs, sl] for n, (bb, pp, sl) in enumerate(chains)]
        us = [_dot1(tc_ref[bb, rs, sl], stack(xs[n])).astype(BF16) for n, (bb, pp, sl) in enumerate(chains)]
        new = [gcols[ci, bb, pp] * states[n]
               + (jnp.where(bdmask, _dot1(bp_ref[bb, rs, sl], us[n], TN_DIMS), 0.0) + kvs[ci, bb, pp])
               for n, (bb, pp, sl) in enumerate(chains)]
        for n, (bb, pp, sl) in enumerate(chains):
            lhs = jnp.concatenate([rt_ref[bb, rs, sl], qc_ref[bb, rs, sl]], axis=1)
            rhs = jnp.concatenate([sts[n], stack(us[n])], axis=0)
            o_ref[bb, rs, sl] = jnp.dot(lhs, rhs, preferred_element_type=F32) + yk_ref[bb, rs, sl]
        states = new
    for n, (bb, pp, _) in enumerate(chains):
        st_ref[bb, pp] = states[n]

    bd = _head_block_diag(1.0 / HEAD_DIM)
    for bb, pp, sl in chains:
        y = o_ref[bb, :, sl]
        mu = _dot2x(y, bd)
        d = y - mu
        var = _dot2x(d * d, bd)
        yn = d * lax.rsqrt(var + RWKV_GN_EPS) * lnx_ref[0:1, sl] + lnx_ref[1:2, sl]
        o_ref[bb, :, sl] = (yn + bonus_ref[bb, :, sl]) * g_ref[bb, :, sl]


def _rwkv_scan(pre, lnx, batch, seq, ts=128):
    nseq = next(c for c in (4, 2, 1) if batch % c == 0)
    big = pl.BlockSpec((nseq, ts, WIDTH), lambda b, t: (b, t, 0))
    return pl.pallas_call(
        functools.partial(_rwkv_scan_kernel, ts=ts),
        grid=(batch // nseq, seq // ts),
        in_specs=[big] * 12 + [pl.BlockSpec((8, WIDTH), lambda b, t: (0, 0))],
        out_specs=big,
        out_shape=jax.ShapeDtypeStruct((batch, seq, WIDTH), F32),
        scratch_shapes=[pltpu.VMEM((nseq, N_PAIRS, LANES, LANES), F32)],
        compiler_params=_cparams(("parallel", "arbitrary")),
        name="rwkv_scan",
    )(*pre, lnx)


def _outproj_kernel(x_ref, oa_ref, orw_ref, w_ref, ln_ref, o_ref):
    y = (jnp.dot(oa_ref[...].astype(BF16), w_ref[0:WIDTH, :], preferred_element_type=F32)
         + jnp.dot(orw_ref[...].astype(BF16), w_ref[WIDTH:2 * WIDTH, :], preferred_element_type=F32))
    o_ref[...] = _layer_norm(ALPHA * x_ref[...] + y, ln_ref[0:1, :], ln_ref[1:2, :])


def _outproj(x2d, o_att, o_rwkv, w_out, ln, tm=512):
    n = x2d.shape[0]
    return pl.pallas_call(
        _outproj_kernel,
        grid=(n // tm,),
        in_specs=[pl.BlockSpec((tm, D_MODEL), lambda i: (i, 0)),
                  pl.BlockSpec((tm, WIDTH), lambda i: (i, 0)),
                  pl.BlockSpec((tm, WIDTH), lambda i: (i, 0)),
                  pl.BlockSpec((2 * WIDTH, D_MODEL), lambda i: (0, 0)),
                  pl.BlockSpec((8, D_MODEL), lambda i: (0, 0))],
        out_specs=pl.BlockSpec((tm, D_MODEL), lambda i: (i, 0)),
        out_shape=jax.ShapeDtypeStruct((n, D_MODEL), F32),
        compiler_params=_cparams(("parallel",)),
        name="outproj_ln",
    )(x2d, o_att, o_rwkv, w_out, ln)


def _kvproj_kernel(m_ref, w_ref, o_ref):
    o_ref[...] = jnp.dot(m_ref[...].astype(BF16), w_ref[...], preferred_element_type=F32).astype(BF16)


def _kvproj(mem2d, w_kv, tm=256):
    n = mem2d.shape[0]
    return pl.pallas_call(
        _kvproj_kernel,
        grid=(n // tm,),
        in_specs=[pl.BlockSpec((tm, D_MODEL), lambda i: (i, 0)),
                  pl.BlockSpec((D_MODEL, 2 * D_MODEL), lambda i: (0, 0))],
        out_specs=pl.BlockSpec((tm, 2 * D_MODEL), lambda i: (i, 0)),
        out_shape=jax.ShapeDtypeStruct((n, 2 * D_MODEL), BF16),
        compiler_params=_cparams(("parallel",)),
        name="kvproj",
    )(mem2d, w_kv)


def _route_lanes(logits, b_router):
    lane = lax.broadcasted_iota(I32, logits.shape, 1)
    lane_f = lane.astype(F32)
    valid = lane < N_EXPERTS
    s = _sigmoid(logits)
    sel = jnp.where(valid, s + b_router, NEG)

    def partner(x, bit):
        return jnp.where((lane & bit) == 0, pltpu.roll(x, LANES - bit, 1), pltpu.roll(x, bit, 1))

    p1 = partner(sel, 1)
    hi1, lo1 = jnp.maximum(sel, p1), jnp.minimum(sel, p1)
    hi2, lo2 = partner(hi1, 2), partner(lo1, 2)
    gsum = jnp.maximum(hi1, hi2) + jnp.maximum(jnp.minimum(hi1, hi2), jnp.maximum(lo1, lo2))
    gmax = jnp.max(gsum, -1, keepdims=True)
    grp_f = lax.shift_right_logical(lane, 2).astype(F32)
    g_best = jnp.min(jnp.where((gsum == gmax) & valid, grp_f, 1e9), -1, keepdims=True)
    in_grp = (grp_f == g_best) & valid
    masked = jnp.where(in_grp, sel, NEG)
    v1 = jnp.max(masked, -1, keepdims=True)
    e1 = jnp.min(jnp.where(in_grp & (masked == v1), lane_f, 1e9), -1, keepdims=True)
    rest = in_grp & (lane_f != e1)
    masked2 = jnp.where(rest, sel, NEG)
    v2 = jnp.max(masked2, -1, keepdims=True)
    e2 = jnp.min(jnp.where(rest & (masked2 == v2), lane_f, 1e9), -1, keepdims=True)
    w1 = jnp.sum(jnp.where(lane_f == e1, s, 0.0), -1, keepdims=True)
    w2 = jnp.sum(jnp.where(lane_f == e2, s, 0.0), -1, keepdims=True)
    tot = w1 + w2
    out = jnp.where(lane == 0, e1, jnp.where(lane == 1, e2, jnp.where(lane == 2, w1 / tot, w2 / tot)))
    return jnp.where(lane < 4, out, 0.0)


def _memattn_kernel(x_ref, kv_ref, wq_ref, wo_ref, ln_ref, wr_ref, br_ref, o_ref, route_ref):
    x = x_ref[...]
    q = jnp.dot(x.astype(BF16), wq_ref[...], preferred_element_type=F32) * (MEM_HEAD_DIM ** -0.5)
    heads = []
    for h in range(N_MEM_HEADS):
        sl = slice(h * MEM_HEAD_DIM, (h + 1) * MEM_HEAD_DIM)
        kh = kv_ref[:, sl]
        vh = kv_ref[:, D_MODEL + h * MEM_HEAD_DIM:D_MODEL + (h + 1) * MEM_HEAD_DIM]
        s = lax.dot_general(q[:, sl].astype(BF16), kh, NT_DIMS, preferred_element_type=F32)
        p = jnp.exp(s - jnp.max(s, -1, keepdims=True))
        o = jnp.dot(p.astype(BF16), vh, preferred_element_type=F32)
        heads.append(o / jnp.sum(p, -1, keepdims=True))
    o = jnp.concatenate(heads, axis=-1).astype(BF16)
    y = jnp.dot(o, wo_ref[...], preferred_element_type=F32)
    x2 = _layer_norm(ALPHA * x + y, ln_ref[0:1, :], ln_ref[1:2, :])
    o_ref[...] = x2
    logits = _dot1(x2, wr_ref[...])
    route_ref[...] = _route_lanes(logits, br_ref[0:1, :])


def _memattn(x3d, kv3d, wq, wo, ln, w_router_pad, b_router_pad, tm=512):
    batch, seq, _ = x3d.shape
    mlen = kv3d.shape[1]
    full = lambda shape: pl.BlockSpec(shape, lambda b, t: (0,) * len(shape))
    return pl.pallas_call(
        _memattn_kernel,
        grid=(batch, seq // tm),
        in_specs=[pl.BlockSpec((None, tm, D_MODEL), lambda b, t: (b, t, 0)),
                  pl.BlockSpec((None, mlen, 2 * D_MODEL), lambda b, t: (b, 0, 0)),
                  full((D_MODEL, D_MODEL)), full((D_MODEL, D_MODEL)), full((8, D_MODEL)),
                  full((D_MODEL, LANES)), full((8, LANES))],
        out_specs=[pl.BlockSpec((None, tm, D_MODEL), lambda b, t: (b, t, 0)),
                   pl.BlockSpec((None, tm, LANES), lambda b, t: (b, t, 0))],
        out_shape=[jax.ShapeDtypeStruct((batch, seq, D_MODEL), F32),
                   jax.ShapeDtypeStruct((batch, seq, LANES), F32)],
        compiler_params=_cparams(("parallel", "parallel")),
        name="memattn_ln",
    )(x3d, kv3d, wq, wo, ln, w_router_pad, b_router_pad)


def _plan_kernel(route_ref, dest_ref, stat_ref, acc_ref):
    phase = pl.program_id(0)
    i = pl.program_id(1)
    tm = route_ref.shape[0]
    lane = lax.broadcasted_iota(I32, (tm, LANES), 1).astype(F32)
    e0 = route_ref[:, 0:1]
    e1 = route_ref[:, 1:2]
    hot = jnp.where((lane == e0) | (lane == e1), 1.0, 0.0)

    @pl.when((phase == 0) & (i == 0))
    def _():
        acc_ref[...] = jnp.zeros_like(acc_ref)

    @pl.when(phase == 0)
    def _():
        acc_ref[0:1, :] = acc_ref[0:1, :] + jnp.sum(hot, axis=0, keepdims=True)

    @pl.when((phase == 1) & (i == 0))
    def _():
        r = lax.broadcasted_iota(I32, (LANES, LANES), 0)
        c = lax.broadcasted_iota(I32, (LANES, LANES), 1)
        before = jnp.where(r < c, 1.0, 0.0)
        cnt8 = jnp.broadcast_to(acc_ref[0:1, :], (8, LANES))
        acc_ref[1:2, :] = _dot2x(cnt8, before.astype(BF16))[0:1, :]

    @pl.when(phase == 1)
    def _():
        r = lax.broadcasted_iota(I32, (tm, tm), 0)
        c = lax.broadcasted_iota(I32, (tm, tm), 1)
        earlier = jnp.where(c < r, 1.0, 0.0).astype(BF16)
        pos = (jnp.dot(earlier, hot.astype(BF16), preferred_element_type=F32)
               + acc_ref[1:2, :] + acc_ref[2:3, :])
        d0 = jnp.sum(jnp.where(lane == e0, pos, 0.0), -1, keepdims=True)
        d1 = jnp.sum(jnp.where(lane == e1, pos, 0.0), -1, keepdims=True)
        dest_ref[...] = jnp.where(lane == 0.0, d0, jnp.where(lane == 1.0, d1, 0.0)).astype(I32)
        acc_ref[2:3, :] = acc_ref[2:3, :] + jnp.sum(hot, axis=0, keepdims=True)
        stat_ref[...] = acc_ref[...]


def _plan(route, tm=1024):
    n = route.shape[0]
    tm = min(tm, n)
    return pl.pallas_call(
        _plan_kernel,
        grid=(2, n // tm),
        in_specs=[pl.BlockSpec((tm, LANES), lambda p, i: (i, 0))],
        out_specs=[pl.BlockSpec((tm, LANES), lambda p, i: (i * p, 0)),
                   pl.BlockSpec((8, LANES), lambda p, i: (0, 0))],
        out_shape=[jax.ShapeDtypeStruct((n, LANES), I32), jax.ShapeDtypeStruct((8, LANES), F32)],
        scratch_shapes=[pltpu.VMEM((8, LANES), F32)],
        compiler_params=_cparams(("arbitrary", "arbitrary")),
        name="moe_plan",
    )(route)


def _dispatch_kernel(dest_ref, prev_ref, x_ref, xs_hbm, stage_ref, sems):
    i = pl.program_id(0)
    tm = dest_ref.shape[2] // 2
    stage_ref[i % 2] = x_ref[...]

    def copy(step, dref, t, slot):
        src = stage_ref.at[step % 2, pl.ds(t, 1)]
        return pltpu.make_async_copy(src, xs_hbm.at[pl.ds(dref[0, 0, slot * tm + t], 1)], sems.at[step % 2])

    def start(t, c):
        copy(i, dest_ref, t, 0).start()
        copy(i, dest_ref, t, 1).start()
        return c

    def wait_step(step, dref):
        def wait(t, c):
            copy(step, dref, t, 0).wait()
            copy(step, dref, t, 1).wait()
            return c
        lax.fori_loop(0, tm, wait, 0, unroll=8)

    lax.fori_loop(0, tm, start, 0, unroll=8)

    @pl.when(i > 0)
    def _():
        wait_step(i - 1, prev_ref)

    @pl.when(i == pl.num_programs(0) - 1)
    def _():
        wait_step(i, dest_ref)


def _dispatch(x2d, dest3, n_rows):
    n_steps = dest3.shape[0]
    tm = dest3.shape[2] // 2
    blk = (1, 1, 2 * tm)
    return pl.pallas_call(
        _dispatch_kernel,
        grid=(n_steps,),
        in_specs=[pl.BlockSpec(blk, lambda i: (i, 0, 0), memory_space=pltpu.SMEM),
                  pl.BlockSpec(blk, lambda i: (jnp.maximum(i - 1, 0), 0, 0), memory_space=pltpu.SMEM),
                  pl.BlockSpec((tm, D_MODEL), lambda i: (i, 0))],
        out_specs=pl.BlockSpec(memory_space=pl.ANY),
        out_shape=jax.ShapeDtypeStruct((n_rows, D_MODEL), F32),
        scratch_shapes=[pltpu.VMEM((2, tm, D_MODEL), F32), pltpu.SemaphoreType.DMA((2,))],
        compiler_params=_cparams(("arbitrary",)),
        name="moe_dispatch",
    )(dest3, dest3, x2d)


def _experts_kernel(rb_ref, ex_ref, nit_ref, off_ref, cnt_ref, x_ref, wg_ref, wu_ref, wd_ref, o_ref,
                    wgb_ref, wub_ref, wdb_ref):
    w = pl.program_id(0)
    rows = x_ref.shape[0]

    @pl.when(w < nit_ref[0])
    def _():
        e = ex_ref[w]
        rb = rb_ref[w]
        prev = jnp.maximum(w - 1, 0)

        @pl.when((w == 0) | (ex_ref[prev] != e))
        def _():
            wgb_ref[...] = wg_ref[...].astype(BF16)
            wub_ref[...] = wu_ref[...].astype(BF16)
            wdb_ref[...] = wd_ref[...].astype(BF16)

        xb = x_ref[...].astype(BF16)
        hg = jnp.dot(xb, wgb_ref[...], preferred_element_type=F32)
        hu = jnp.dot(xb, wub_ref[...], preferred_element_type=F32)
        h = hg * _sigmoid(hg) * hu
        y = jnp.dot(h.astype(BF16), wdb_ref[...], preferred_element_type=F32)
        row = rb * rows + lax.broadcasted_iota(I32, (rows, 1), 0)
        mine = (row >= off_ref[e]) & (row < off_ref[e] + cnt_ref[e])
        first = (w == 0) | (rb_ref[prev] != rb)

        @pl.when(first)
        def _():
            o_ref[...] = jnp.where(mine, y, 0.0)

        @pl.when(jnp.logical_not(first))
        def _():
            o_ref[...] = jnp.where(mine, y, o_ref[...])


def _experts(xs, rb, ex, n_items, off, cnt, w_gate, w_up, w_down, layer, rows):
    n_work = rb.shape[0]
    item = lambda w, nit: jnp.minimum(w, nit[0] - 1)
    row_map = lambda w, rb, ex, nit, off, cnt: (rb[item(w, nit)], 0)
    w_map = lambda w, rb, ex, nit, off, cnt: (layer, ex[item(w, nit)], 0, 0)
    grid_spec = pltpu.PrefetchScalarGridSpec(
        num_scalar_prefetch=5,
        grid=(n_work,),
        in_specs=[pl.BlockSpec((rows, D_MODEL), row_map),
                  pl.BlockSpec((None, None, D_MODEL, D_EXPERT), w_map),
                  pl.BlockSpec((None, None, D_MODEL, D_EXPERT), w_map),
                  pl.BlockSpec((None, None, D_EXPERT, D_MODEL), w_map)],
        out_specs=pl.BlockSpec((rows, D_MODEL), row_map),
        scratch_shapes=[pltpu.VMEM((D_MODEL, D_EXPERT), BF16), pltpu.VMEM((D_MODEL, D_EXPERT), BF16),
                        pltpu.VMEM((D_EXPERT, D_MODEL), BF16)],
    )
    return pl.pallas_call(
        _experts_kernel,
        grid_spec=grid_spec,
        out_shape=jax.ShapeDtypeStruct(xs.shape, F32),
        compiler_params=_cparams(("arbitrary",)),
        name="moe_experts",
    )(rb, ex, n_items, off, cnt, xs, w_gate, w_up, w_down)


def _combine_kernel(dest_ref, next_ref, x_ref, gate_ref, ln_ref, y_hbm, o_ref, buf_ref, sems):
    i = pl.program_id(0)
    tm = x_ref.shape[0]

    def copy(step, dref, slot, t):
        src = y_hbm.at[pl.ds(dref[0, 0, slot * tm + t], 1)]
        return pltpu.make_async_copy(src, buf_ref.at[step % 2, slot, pl.ds(t, 1)], sems.at[step % 2])

    def start_step(step, dref):
        def start(t, c):
            copy(step, dref, 0, t).start()
            copy(step, dref, 1, t).start()
            return c
        lax.fori_loop(0, tm, start, 0, unroll=8)

    @pl.when(i == 0)
    def _():
        start_step(0, dest_ref)

    @pl.when(i + 1 < pl.num_programs(0))
    def _():
        start_step(i + 1, next_ref)

    def wait(t, c):
        copy(i, dest_ref, 0, t).wait()
        copy(i, dest_ref, 1, t).wait()
        return c

    lax.fori_loop(0, tm, wait, 0, unroll=8)
    cur = i % 2
    y = gate_ref[:, 0:1] * buf_ref[cur, 0] + gate_ref[:, 1:2] * buf_ref[cur, 1]
    o_ref[...] = _layer_norm(ALPHA * x_ref[...] + y, ln_ref[0:1, :], ln_ref[1:2, :])


def _combine(x2d, gates, dest3, ys, ln, tm=256):
    n = x2d.shape[0]
    n_steps = n // tm
    blk = (1, 1, 2 * tm)
    return pl.pallas_call(
        _combine_kernel,
        grid=(n_steps,),
        in_specs=[pl.BlockSpec(blk, lambda i: (i, 0, 0), memory_space=pltpu.SMEM),
                  pl.BlockSpec(blk, lambda i: (jnp.minimum(i + 1, n_steps - 1), 0, 0), memory_space=pltpu.SMEM),
                  pl.BlockSpec((tm, D_MODEL), lambda i: (i, 0)),
                  pl.BlockSpec((tm, 2), lambda i: (i, 0)),
                  pl.BlockSpec((8, D_MODEL), lambda i: (0, 0)),
                  pl.BlockSpec(memory_space=pl.ANY)],
        out_specs=pl.BlockSpec((tm, D_MODEL), lambda i: (i, 0)),
        out_shape=jax.ShapeDtypeStruct((n, D_MODEL), F32),
        scratch_shapes=[pltpu.VMEM((2, 2, tm, D_MODEL), F32), pltpu.SemaphoreType.DMA((2,))],
        compiler_params=_cparams(("arbitrary",)),
        name="moe_combine_ln",
    )(dest3, dest3, x2d, gates, ln, ys)


def _work_items(counts, off, n_rows, rows):
    n_work = n_rows // rows + N_EXPERTS
    first = off // rows
    last = (off + counts - 1) // rows
    per_expert = jnp.where(counts > 0, last - first + 1, 0)
    item_end = jnp.cumsum(per_expert)
    item_start = item_end - per_expert
    w = jnp.arange(n_work, dtype=I32)
    ex = jnp.minimum(jnp.sum((item_end[None, :] <= w[:, None]).astype(I32), axis=1), N_EXPERTS - 1)
    rb = jnp.clip(first[ex] + (w - item_start[ex]), 0, n_rows // rows - 1).astype(I32)
    return rb, ex, item_end[-1:].astype(I32)


def _moe(x2, route, w_gate, w_up, w_down, layer, ln, rows=MOE_ROWS, tm=256):
    n = x2.shape[0]
    n_rows = 2 * n
    gates = route[:, 2:4]
    dest_tile, stat = _plan(route)
    counts = stat[0, :N_EXPERTS].astype(I32)
    off = stat[1, :N_EXPERTS].astype(I32)
    rb, ex, n_items = _work_items(counts, off, n_rows, rows)
    dest3 = dest_tile[:, 0:2].reshape(n // tm, tm, 2).transpose(0, 2, 1).reshape(n // tm, 1, 2 * tm)
    xs = _dispatch(x2, dest3, n_rows)
    ys = _experts(xs, rb, ex, n_items, off, counts, w_gate, w_up, w_down, layer, rows)
    return _combine(x2, gates, dest3, ys, ln, tm)


def _pad_rows(w, rows_before, total):
    return jnp.pad(w, ((rows_before, total - rows_before - w.shape[0]), (0, 0)))


def kernel(x, mem, w_in, w_in_vres, mu_shift, mu_vres, w0, w_decay_up, a0, w_iclr_up, v0, w_vres_up,
           w_gate_up, k_k, k_a, r_k, lnx_g, lnx_b, attn_out_g, w_out, w_mem_q, w_mem_kv, w_mem_o,
           w_router, b_router, w_exp_gate, w_exp_up, w_exp_down, ln_g, ln_b):
    batch, seq, d = x.shape
    n = batch * seq
    mlen = mem.shape[1]
    zeros = jnp.zeros
    slopes = 2.0 ** (-8.0 * jnp.arange(1, N_HEADS + 1, dtype=F32) / N_HEADS)
    w_router_pad = jnp.pad(w_router, ((0, 0), (0, LANES - N_EXPERTS)))
    b_router_pad = jnp.pad(b_router.reshape(1, N_EXPERTS), ((0, 7), (0, LANES - N_EXPERTS)))
    mem2d = mem.reshape(batch * mlen, d)
    n_shift = w_in.shape[2] - 3 * WIDTH
    n_lora = n_shift - 3 * WIDTH
    n_vres = w_in_vres.shape[2]
    x2d = x.reshape(n, d)
    v_first = None
    for l in range(DEPTH):
        has_vres = l > 0
        w_extra = w_in_vres[l - 1] if has_vres else zeros((d, n_vres), F32)
        w_pad = jnp.concatenate(
            [w_in[l], w_extra, zeros((d, IN_COLS_PAD - w_in.shape[2] - n_vres), F32)], axis=1).astype(BF16)
        z_att, z_rkv, z_lora = _inproj(x2d, w_pad)
        o_att = _moba(z_att.reshape(batch, seq, 3 * WIDTH), slopes, attn_out_g[l].reshape(1, WIDTH), batch, seq)

        mu_rkv = mu_shift[l, :3 * WIDTH].reshape(1, 3 * WIDTH)
        mu_extra = mu_vres[l - 1] if has_vres else zeros((n_vres,), F32)
        mu_lora = jnp.concatenate(
            [mu_shift[l, 3 * WIDTH:], mu_extra, zeros((LORA_COLS - n_lora - n_vres,), F32)]).reshape(1, LORA_COLS)
        v0_l = v0[l - 1] if has_vres else zeros((WIDTH,), F32)
        vecs = jnp.stack([w0[l], a0[l], v0_l, k_k[l], k_a[l], r_k[l].reshape(WIDTH),
                          zeros((WIDTH,), F32), zeros((WIDTH,), F32)])
        n_dec, n_iclr = w_decay_up.shape[1], w_iclr_up.shape[1]
        wd = _pad_rows(w_decay_up[l], 0, LANES).astype(BF16)
        wa = _pad_rows(w_iclr_up[l], n_dec, LANES).astype(BF16)
        wg = w_gate_up[l].astype(BF16)
        wv = _pad_rows(w_vres_up[l - 1], 0, LANES).astype(BF16) if has_vres else None
        pre = _rwkv_pre(z_rkv.reshape(batch, seq, 3 * WIDTH), z_lora.reshape(batch, seq, LORA_COLS),
                        v_first, mu_rkv, mu_lora, vecs, wd, wa, wg, wv, batch, seq)
        if not has_vres:
            v_first = pre[5]
        lnx = jnp.concatenate([lnx_g[l][None], lnx_b[l][None], zeros((6, WIDTH), F32)])
        o_rwkv = _rwkv_scan(pre, lnx, batch, seq)

        ln = lambda j: jnp.concatenate([ln_g[l, j][None], ln_b[l, j][None], zeros((6, d), F32)])
        x1 = _outproj(x2d, o_att.reshape(n, WIDTH), o_rwkv.reshape(n, WIDTH), w_out[l].astype(BF16), ln(0))

        kv = _kvproj(mem2d, w_mem_kv[l].astype(BF16)).reshape(batch, mlen, 2 * d)
        x2, route = _memattn(x1.reshape(batch, seq, d), kv, w_mem_q[l].astype(BF16),
                             w_mem_o[l].astype(BF16), ln(1), w_router_pad, b_router_pad)
        x2d = _moe(x2.reshape(n, d), route.reshape(n, LANES), w_exp_gate, w_exp_up, w_exp_down, l, ln(2))
    return x2d.reshape(batch, seq, d)
```

```python
import functools

import jax
import jax.numpy as jnp
from jax import lax
from jax.experimental import pallas as pl
from jax.experimental.pallas import tpu as pltpu

F32 = jnp.float32
BF16 = jnp.bfloat16
I32 = jnp.int32

D_MODEL = 1024
DEPTH = 2
N_HEADS = 8
HEAD_DIM = 64
WIDTH = N_HEADS * HEAD_DIM
N_PAIRS = N_HEADS // 2
LANES = 128
MOBA_BLOCK = 256
MOBA_TOP = 3
GATE_ROWS = 16
POS_BLK = 16
POS_IN = 17
LORA_COLS = 384
IN_COLS_PAD = 3 * WIDTH + 3 * WIDTH + LORA_COLS
RWKV_GN_EPS = 64e-5
N_MEM_HEADS = 4
MEM_HEAD_DIM = D_MODEL // N_MEM_HEADS
N_EXPERTS = 32
N_GROUPS = 8
D_EXPERT = 512
LN_EPS = 1e-5
RMS_EPS = 1e-6
NEG = -1e30
ALPHA = (2 * DEPTH) ** 0.25
CHUNK = 64
MOE_ROWS = 256

VMEM_LIMIT = 48 * 1024 * 1024

NT_DIMS = (((1,), (1,)), ((), ()))
TN_DIMS = (((0,), (0,)), ((), ()))


def _cparams(sem):
    return pltpu.CompilerParams(dimension_semantics=sem, vmem_limit_bytes=VMEM_LIMIT)


def _split(a):
    hi = a.astype(BF16)
    lo = (a - hi.astype(F32)).astype(BF16)
    return hi, lo


def _dot1(a, b, dims=(((1,), (0,)), ((), ()))):
    return lax.dot_general(a.astype(BF16), b.astype(BF16), dims, preferred_element_type=F32)


def _layer_norm(y, g, b):
    mu = jnp.mean(y, -1, keepdims=True)
    d = y - mu
    var = jnp.mean(d * d, -1, keepdims=True)
    return d * lax.rsqrt(var + LN_EPS) * g + b


def _sigmoid(x):
    return 1.0 / (1.0 + jnp.exp(-x))


def _inproj_kernel(x_ref, w_ref, att_ref, rkv_ref, lora_ref):
    xb = x_ref[...].astype(BF16)
    att_ref[...] = jnp.dot(xb, w_ref[:, 0:3 * WIDTH], preferred_element_type=F32)
    rkv_ref[...] = jnp.dot(xb, w_ref[:, 3 * WIDTH:6 * WIDTH], preferred_element_type=F32)
    lora_ref[...] = jnp.dot(xb, w_ref[:, 6 * WIDTH:IN_COLS_PAD], preferred_element_type=F32)


def _inproj(x2d, w_pad, tm=256):
    n = x2d.shape[0]
    return pl.pallas_call(
        _inproj_kernel,
        grid=(n // tm,),
        in_specs=[pl.BlockSpec((tm, D_MODEL), lambda i: (i, 0)),
                  pl.BlockSpec((D_MODEL, IN_COLS_PAD), lambda i: (0, 0))],
        out_specs=[pl.BlockSpec((tm, 3 * WIDTH), lambda i: (i, 0)),
                   pl.BlockSpec((tm, 3 * WIDTH), lambda i: (i, 0)),
                   pl.BlockSpec((tm, LORA_COLS), lambda i: (i, 0))],
        out_shape=[jax.ShapeDtypeStruct((n, 3 * WIDTH), F32),
                   jax.ShapeDtypeStruct((n, 3 * WIDTH), F32),
                   jax.ShapeDtypeStruct((n, LORA_COLS), F32)],
        compiler_params=_cparams(("parallel",)),
        name="inproj",
    )(x2d, w_pad)


def _alibi_slope(head):
    return 2.0 ** (-8.0 * (head + 1) / N_HEADS)


def _moba_kernel(q_ref, k_ref, v_ref, gain_ref, o_ref, kaug_ref, vb_ref, kmp_ref, *, seq):
    i = pl.program_id(1)
    nb = seq // MOBA_BLOCK
    blk = MOBA_BLOCK
    pair_lanes = [slice(pp * LANES, (pp + 1) * LANES) for pp in range(N_PAIRS)]

    @pl.when(i == 0)
    def _():
        rowi = lax.broadcasted_iota(I32, (seq, LANES), 0)
        rowblk = lax.shift_right_logical(rowi, 8)
        rowin = rowi & (blk - 1)
        lane_s = lax.broadcasted_iota(I32, (seq, LANES), 1)
        for head in range(N_HEADS):
            pp, hh = divmod(head, 2)
            k = k_ref[:, pair_lanes[pp]]
            slope = _alibi_slope(head)
            ob = HEAD_DIM * (1 - hh)
            inhead = (lane_s >= HEAD_DIM * hh) & (lane_s < HEAD_DIM * (hh + 1))
            c = lane_s - ob
            aug = jnp.where(c == rowblk, 1.0, 0.0)
            aug = jnp.where(c == POS_BLK, (slope * blk) * rowblk.astype(F32), aug)
            aug = jnp.where(c == POS_IN, slope * rowin.astype(F32), aug)
            kaug_ref[head] = jnp.where(inhead, k, aug).astype(BF16)
            kmp_ref[head] = jnp.zeros((LANES, LANES), F32)
            kmp_ref[head, ob:ob + nb, :] = jnp.sum(k.reshape(nb, blk, LANES), axis=1) * (1.0 / blk)
        vb_ref[...] = v_ref[...].astype(BF16)

    def query_block(ii):
        causal = (lax.broadcasted_iota(I32, (blk, blk), 0) >= lax.broadcasted_iota(I32, (blk, blk), 1))
        lane = lax.broadcasted_iota(I32, (blk, LANES), 1)
        nk = (ii + 1) * blk
        inheads = [(lane >= HEAD_DIM * hh) & (lane < HEAD_DIM * (hh + 1)) for hh in range(2)]
        qas = []
        for head in range(N_HEADS):
            pp, hh = divmod(head, 2)
            q = q_ref[:, pair_lanes[pp]] * (HEAD_DIM ** -0.5)
            ob = HEAD_DIM * (1 - hh)
            inhead = inheads[hh]
            c = lane - ob
            qaug = jnp.where((c == POS_BLK) | (c == POS_IN), 1.0, 0.0)
            if ii > MOBA_TOP:
                qm = jnp.where(inhead, q, 0.0)
                gate = _dot1(kmp_ref[head], qm, NT_DIMS)[ob:ob + GATE_ROWS, :]
                jrow = lax.broadcasted_iota(I32, (GATE_ROWS, blk), 0)
                cnt = jnp.zeros((GATE_ROWS, blk), I32)
                for jp in range(ii):
                    rowv = gate[jp:jp + 1, :]
                    beats = (rowv > gate) | ((rowv == gate) & (jp < jrow))
                    cnt = cnt + jnp.where(beats, 1, 0)
                drop = jnp.where((jrow < ii) & (cnt >= MOBA_TOP), NEG, 0.0).astype(BF16)
                place = (lax.broadcasted_iota(I32, (GATE_ROWS, LANES), 1) - ob
                         == lax.broadcasted_iota(I32, (GATE_ROWS, LANES), 0))
                qaug = qaug + lax.dot_general(drop, jnp.where(place, 1.0, 0.0).astype(BF16), TN_DIMS,
                                              preferred_element_type=F32)
            qas.append(jnp.where(inhead, q, qaug).astype(BF16))

        scores = [lax.dot_general(qas[head], kaug_ref[head, 0:nk, :], NT_DIMS, preferred_element_type=F32)
                  for head in range(N_HEADS)]
        probs, sums = [], []
        for s in scores:
            parts = [s[:, j * blk:(j + 1) * blk] for j in range(ii)]
            parts.append(jnp.where(causal, s[:, ii * blk:nk], NEG))
            m = jnp.max(functools.reduce(jnp.maximum, parts), -1, keepdims=True)
            ps = [jnp.exp(x - m) for x in parts]
            sums.append(jnp.sum(functools.reduce(lambda a, b: a + b, ps), -1, keepdims=True))
            probs.append(jnp.concatenate([x.astype(BF16) for x in ps], axis=1))
        outs = []
        for head in range(N_HEADS):
            pp, hh = divmod(head, 2)
            out = jnp.dot(probs[head], vb_ref[0:nk, pair_lanes[pp]], preferred_element_type=F32) / sums[head]
            ms = jnp.sum(jnp.where(inheads[hh], out * out, 0.0), -1, keepdims=True) * (1.0 / HEAD_DIM)
            outs.append(out * lax.rsqrt(ms + RMS_EPS) * gain_ref[:, pair_lanes[pp]])
        for pp in range(N_PAIRS):
            o_ref[:, pair_lanes[pp]] = jnp.where(lane < HEAD_DIM, outs[2 * pp], outs[2 * pp + 1])

    for ii in range(nb):
        pl.when(i == ii)(functools.partial(query_block, ii))


def _moba(z_att, gain, batch, seq):
    nb = seq // MOBA_BLOCK
    return pl.pallas_call(
        functools.partial(_moba_kernel, seq=seq),
        grid=(batch, nb),
        in_specs=[pl.BlockSpec((None, MOBA_BLOCK, WIDTH), lambda b, i: (b, i, 0)),
                  pl.BlockSpec((None, seq, WIDTH), lambda b, i: (b, 0, 1)),
                  pl.BlockSpec((None, seq, WIDTH), lambda b, i: (b, 0, 2)),
                  pl.BlockSpec((1, WIDTH), lambda b, i: (0, 0))],
        out_specs=pl.BlockSpec((None, MOBA_BLOCK, WIDTH), lambda b, i: (b, i, 0)),
        out_shape=jax.ShapeDtypeStruct((batch, seq, WIDTH), F32),
        scratch_shapes=[pltpu.VMEM((N_HEADS, seq, LANES), BF16),
                        pltpu.VMEM((seq, WIDTH), BF16),
                        pltpu.VMEM((N_HEADS, LANES, LANES), F32)],
        compiler_params=_cparams(("parallel", "arbitrary")),
        name="moba",
    )(z_att, z_att, z_att, gain)


def _dot2x(a, b_exact):
    ah, al = _split(a)
    return (jnp.dot(ah, b_exact, preferred_element_type=F32)
            + jnp.dot(al, b_exact, preferred_element_type=F32))


def _head_block_diag(scale):
    r = lax.shift_right_logical(lax.broadcasted_iota(I32, (LANES, LANES), 0), 6)
    c = lax.shift_right_logical(lax.broadcasted_iota(I32, (LANES, LANES), 1), 6)
    return jnp.where(r == c, scale, 0.0).astype(BF16)


def _softplus(x):
    return jnp.maximum(x, 0.0) + jnp.log(1.0 + jnp.exp(-jnp.abs(x)))


def _rwkv_chunks(items):
    c_ = CHUNK
    tri = jnp.where(lax.broadcasted_iota(I32, (c_, c_), 0) >= lax.broadcasted_iota(I32, (c_, c_), 1),
                    1.0, 0.0).astype(BF16)
    lane = lax.broadcasted_iota(I32, (c_, LANES), 1)
    m0 = lane < HEAD_DIM
    t_idx = lax.broadcasted_iota(I32, (c_, LANES), 0)
    s_idx = lane & (HEAD_DIM - 1)
    strict = t_idx > s_idx
    eye2 = jnp.where(t_idx == s_idx, 1.0, 0.0)
    keep = jnp.concatenate([strict, t_idx >= s_idx], axis=0)

    def stack(x):
        return jnp.concatenate([jnp.where(m0, x, 0.0), jnp.where(m0, 0.0, x)], axis=0)

    splits = [_split(it[0]) for it in items]
    cs = [jnp.dot(tri, hi, preferred_element_type=F32) + jnp.dot(tri, lo, preferred_element_type=F32)
          for hi, lo in splits]
    base = []
    for (lw, r, k2, v, a, b), c in zip(items, cs):
        e_pos = jnp.exp(c)
        e_neg = jnp.exp(-c)
        at = a * jnp.exp(c - lw)
        bt = b * e_neg
        kt = k2 * e_neg
        rt = r * e_pos
        gc = e_pos[c_ - 1:c_, :]
        base.append(dict(at=at, rt=rt, gc=gc, bp=bt * gc, kp=kt * gc, vs=stack(v).astype(BF16),
                         ar=jnp.concatenate([at, rt], axis=0).astype(BF16),
                         sb=stack(bt).astype(BF16), sk=stack(kt).astype(BF16)))
    gb = [jnp.where(keep, lax.dot_general(d['ar'], d['sb'], NT_DIMS, preferred_element_type=F32), 0.0)
          for d in base]
    gk = [jnp.where(keep, lax.dot_general(d['ar'], d['sk'], NT_DIMS, preferred_element_type=F32), 0.0)
          for d in base]
    ls = [g[0:c_] for g in gb]
    ts = [eye2 + l for l in ls]
    lps = [_dot1(l, stack(l)) for l in ls]
    for _ in range(4):
        xs = [_dot1(jnp.concatenate([t, lp], axis=0), stack(lp)) for t, lp in zip(ts, lps)]
        ts = [t + x[0:c_] for t, x in zip(ts, xs)]
        lps = [x[c_:2 * c_] for x in xs]
    ts = [t + _dot1(t, stack(lp)) for t, lp in zip(ts, lps)]
    outs = []
    for d, g_b, g_k, t in zip(base, gb, gk, ts):
        pyk = jnp.dot(g_k.astype(BF16), d['vs'], preferred_element_type=F32)
        grow = jnp.broadcast_to(d['gc'], (c_, LANES))
        outs.append((d['at'].astype(BF16), pyk[0:c_], t.astype(BF16), d['bp'].astype(BF16),
                     d['kp'].astype(BF16), grow, d['rt'].astype(BF16), g_b[c_:2 * c_].astype(BF16),
                     pyk[c_:2 * c_]))
    return outs


def _rwkv_pre_kernel(*refs, has_vres, tb):
    if has_vres:
        (rkv_ref, lora_ref, vf_ref, mu_rkv_ref, mu_lora_ref, vec_ref, wd_ref, wa_ref, wg_ref, wv_ref,
         at_ref, p_ref, tc_ref, bp_ref, kp_ref, v_ref, gr_ref, rt_ref, qc_ref, yk_ref, bonus_ref, g_ref,
         carry_ref, carryl_ref) = refs
    else:
        (rkv_ref, lora_ref, mu_rkv_ref, mu_lora_ref, vec_ref, wd_ref, wa_ref, wg_ref,
         at_ref, p_ref, tc_ref, bp_ref, kp_ref, v_ref, gr_ref, rt_ref, qc_ref, yk_ref, bonus_ref, g_ref,
         carry_ref, carryl_ref) = refs
    t_idx = pl.program_id(1)

    @pl.when(t_idx == 0)
    def _():
        carry_ref[...] = jnp.zeros_like(carry_ref)
        carryl_ref[...] = jnp.zeros_like(carryl_ref)

    def token_shift(z, cref, mu):
        first = lax.broadcasted_iota(I32, z.shape, 0) == 0
        prev = jnp.where(first, cref[0:1, :], pltpu.roll(z, 1, 0))
        cref[0:1, :] = z[tb - 1:tb, :]
        return z + (prev - z) * mu

    zs = token_shift(rkv_ref[...], carry_ref, mu_rkv_ref[...])
    zl = token_shift(lora_ref[...], carryl_ref, mu_lora_ref[...])
    l0 = zl[:, 0:LANES]
    w0, a0, v0 = vec_ref[0:1, :], vec_ref[1:2, :], vec_ref[2:3, :]
    k_k, k_a, r_k = vec_ref[3:4, :], vec_ref[4:5, :], vec_ref[5:6, :]
    dw = _dot1(jnp.tanh(l0), wd_ref[...])
    lw_all = -jnp.exp(-_softplus(-(w0 + dw)) - 0.5)
    a_lr = _sigmoid(a0 + _dot1(l0, wa_ref[...]))
    g_ref[...] = _dot1(_sigmoid(zl[:, LANES:2 * LANES]), wg_ref[...])
    r_all = zs[:, 0:WIDTH]
    k_all = zs[:, WIDTH:2 * WIDTH]
    v_all = zs[:, 2 * WIDTH:3 * WIDTH]
    if has_vres:
        mix = _sigmoid(v0 + _dot1(zl[:, 2 * LANES:3 * LANES], wv_ref[...]))
        v_all = v_all + (vf_ref[...] - v_all) * mix
    v_ref[...] = v_all
    bd = _head_block_diag(1.0)
    items, where = [], []
    for pp in range(N_PAIRS):
        sl = slice(pp * LANES, (pp + 1) * LANES)
        r, k, v = r_all[:, sl], k_all[:, sl], v_all[:, sl]
        alr = a_lr[:, sl]
        kk = k * k_k[:, sl]
        ss = _dot2x(kk * kk, bd)
        kk = kk * lax.rsqrt(jnp.maximum(ss, 1e-24))
        k2 = k * (1.0 + (alr - 1.0) * k_a[:, sl])
        bonus_ref[:, sl] = _dot2x(r * k2 * r_k[:, sl], bd) * v
        a = -kk
        b = kk * alr
        lw = lw_all[:, sl]
        for ci in range(tb // CHUNK):
            rs = slice(ci * CHUNK, (ci + 1) * CHUNK)
            items.append((lw[rs], r[rs], k2[rs], v[rs], a[rs], b[rs]))
            where.append((rs, sl))
    for (rs, sl), outs in zip(where, _rwkv_chunks(items)):
        for ref, val in zip((at_ref, p_ref, tc_ref, bp_ref, kp_ref, gr_ref, rt_ref, qc_ref, yk_ref), outs):
            ref[rs, sl] = val


def _rwkv_pre(z_rkv, z_lora, v_first, mu_rkv, mu_lora, vecs, wd, wa, wg, wv, batch, seq, tb=128):
    has_vres = v_first is not None
    big = lambda w: pl.BlockSpec((None, tb, w), lambda b, t: (b, t, 0))
    full = lambda shape: pl.BlockSpec(shape, lambda b, t: (0,) * len(shape))
    in_specs = [big(3 * WIDTH), big(LORA_COLS)] + ([big(WIDTH)] if has_vres else [])
    in_specs += [full((1, 3 * WIDTH)), full((1, LORA_COLS)), full((8, WIDTH)),
                 full((LANES, WIDTH)), full((LANES, WIDTH)), full((LANES, WIDTH))]
    args = [z_rkv, z_lora] + ([v_first] if has_vres else []) + [mu_rkv, mu_lora, vecs, wd, wa, wg]
    if has_vres:
        in_specs.append(full((LANES, WIDTH)))
        args.append(wv)
    out_dtypes = [BF16, F32, BF16, BF16, BF16, F32, F32, BF16, BF16, F32, F32, F32]
    return pl.pallas_call(
        functools.partial(_rwkv_pre_kernel, has_vres=has_vres, tb=tb),
        grid=(batch, seq // tb),
        in_specs=in_specs,
        out_specs=[big(WIDTH)] * len(out_dtypes),
        out_shape=[jax.ShapeDtypeStruct((batch, seq, WIDTH), dt) for dt in out_dtypes],
        scratch_shapes=[pltpu.VMEM((8, 3 * WIDTH), F32), pltpu.VMEM((8, LORA_COLS), F32)],
        compiler_params=_cparams(("parallel", "arbitrary")),
        name="rwkv_pre",
    )(*args)


def _rwkv_scan_kernel(at_ref, p_ref, tc_ref, bp_ref, kp_ref, v_ref, gr_ref, rt_ref, qc_ref, yk_ref,
                      bonus_ref, g_ref, lnx_ref, o_ref, st_ref, *, ts):
    t_idx = pl.program_id(1)

    @pl.when(t_idx == 0)
    def _():
        st_ref[...] = jnp.zeros_like(st_ref)

    lane = lax.broadcasted_iota(I32, (CHUNK, LANES), 1)
    m0 = lane < HEAD_DIM
    bdmask = (lax.shift_right_logical(lax.broadcasted_iota(I32, (LANES, LANES), 0), 6)
              == lax.shift_right_logical(lax.broadcasted_iota(I32, (LANES, LANES), 1), 6))
    ones_t = jnp.full((CHUNK, LANES), 1.0 / CHUNK, BF16)

    def stack(x):
        return jnp.concatenate([jnp.where(m0, x, 0.0), jnp.where(m0, 0.0, x)], axis=0)

    tn = functools.partial(lax.dot_general, dimension_numbers=TN_DIMS, preferred_element_type=F32)
    nseq = at_ref.shape[0]
    chains = [(bb, pp, slice(pp * LANES, (pp + 1) * LANES)) for bb in range(nseq) for pp in range(N_PAIRS)]
    chunks = [slice(ci * CHUNK, (ci + 1) * CHUNK) for ci in range(ts // CHUNK)]

    gcols, kvs = {}, {}
    for ci, rs in enumerate(chunks):
        for bb, pp, sl in chains:
            gh, gl = _split(gr_ref[bb, rs, sl])
            gcols[ci, bb, pp] = tn(gh, ones_t) + tn(gl, ones_t)
            kvs[ci, bb, pp] = jnp.where(bdmask, _dot1(kp_ref[bb, rs, sl], v_ref[bb, rs, sl], TN_DIMS), 0.0)

    states = [st_ref[bb, pp] for bb, pp, _ in chains]
    for ci, rs in enumerate(chunks):
        sts = [st.astype(BF16) for st in states]
        xs = [_dot1(at_ref[bb, rs, sl], sts[n]) + p_ref[bb, rs, sl] for n, (bb, pp, sl) in enumerate(chains)]
        us = [_dot1(tc_ref[bb, rs, sl], stack(xs[n])).astype(BF16) for n, (bb, pp, sl) in enumerate(chains)]
        new = [gcols[ci, bb, pp] * states[n]
               + (jnp.where(bdmask, _dot1(bp_ref[bb, rs, sl], us[n], TN_DIMS), 0.0) + kvs[ci, bb, pp])
               for n, (bb, pp, sl) in enumerate(chains)]
        for n, (bb, pp, sl) in enumerate(chains):
            lhs = jnp.concatenate([rt_ref[bb, rs, sl], qc_ref[bb, rs, sl]], axis=1)
            rhs = jnp.concatenate([sts[n], stack(us[n])], axis=0)
            o_ref[bb, rs, sl] = jnp.dot(lhs, rhs, preferred_element_type=F32) + yk_ref[bb, rs, sl]
        states = new
    for n, (bb, pp, _) in enumerate(chains):
        st_ref[bb, pp] = states[n]

    bd = _head_block_diag(1.0 / HEAD_DIM)
    for bb, pp, sl in chains:
        y = o_ref[bb, :, sl]
        mu = _dot2x(y, bd)
        d = y - mu
        var = _dot2x(d * d, bd)
        yn = d * lax.rsqrt(var + RWKV_GN_EPS) * lnx_ref[0:1, sl] + lnx_ref[1:2, sl]
        o_ref[bb, :, sl] = (yn + bonus_ref[bb, :, sl]) * g_ref[bb, :, sl]


def _rwkv_scan(pre, lnx, batch, seq, ts=128):
    nseq = next(c for c in (4, 2, 1) if batch % c == 0)
    big = pl.BlockSpec((nseq, ts, WIDTH), lambda b, t: (b, t, 0))
    return pl.pallas_call(
        functools.partial(_rwkv_scan_kernel, ts=ts),
        grid=(batch // nseq, seq // ts),
        in_specs=[big] * 12 + [pl.BlockSpec((8, WIDTH), lambda b, t: (0, 0))],
        out_specs=big,
        out_shape=jax.ShapeDtypeStruct((batch, seq, WIDTH), F32),
        scratch_shapes=[pltpu.VMEM((nseq, N_PAIRS, LANES, LANES), F32)],
        compiler_params=_cparams(("parallel", "arbitrary")),
        name="rwkv_scan",
    )(*pre, lnx)


def _outproj_kernel(x_ref, oa_ref, orw_ref, w_ref, ln_ref, o_ref):
    y = (jnp.dot(oa_ref[...].astype(BF16), w_ref[0:WIDTH, :], preferred_element_type=F32)
         + jnp.dot(orw_ref[...].astype(BF16), w_ref[WIDTH:2 * WIDTH, :], preferred_element_type=F32))
    o_ref[...] = _layer_norm(ALPHA * x_ref[...] + y, ln_ref[0:1, :], ln_ref[1:2, :])


def _outproj(x2d, o_att, o_rwkv, w_out, ln, tm=512):
    n = x2d.shape[0]
    return pl.pallas_call(
        _outproj_kernel,
        grid=(n // tm,),
        in_specs=[pl.BlockSpec((tm, D_MODEL), lambda i: (i, 0)),
                  pl.BlockSpec((tm, WIDTH), lambda i: (i, 0)),
                  pl.BlockSpec((tm, WIDTH), lambda i: (i, 0)),
                  pl.BlockSpec((2 * WIDTH, D_MODEL), lambda i: (0, 0)),
                  pl.BlockSpec((8, D_MODEL), lambda i: (0, 0))],
        out_specs=pl.BlockSpec((tm, D_MODEL), lambda i: (i, 0)),
        out_shape=jax.ShapeDtypeStruct((n, D_MODEL), F32),
        compiler_params=_cparams(("parallel",)),
        name="outproj_ln",
    )(x2d, o_att, o_rwkv, w_out, ln)


def _kvproj_kernel(m_ref, w_ref, o_ref):
    o_ref[...] = jnp.dot(m_ref[...].astype(BF16), w_ref[...], preferred_element_type=F32).astype(BF16)


def _kvproj(mem2d, w_kv, tm=256):
    n = mem2d.shape[0]
    return pl.pallas_call(
        _kvproj_kernel,
        grid=(n // tm,),
        in_specs=[pl.BlockSpec((tm, D_MODEL), lambda i: (i, 0)),
                  pl.BlockSpec((D_MODEL, 2 * D_MODEL), lambda i: (0, 0))],
        out_specs=pl.BlockSpec((tm, 2 * D_MODEL), lambda i: (i, 0)),
        out_shape=jax.ShapeDtypeStruct((n, 2 * D_MODEL), BF16),
        compiler_params=_cparams(("parallel",)),
        name="kvproj",
    )(mem2d, w_kv)


def _route_lanes(logits, b_router):
    lane = lax.broadcasted_iota(I32, logits.shape, 1)
    lane_f = lane.astype(F32)
    valid = lane < N_EXPERTS
    s = _sigmoid(logits)
    sel = jnp.where(valid, s + b_router, NEG)

    def partner(x, bit):
        return jnp.where((lane & bit) == 0, pltpu.roll(x, LANES - bit, 1), pltpu.roll(x, bit, 1))

    p1 = partner(sel, 1)
    hi1, lo1 = jnp.maximum(sel, p1), jnp.minimum(sel, p1)
    hi2, lo2 = partner(hi1, 2), partner(lo1, 2)
    gsum = jnp.maximum(hi1, hi2) + jnp.maximum(jnp.minimum(hi1, hi2), jnp.maximum(lo1, lo2))
    gmax = jnp.max(gsum, -1, keepdims=True)
    grp_f = lax.shift_right_logical(lane, 2).astype(F32)
    g_best = jnp.min(jnp.where((gsum == gmax) & valid, grp_f, 1e9), -1, keepdims=True)
    in_grp = (grp_f == g_best) & valid
    masked = jnp.where(in_grp, sel, NEG)
    v1 = jnp.max(masked, -1, keepdims=True)
    e1 = jnp.min(jnp.where(in_grp & (masked == v1), lane_f, 1e9), -1, keepdims=True)
    rest = in_grp & (lane_f != e1)
    masked2 = jnp.where(rest, sel, NEG)
    v2 = jnp.max(masked2, -1, keepdims=True)
    e2 = jnp.min(jnp.where(rest & (masked2 == v2), lane_f, 1e9), -1, keepdims=True)
    w1 = jnp.sum(jnp.where(lane_f == e1, s, 0.0), -1, keepdims=True)
    w2 = jnp.sum(jnp.where(lane_f == e2, s, 0.0), -1, keepdims=True)
    tot = w1 + w2
    out = jnp.where(lane == 0, e1, jnp.where(lane == 1, e2, jnp.where(lane == 2, w1 / tot, w2 / tot)))
    return jnp.where(lane < 4, out, 0.0)


def _memattn_kernel(x_ref, kv_ref, wq_ref, wo_ref, ln_ref, wr_ref, br_ref, o_ref, route_ref):
    x = x_ref[...]
    q = jnp.dot(x.astype(BF16), wq_ref[...], preferred_element_type=F32) * (MEM_HEAD_DIM ** -0.5)
    heads = []
    for h in range(N_MEM_HEADS):
        sl = slice(h * MEM_HEAD_DIM, (h + 1) * MEM_HEAD_DIM)
        kh = kv_ref[:, sl]
        vh = kv_ref[:, D_MODEL + h * MEM_HEAD_DIM:D_MODEL + (h + 1) * MEM_HEAD_DIM]
        s = lax.dot_general(q[:, sl].astype(BF16), kh, NT_DIMS, preferred_element_type=F32)
        p = jnp.exp(s - jnp.max(s, -1, keepdims=True))
        o = jnp.dot(p.astype(BF16), vh, preferred_element_type=F32)
        heads.append(o / jnp.sum(p, -1, keepdims=True))
    o = jnp.concatenate(heads, axis=-1).astype(BF16)
    y = jnp.dot(o, wo_ref[...], preferred_element_type=F32)
    x2 = _layer_norm(ALPHA * x + y, ln_ref[0:1, :], ln_ref[1:2, :])
    o_ref[...] = x2
    logits = _dot1(x2, wr_ref[...])
    route_ref[...] = _route_lanes(logits, br_ref[0:1, :])


def _memattn(x3d, kv3d, wq, wo, ln, w_router_pad, b_router_pad, tm=512):
    batch, seq, _ = x3d.shape
    mlen = kv3d.shape[1]
    full = lambda shape: pl.BlockSpec(shape, lambda b, t: (0,) * len(shape))
    return pl.pallas_call(
        _memattn_kernel,
        grid=(batch, seq // tm),
        in_specs=[pl.BlockSpec((None, tm, D_MODEL), lambda b, t: (b, t, 0)),
                  pl.BlockSpec((None, mlen, 2 * D_MODEL), lambda b, t: (b, 0, 0)),
                  full((D_MODEL, D_MODEL)), full((D_MODEL, D_MODEL)), full((8, D_MODEL)),
                  full((D_MODEL, LANES)), full((8, LANES))],
        out_specs=[pl.BlockSpec((None, tm, D_MODEL), lambda b, t: (b, t, 0)),
                   pl.BlockSpec((None, tm, LANES), lambda b, t: (b, t, 0))],
        out_shape=[jax.ShapeDtypeStruct((batch, seq, D_MODEL), F32),
                   jax.ShapeDtypeStruct((batch, seq, LANES), F32)],
        compiler_params=_cparams(("parallel", "parallel")),
        name="memattn_ln",
    )(x3d, kv3d, wq, wo, ln, w_router_pad, b_router_pad)


def _plan_kernel(route_ref, dest_ref, stat_ref, acc_ref):
    phase = pl.program_id(0)
    i = pl.program_id(1)
    tm = route_ref.shape[0]
    lane = lax.broadcasted_iota(I32, (tm, LANES), 1).astype(F32)
    e0 = route_ref[:, 0:1]
    e1 = route_ref[:, 1:2]
    hot = jnp.where((lane == e0) | (lane == e1), 1.0, 0.0)

    @pl.when((phase == 0) & (i == 0))
    def _():
        acc_ref[...] = jnp.zeros_like(acc_ref)

    @pl.when(phase == 0)
    def _():
        acc_ref[0:1, :] = acc_ref[0:1, :] + jnp.sum(hot, axis=0, keepdims=True)

    @pl.when((phase == 1) & (i == 0))
    def _():
        r = lax.broadcasted_iota(I32, (LANES, LANES), 0)
        c = lax.broadcasted_iota(I32, (LANES, LANES), 1)
        before = jnp.where(r < c, 1.0, 0.0)
        cnt8 = jnp.broadcast_to(acc_ref[0:1, :], (8, LANES))
        acc_ref[1:2, :] = _dot2x(cnt8, before.astype(BF16))[0:1, :]

    @pl.when(phase == 1)
    def _():
        r = lax.broadcasted_iota(I32, (tm, tm), 0)
        c = lax.broadcasted_iota(I32, (tm, tm), 1)
        earlier = jnp.where(c < r, 1.0, 0.0).astype(BF16)
        pos = (jnp.dot(earlier, hot.astype(BF16), preferred_element_type=F32)
               + acc_ref[1:2, :] + acc_ref[2:3, :])
        d0 = jnp.sum(jnp.where(lane == e0, pos, 0.0), -1, keepdims=True)
        d1 = jnp.sum(jnp.where(lane == e1, pos, 0.0), -1, keepdims=True)
        dest_ref[...] = jnp.where(lane == 0.0, d0, jnp.where(lane == 1.0, d1, 0.0)).astype(I32)
        acc_ref[2:3, :] = acc_ref[2:3, :] + jnp.sum(hot, axis=0, keepdims=True)
        stat_ref[...] = acc_ref[...]


def _plan(route, tm=1024):
    n = route.shape[0]
    tm = min(tm, n)
    return pl.pallas_call(
        _plan_kernel,
        grid=(2, n // tm),
        in_specs=[pl.BlockSpec((tm, LANES), lambda p, i: (i, 0))],
        out_specs=[pl.BlockSpec((tm, LANES), lambda p, i: (i * p, 0)),
                   pl.BlockSpec((8, LANES), lambda p, i: (0, 0))],
        out_shape=[jax.ShapeDtypeStruct((n, LANES), I32), jax.ShapeDtypeStruct((8, LANES), F32)],
        scratch_shapes=[pltpu.VMEM((8, LANES), F32)],
        compiler_params=_cparams(("arbitrary", "arbitrary")),
        name="moe_plan",
    )(route)


def _dispatch_kernel(dest_ref, prev_ref, x_ref, xs_hbm, stage_ref, sems):
    i = pl.program_id(0)
    tm = dest_ref.shape[2] // 2
    stage_ref[i % 2] = x_ref[...]

    def copy(step, dref, t, slot):
        src = stage_ref.at[step % 2, pl.ds(t, 1)]
        return pltpu.make_async_copy(src, xs_hbm.at[pl.ds(dref[0, 0, slot * tm + t], 1)], sems.at[step % 2])

    def start(t, c):
        copy(i, dest_ref, t, 0).start()
        copy(i, dest_ref, t, 1).start()
        return c

    def wait_step(step, dref):
        def wait(t, c):
            copy(step, dref, t, 0).wait()
            copy(step, dref, t, 1).wait()
            return c
        lax.fori_loop(0, tm, wait, 0, unroll=8)

    lax.fori_loop(0, tm, start, 0, unroll=8)

    @pl.when(i > 0)
    def _():
        wait_step(i - 1, prev_ref)

    @pl.when(i == pl.num_programs(0) - 1)
    def _():
        wait_step(i, dest_ref)


def _dispatch(x2d, dest3, n_rows):
    n_steps = dest3.shape[0]
    tm = dest3.shape[2] // 2
    blk = (1, 1, 2 * tm)
    return pl.pallas_call(
        _dispatch_kernel,
        grid=(n_steps,),
        in_specs=[pl.BlockSpec(blk, lambda i: (i, 0, 0), memory_space=pltpu.SMEM),
                  pl.BlockSpec(blk, lambda i: (jnp.maximum(i - 1, 0), 0, 0), memory_space=pltpu.SMEM),
                  pl.BlockSpec((tm, D_MODEL), lambda i: (i, 0))],
        out_specs=pl.BlockSpec(memory_space=pl.ANY),
        out_shape=jax.ShapeDtypeStruct((n_rows, D_MODEL), F32),
        scratch_shapes=[pltpu.VMEM((2, tm, D_MODEL), F32), pltpu.SemaphoreType.DMA((2,))],
        compiler_params=_cparams(("arbitrary",)),
        name="moe_dispatch",
    )(dest3, dest3, x2d)


def _experts_kernel(rb_ref, ex_ref, nit_ref, off_ref, cnt_ref, x_ref, wg_ref, wu_ref, wd_ref, o_ref,
                    wgb_ref, wub_ref, wdb_ref):
    w = pl.program_id(0)
    rows = x_ref.shape[0]

    @pl.when(w < nit_ref[0])
    def _():
        e = ex_ref[w]
        rb = rb_ref[w]
        prev = jnp.maximum(w - 1, 0)

        @pl.when((w == 0) | (ex_ref[prev] != e))
        def _():
            wgb_ref[...] = wg_ref[...].astype(BF16)
            wub_ref[...] = wu_ref[...].astype(BF16)
            wdb_ref[...] = wd_ref[...].astype(BF16)

        xb = x_ref[...].astype(BF16)
        hg = jnp.dot(xb, wgb_ref[...], preferred_element_type=F32)
        hu = jnp.dot(xb, wub_ref[...], preferred_element_type=F32)
        h = hg * _sigmoid(hg) * hu
        y = jnp.dot(h.astype(BF16), wdb_ref[...], preferred_element_type=F32)
        row = rb * rows + lax.broadcasted_iota(I32, (rows, 1), 0)
        mine = (row >= off_ref[e]) & (row < off_ref[e] + cnt_ref[e])
        first = (w == 0) | (rb_ref[prev] != rb)

        @pl.when(first)
        def _():
            o_ref[...] = jnp.where(mine, y, 0.0)

        @pl.when(jnp.logical_not(first))
        def _():
            o_ref[...] = jnp.where(mine, y, o_ref[...])


def _experts(xs, rb, ex, n_items, off, cnt, w_gate, w_up, w_down, layer, rows):
    n_work = rb.shape[0]
    item = lambda w, nit: jnp.minimum(w, nit[0] - 1)
    row_map = lambda w, rb, ex, nit, off, cnt: (rb[item(w, nit)], 0)
    w_map = lambda w, rb, ex, nit, off, cnt: (layer, ex[item(w, nit)], 0, 0)
    grid_spec = pltpu.PrefetchScalarGridSpec(
        num_scalar_prefetch=5,
        grid=(n_work,),
        in_specs=[pl.BlockSpec((rows, D_MODEL), row_map),
                  pl.BlockSpec((None, None, D_MODEL, D_EXPERT), w_map),
                  pl.BlockSpec((None, None, D_MODEL, D_EXPERT), w_map),
                  pl.BlockSpec((None, None, D_EXPERT, D_MODEL), w_map)],
        out_specs=pl.BlockSpec((rows, D_MODEL), row_map),
        scratch_shapes=[pltpu.VMEM((D_MODEL, D_EXPERT), BF16), pltpu.VMEM((D_MODEL, D_EXPERT), BF16),
                        pltpu.VMEM((D_EXPERT, D_MODEL), BF16)],
    )
    return pl.pallas_call(
        _experts_kernel,
        grid_spec=grid_spec,
        out_shape=jax.ShapeDtypeStruct(xs.shape, F32),
        compiler_params=_cparams(("arbitrary",)),
        name="moe_experts",
    )(rb, ex, n_items, off, cnt, xs, w_gate, w_up, w_down)


def _combine_kernel(dest_ref, next_ref, x_ref, gate_ref, ln_ref, y_hbm, o_ref, buf_ref, sems):
    i = pl.program_id(0)
    tm = x_ref.shape[0]

    def copy(step, dref, slot, t):
        src = y_hbm.at[pl.ds(dref[0, 0, slot * tm + t], 1)]
        return pltpu.make_async_copy(src, buf_ref.at[step % 2, slot, pl.ds(t, 1)], sems.at[step % 2])

    def start_step(step, dref):
        def start(t, c):
            copy(step, dref, 0, t).start()
            copy(step, dref, 1, t).start()
            return c
        lax.fori_loop(0, tm, start, 0, unroll=8)

    @pl.when(i == 0)
    def _():
        start_step(0, dest_ref)

    @pl.when(i + 1 < pl.num_programs(0))
    def _():
        start_step(i + 1, next_ref)

    def wait(t, c):
        copy(i, dest_ref, 0, t).wait()
        copy(i, dest_ref, 1, t).wait()
        return c

    lax.fori_loop(0, tm, wait, 0, unroll=8)
    cur = i % 2
    y = gate_ref[:, 0:1] * buf_ref[cur, 0] + gate_ref[:, 1:2] * buf_ref[cur, 1]
    o_ref[...] = _layer_norm(ALPHA * x_ref[...] + y, ln_ref[0:1, :], ln_ref[1:2, :])


def _combine(x2d, gates, dest3, ys, ln, tm=256):
    n = x2d.shape[0]
    n_steps = n // tm
    blk = (1, 1, 2 * tm)
    return pl.pallas_call(
        _combine_kernel,
        grid=(n_steps,),
        in_specs=[pl.BlockSpec(blk, lambda i: (i, 0, 0), memory_space=pltpu.SMEM),
                  pl.BlockSpec(blk, lambda i: (jnp.minimum(i + 1, n_steps - 1), 0, 0), memory_space=pltpu.SMEM),
                  pl.BlockSpec((tm, D_MODEL), lambda i: (i, 0)),
                  pl.BlockSpec((tm, 2), lambda i: (i, 0)),
                  pl.BlockSpec((8, D_MODEL), lambda i: (0, 0)),
                  pl.BlockSpec(memory_space=pl.ANY)],
        out_specs=pl.BlockSpec((tm, D_MODEL), lambda i: (i, 0)),
        out_shape=jax.ShapeDtypeStruct((n, D_MODEL), F32),
        scratch_shapes=[pltpu.VMEM((2, 2, tm, D_MODEL), F32), pltpu.SemaphoreType.DMA((2,))],
        compiler_params=_cparams(("arbitrary",)),
        name="moe_combine_ln",
    )(dest3, dest3, x2d, gates, ln, ys)


def _work_items(counts, off, n_rows, rows):
    n_work = n_rows // rows + N_EXPERTS
    first = off // rows
    last = (off + counts - 1) // rows
    per_expert = jnp.where(counts > 0, last - first + 1, 0)
    item_end = jnp.cumsum(per_expert)
    item_start = item_end - per_expert
    w = jnp.arange(n_work, dtype=I32)
    ex = jnp.minimum(jnp.sum((item_end[None, :] <= w[:, None]).astype(I32), axis=1), N_EXPERTS - 1)
    rb = jnp.clip(first[ex] + (w - item_start[ex]), 0, n_rows // rows - 1).astype(I32)
    return rb, ex, item_end[-1:].astype(I32)


def _moe(x2, route, w_gate, w_up, w_down, layer, ln, rows=MOE_ROWS, tm=256):
    n = x2.shape[0]
    n_rows = 2 * n
    gates = route[:, 2:4]
    dest_tile, stat = _plan(route)
    counts = stat[0, :N_EXPERTS].astype(I32)
    off = stat[1, :N_EXPERTS].astype(I32)
    rb, ex, n_items = _work_items(counts, off, n_rows, rows)
    dest3 = dest_tile[:, 0:2].reshape(n // tm, tm, 2).transpose(0, 2, 1).reshape(n // tm, 1, 2 * tm)
    xs = _dispatch(x2, dest3, n_rows)
    ys = _experts(xs, rb, ex, n_items, off, counts, w_gate, w_up, w_down, layer, rows)
    return _combine(x2, gates, dest3, ys, ln, tm)


def _pad_rows(w, rows_before, total):
    return jnp.pad(w, ((rows_before, total - rows_before - w.shape[0]), (0, 0)))


def kernel(x, mem, w_in, w_in_vres, mu_shift, mu_vres, w0, w_decay_up, a0, w_iclr_up, v0, w_vres_up,
           w_gate_up, k_k, k_a, r_k, lnx_g, lnx_b, attn_out_g, w_out, w_mem_q, w_mem_kv, w_mem_o,
           w_router, b_router, w_exp_gate, w_exp_up, w_exp_down, ln_g, ln_b):
    batch, seq, d = x.shape
    n = batch * seq
    mlen = mem.shape[1]
    zeros = jnp.zeros
    w_router_pad = jnp.pad(w_router, ((0, 0), (0, LANES - N_EXPERTS)))
    b_router_pad = jnp.pad(b_router.reshape(1, N_EXPERTS), ((0, 7), (0, LANES - N_EXPERTS)))
    mem2d = mem.reshape(batch * mlen, d)
    n_shift = w_in.shape[2] - 3 * WIDTH
    n_lora = n_shift - 3 * WIDTH
    n_vres = w_in_vres.shape[2]
    x2d = x.reshape(n, d)
    v_first = None
    for l in range(DEPTH):
        has_vres = l > 0
        w_extra = w_in_vres[l - 1] if has_vres else zeros((d, n_vres), F32)
        w_pad = jnp.concatenate(
            [w_in[l], w_extra, zeros((d, IN_COLS_PAD - w_in.shape[2] - n_vres), F32)], axis=1).astype(BF16)
        z_att, z_rkv, z_lora = _inproj(x2d, w_pad)
        o_att = _moba(z_att.reshape(batch, seq, 3 * WIDTH), attn_out_g[l].reshape(1, WIDTH), batch, seq)

        mu_rkv = mu_shift[l, :3 * WIDTH].reshape(1, 3 * WIDTH)
        mu_extra = mu_vres[l - 1] if has_vres else zeros((n_vres,), F32)
        mu_lora = jnp.concatenate(
            [mu_shift[l, 3 * WIDTH:], mu_extra, zeros((LORA_COLS - n_lora - n_vres,), F32)]).reshape(1, LORA_COLS)
        v0_l = v0[l - 1] if has_vres else zeros((WIDTH,), F32)
        vecs = jnp.stack([w0[l], a0[l], v0_l, k_k[l], k_a[l], r_k[l].reshape(WIDTH),
                          zeros((WIDTH,), F32), zeros((WIDTH,), F32)])
        n_dec, n_iclr = w_decay_up.shape[1], w_iclr_up.shape[1]
        wd = _pad_rows(w_decay_up[l], 0, LANES).astype(BF16)
        wa = _pad_rows(w_iclr_up[l], n_dec, LANES).astype(BF16)
        wg = w_gate_up[l].astype(BF16)
        wv = _pad_rows(w_vres_up[l - 1], 0, LANES).astype(BF16) if has_vres else None
        pre = _rwkv_pre(z_rkv.reshape(batch, seq, 3 * WIDTH), z_lora.reshape(batch, seq, LORA_COLS),
                        v_first, mu_rkv, mu_lora, vecs, wd, wa, wg, wv, batch, seq)
        if not has_vres:
            v_first = pre[5]
        lnx = jnp.concatenate([lnx_g[l][None], lnx_b[l][None], zeros((6, WIDTH), F32)])
        o_rwkv = _rwkv_scan(pre, lnx, batch, seq)

        ln = lambda j: jnp.concatenate([ln_g[l, j][None], ln_b[l, j][None], zeros((6, d), F32)])
        x1 = _outproj(x2d, o_att.reshape(n, WIDTH), o_rwkv.reshape(n, WIDTH), w_out[l].astype(BF16), ln(0))

        kv = _kvproj(mem2d, w_mem_kv[l].astype(BF16)).reshape(batch, mlen, 2 * d)
        x2, route = _memattn(x1.reshape(batch, seq, d), kv, w_mem_q[l].astype(BF16),
                             w_mem_o[l].astype(BF16), ln(1), w_router_pad, b_router_pad)
        x2d = _moe(x2.reshape(n, d), route.reshape(n, LANES), w_exp_gate, w_exp_up, w_exp_down, l, ln(2))
    return x2d.reshape(batch, seq, d)
```

```python
import functools

import jax
import jax.numpy as jnp
from jax import lax
from jax.experimental import pallas as pl
from jax.experimental.pallas import tpu as pltpu

F32 = jnp.float32
BF16 = jnp.bfloat16
I32 = jnp.int32

D_MODEL = 1024
DEPTH = 2
N_HEADS = 8
HEAD_DIM = 64
WIDTH = N_HEADS * HEAD_DIM
N_PAIRS = N_HEADS // 2
LANES = 128
MOBA_BLOCK = 256
MOBA_TOP = 3
GATE_ROWS = 16
POS_BLK = 16
POS_IN = 17
LORA_COLS = 384
IN_COLS_PAD = 3 * WIDTH + 3 * WIDTH + LORA_COLS
RWKV_GN_EPS = 64e-5
N_MEM_HEADS = 4
MEM_HEAD_DIM = D_MODEL // N_MEM_HEADS
N_EXPERTS = 32
N_GROUPS = 8
D_EXPERT = 512
LN_EPS = 1e-5
RMS_EPS = 1e-6
NEG = -1e30
ALPHA = (2 * DEPTH) ** 0.25
CHUNK = 64
MOE_ROWS = 256

VMEM_LIMIT = 48 * 1024 * 1024

NT_DIMS = (((1,), (1,)), ((), ()))
TN_DIMS = (((0,), (0,)), ((), ()))


def _cparams(sem):
    return pltpu.CompilerParams(dimension_semantics=sem, vmem_limit_bytes=VMEM_LIMIT)


def _split(a):
    hi = a.astype(BF16)
    lo = (a - hi.astype(F32)).astype(BF16)
    return hi, lo


def _dot1(a, b, dims=(((1,), (0,)), ((), ()))):
    return lax.dot_general(a.astype(BF16), b.astype(BF16), dims, preferred_element_type=F32)


def _layer_norm(y, g, b):
    mu = jnp.mean(y, -1, keepdims=True)
    d = y - mu
    var = jnp.mean(d * d, -1, keepdims=True)
    return d * lax.rsqrt(var + LN_EPS) * g + b


def _sigmoid(x):
    return 1.0 / (1.0 + jnp.exp(-x))


def _inproj_kernel(x_ref, w_ref, att_ref, rkv_ref, lora_ref):
    xb = x_ref[...].astype(BF16)
    att_ref[...] = jnp.dot(xb, w_ref[:, 0:3 * WIDTH], preferred_element_type=F32)
    rkv_ref[...] = jnp.dot(xb, w_ref[:, 3 * WIDTH:6 * WIDTH], preferred_element_type=F32)
    lora_ref[...] = jnp.dot(xb, w_ref[:, 6 * WIDTH:IN_COLS_PAD], preferred_element_type=F32)


def _inproj(x2d, w_pad, tm=256):
    n = x2d.shape[0]
    return pl.pallas_call(
        _inproj_kernel,
        grid=(n // tm,),
        in_specs=[pl.BlockSpec((tm, D_MODEL), lambda i: (i, 0)),
                  pl.BlockSpec((D_MODEL, IN_COLS_PAD), lambda i: (0, 0))],
        out_specs=[pl.BlockSpec((tm, 3 * WIDTH), lambda i: (i, 0)),
                   pl.BlockSpec((tm, 3 * WIDTH), lambda i: (i, 0)),
                   pl.BlockSpec((tm, LORA_COLS), lambda i: (i, 0))],
        out_shape=[jax.ShapeDtypeStruct((n, 3 * WIDTH), F32),
                   jax.ShapeDtypeStruct((n, 3 * WIDTH), F32),
                   jax.ShapeDtypeStruct((n, LORA_COLS), F32)],
        compiler_params=_cparams(("parallel",)),
        name="inproj",
    )(x2d, w_pad)


def _moba_kernel(slopes_ref, q_ref, qf_ref, k_ref, v_ref, gain_ref, o_ref, kaug_ref, vb_ref, kmp_ref,
                 sel_ref, *, seq):
    pair = pl.program_id(1)
    i = pl.program_id(2)
    nb = seq // MOBA_BLOCK
    blk = MOBA_BLOCK

    @pl.when(i == 0)
    def _():
        k = k_ref[...]
        rowi = lax.broadcasted_iota(I32, (seq, LANES), 0)
        rowblk = lax.shift_right_logical(rowi, 8)
        rowin = rowi & (blk - 1)
        lane_s = lax.broadcasted_iota(I32, (seq, LANES), 1)
        kmean = jnp.sum(k.reshape(nb, blk, LANES), axis=1) * (1.0 / blk)
        for hh in range(2):
            slope = slopes_ref[2 * pair + hh]
            ob = HEAD_DIM * (1 - hh)
            inhead = (lane_s >= HEAD_DIM * hh) & (lane_s < HEAD_DIM * (hh + 1))
            c = lane_s - ob
            aug = jnp.where(c == rowblk, 1.0, 0.0)
            aug = jnp.where(c == POS_BLK, (slope * blk) * rowblk.astype(F32), aug)
            aug = jnp.where(c == POS_IN, slope * rowin.astype(F32), aug)
            kaug_ref[hh] = jnp.where(inhead, k, aug).astype(BF16)
            kmp_ref[hh] = jnp.zeros((LANES, LANES), F32)
            kmp_ref[hh, ob:ob + nb, :] = kmean
            qm = jnp.where(inhead, qf_ref[...] * (HEAD_DIM ** -0.5), 0.0)
            gate = _dot1(kmp_ref[hh], qm, NT_DIMS)[ob:ob + GATE_ROWS, :]
            jrow = lax.broadcasted_iota(I32, (GATE_ROWS, seq), 0)
            qblk = lax.shift_right_logical(lax.broadcasted_iota(I32, (GATE_ROWS, seq), 1), 8)
            cnt = jnp.zeros((GATE_ROWS, seq), I32)
            for jp in range(nb - 1):
                rowv = gate[jp:jp + 1, :]
                beats = ((rowv > gate) | ((rowv == gate) & (jp < jrow))) & (jp < qblk)
                cnt = cnt + jnp.where(beats, 1, 0)
            drop = jnp.where((jrow < qblk) & (cnt >= MOBA_TOP) & (qblk > MOBA_TOP), NEG, 0.0).astype(BF16)
            place = (lax.broadcasted_iota(I32, (GATE_ROWS, LANES), 1) - ob
                     == lax.broadcasted_iota(I32, (GATE_ROWS, LANES), 0))
            sel_ref[hh] = lax.dot_general(drop, jnp.where(place, 1.0, 0.0).astype(BF16), TN_DIMS,
                                          preferred_element_type=F32)
        vb_ref[...] = v_ref[...].astype(BF16)

    def query_block(ii):
        q = q_ref[...] * (HEAD_DIM ** -0.5)
        causal = (lax.broadcasted_iota(I32, (blk, blk), 0) >= lax.broadcasted_iota(I32, (blk, blk), 1))
        lane = lax.broadcasted_iota(I32, (blk, LANES), 1)
        gain = gain_ref[...]
        nk = (ii + 1) * blk
        inheads = [(lane >= HEAD_DIM * hh) & (lane < HEAD_DIM * (hh + 1)) for hh in range(2)]
        qas = []
        for hh in range(2):
            ob = HEAD_DIM * (1 - hh)
            inhead = inheads[hh]
            c = lane - ob
            qaug = jnp.where((c == POS_BLK) | (c == POS_IN), 1.0, 0.0)
            if ii > MOBA_TOP:
                qaug = qaug + sel_ref[hh, ii * blk:(ii + 1) * blk, :]
            qas.append(jnp.where(inhead, q, qaug).astype(BF16))

        scores = [lax.dot_general(qas[hh], kaug_ref[hh, 0:nk, :], NT_DIMS, preferred_element_type=F32)
                  for hh in range(2)]
        probs, sums = [], []
        for s in scores:
            parts = [s[:, j * blk:(j + 1) * blk] for j in range(ii)]
            parts.append(jnp.where(causal, s[:, ii * blk:nk], NEG))
            m = jnp.max(functools.reduce(jnp.maximum, parts), -1, keepdims=True)
            ps = [jnp.exp(x - m) for x in parts]
            sums.append(jnp.sum(functools.reduce(lambda a, b: a + b, ps), -1, keepdims=True))
            probs.append(jnp.concatenate([x.astype(BF16) for x in ps], axis=1))
        outs = []
        for hh in range(2):
            out = jnp.dot(probs[hh], vb_ref[0:nk, :], preferred_element_type=F32) / sums[hh]
            ms = jnp.sum(jnp.where(inheads[hh], out * out, 0.0), -1, keepdims=True) * (1.0 / HEAD_DIM)
            outs.append(out * lax.rsqrt(ms + RMS_EPS) * gain)
        o_ref[...] = jnp.where(lane < HEAD_DIM, outs[0], outs[1])

    for ii in range(nb):
        pl.when(i == ii)(functools.partial(query_block, ii))


def _moba(z_att, slopes, gain, batch, seq):
    nb = seq // MOBA_BLOCK
    grid_spec = pltpu.PrefetchScalarGridSpec(
        num_scalar_prefetch=1,
        grid=(batch, N_PAIRS, nb),
        in_specs=[pl.BlockSpec((None, MOBA_BLOCK, LANES), lambda b, p, i, s: (b, i, p)),
                  pl.BlockSpec((None, seq, LANES), lambda b, p, i, s: (b, 0, p)),
                  pl.BlockSpec((None, seq, LANES), lambda b, p, i, s: (b, 0, N_PAIRS + p)),
                  pl.BlockSpec((None, seq, LANES), lambda b, p, i, s: (b, 0, 2 * N_PAIRS + p)),
                  pl.BlockSpec((1, LANES), lambda b, p, i, s: (0, p))],
        out_specs=pl.BlockSpec((None, MOBA_BLOCK, LANES), lambda b, p, i, s: (b, i, p)),
        scratch_shapes=[pltpu.VMEM((2, seq, LANES), BF16),
                        pltpu.VMEM((seq, LANES), BF16),
                        pltpu.VMEM((2, LANES, LANES), F32),
                        pltpu.VMEM((2, seq, LANES), F32)],
    )
    return pl.pallas_call(
        functools.partial(_moba_kernel, seq=seq),
        grid_spec=grid_spec,
        out_shape=jax.ShapeDtypeStruct((batch, seq, WIDTH), F32),
        compiler_params=_cparams(("parallel", "parallel", "arbitrary")),
        name="moba",
    )(slopes, z_att, z_att, z_att, z_att, gain)


def _dot2x(a, b_exact):
    ah, al = _split(a)
    return (jnp.dot(ah, b_exact, preferred_element_type=F32)
            + jnp.dot(al, b_exact, preferred_element_type=F32))


def _head_block_diag(scale):
    r = lax.shift_right_logical(lax.broadcasted_iota(I32, (LANES, LANES), 0), 6)
    c = lax.shift_right_logical(lax.broadcasted_iota(I32, (LANES, LANES), 1), 6)
    return jnp.where(r == c, scale, 0.0).astype(BF16)


def _softplus(x):
    return jnp.maximum(x, 0.0) + jnp.log(1.0 + jnp.exp(-jnp.abs(x)))


def _rwkv_chunks(items):
    c_ = CHUNK
    tri = jnp.where(lax.broadcasted_iota(I32, (c_, c_), 0) >= lax.broadcasted_iota(I32, (c_, c_), 1),
                    1.0, 0.0).astype(BF16)
    lane = lax.broadcasted_iota(I32, (c_, LANES), 1)
    m0 = lane < HEAD_DIM
    t_idx = lax.broadcasted_iota(I32, (c_, LANES), 0)
    s_idx = lane & (HEAD_DIM - 1)
    strict = t_idx > s_idx
    eye2 = jnp.where(t_idx == s_idx, 1.0, 0.0)
    keep = jnp.concatenate([strict, t_idx >= s_idx], axis=0)

    def stack(x):
        return jnp.concatenate([jnp.where(m0, x, 0.0), jnp.where(m0, 0.0, x)], axis=0)

    splits = [_split(it[0]) for it in items]
    cs = [jnp.dot(tri, hi, preferred_element_type=F32) + jnp.dot(tri, lo, preferred_element_type=F32)
          for hi, lo in splits]
    base = []
    for (lw, r, k2, v, a, b), c in zip(items, cs):
        e_pos = jnp.exp(c)
        e_neg = jnp.exp(-c)
        at = a * jnp.exp(c - lw)
        bt = b * e_neg
        kt = k2 * e_neg
        rt = r * e_pos
        gc = e_pos[c_ - 1:c_, :]
        base.append(dict(at=at, rt=rt, gc=gc, bp=bt * gc, kp=kt * gc, vs=stack(v).astype(BF16),
                         ar=jnp.concatenate([at, rt], axis=0).astype(BF16),
                         sb=stack(bt).astype(BF16), sk=stack(kt).astype(BF16)))
    gb = [jnp.where(keep, lax.dot_general(d['ar'], d['sb'], NT_DIMS, preferred_element_type=F32), 0.0)
          for d in base]
    gk = [jnp.where(keep, lax.dot_general(d['ar'], d['sk'], NT_DIMS, preferred_element_type=F32), 0.0)
          for d in base]
    ls = [g[0:c_] for g in gb]
    ts = [eye2 + l for l in ls]
    lps = [_dot1(l, stack(l)) for l in ls]
    for _ in range(4):
        xs = [_dot1(jnp.concatenate([t, lp], axis=0), stack(lp)) for t, lp in zip(ts, lps)]
        ts = [t + x[0:c_] for t, x in zip(ts, xs)]
        lps = [x[c_:2 * c_] for x in xs]
    ts = [t + _dot1(t, stack(lp)) for t, lp in zip(ts, lps)]
    outs = []
    for d, g_b, g_k, t in zip(base, gb, gk, ts):
        pyk = jnp.dot(g_k.astype(BF16), d['vs'], preferred_element_type=F32)
        grow = jnp.broadcast_to(d['gc'], (c_, LANES))
        outs.append((d['at'].astype(BF16), pyk[0:c_], t.astype(BF16), d['bp'].astype(BF16),
                     d['kp'].astype(BF16), grow, d['rt'].astype(BF16), g_b[c_:2 * c_].astype(BF16),
                     pyk[c_:2 * c_]))
    return outs


def _rwkv_pre_kernel(*refs, has_vres, tb):
    if has_vres:
        (rkv_ref, lora_ref, vf_ref, mu_rkv_ref, mu_lora_ref, vec_ref, wd_ref, wa_ref, wg_ref, wv_ref,
         at_ref, p_ref, tc_ref, bp_ref, kp_ref, v_ref, gr_ref, rt_ref, qc_ref, yk_ref, bonus_ref, g_ref,
         carry_ref, carryl_ref) = refs
    else:
        (rkv_ref, lora_ref, mu_rkv_ref, mu_lora_ref, vec_ref, wd_ref, wa_ref, wg_ref,
         at_ref, p_ref, tc_ref, bp_ref, kp_ref, v_ref, gr_ref, rt_ref, qc_ref, yk_ref, bonus_ref, g_ref,
         carry_ref, carryl_ref) = refs
    t_idx = pl.program_id(1)

    @pl.when(t_idx == 0)
    def _():
        carry_ref[...] = jnp.zeros_like(carry_ref)
        carryl_ref[...] = jnp.zeros_like(carryl_ref)

    def token_shift(z, cref, mu):
        first = lax.broadcasted_iota(I32, z.shape, 0) == 0
        prev = jnp.where(first, cref[0:1, :], pltpu.roll(z, 1, 0))
        cref[0:1, :] = z[tb - 1:tb, :]
        return z + (prev - z) * mu

    zs = token_shift(rkv_ref[...], carry_ref, mu_rkv_ref[...])
    zl = token_shift(lora_ref[...], carryl_ref, mu_lora_ref[...])
    l0 = zl[:, 0:LANES]
    w0, a0, v0 = vec_ref[0:1, :], vec_ref[1:2, :], vec_ref[2:3, :]
    k_k, k_a, r_k = vec_ref[3:4, :], vec_ref[4:5, :], vec_ref[5:6, :]
    dw = _dot1(jnp.tanh(l0), wd_ref[...])
    lw_all = -jnp.exp(-_softplus(-(w0 + dw)) - 0.5)
    a_lr = _sigmoid(a0 + _dot1(l0, wa_ref[...]))
    g_ref[...] = _dot1(_sigmoid(zl[:, LANES:2 * LANES]), wg_ref[...])
    r_all = zs[:, 0:WIDTH]
    k_all = zs[:, WIDTH:2 * WIDTH]
    v_all = zs[:, 2 * WIDTH:3 * WIDTH]
    if has_vres:
        mix = _sigmoid(v0 + _dot1(zl[:, 2 * LANES:3 * LANES], wv_ref[...]))
        v_all = v_all + (vf_ref[...] - v_all) * mix
    v_ref[...] = v_all
    bd = _head_block_diag(1.0)
    items, where = [], []
    for pp in range(N_PAIRS):
        sl = slice(pp * LANES, (pp + 1) * LANES)
        r, k, v = r_all[:, sl], k_all[:, sl], v_all[:, sl]
        alr = a_lr[:, sl]
        kk = k * k_k[:, sl]
        ss = _dot2x(kk * kk, bd)
        kk = kk * lax.rsqrt(jnp.maximum(ss, 1e-24))
        k2 = k * (1.0 + (alr - 1.0) * k_a[:, sl])
        bonus_ref[:, sl] = _dot2x(r * k2 * r_k[:, sl], bd) * v
        a = -kk
        b = kk * alr
        lw = lw_all[:, sl]
        for ci in range(tb // CHUNK):
            rs = slice(ci * CHUNK, (ci + 1) * CHUNK)
            items.append((lw[rs], r[rs], k2[rs], v[rs], a[rs], b[rs]))
            where.append((rs, sl))
    for (rs, sl), outs in zip(where, _rwkv_chunks(items)):
        for ref, val in zip((at_ref, p_ref, tc_ref, bp_ref, kp_ref, gr_ref, rt_ref, qc_ref, yk_ref), outs):
            ref[rs, sl] = val


def _rwkv_pre(z_rkv, z_lora, v_first, mu_rkv, mu_lora, vecs, wd, wa, wg, wv, batch, seq, tb=128):
    has_vres = v_first is not None
    big = lambda w: pl.BlockSpec((None, tb, w), lambda b, t: (b, t, 0))
    full = lambda shape: pl.BlockSpec(shape, lambda b, t: (0,) * len(shape))
    in_specs = [big(3 * WIDTH), big(LORA_COLS)] + ([big(WIDTH)] if has_vres else [])
    in_specs += [full((1, 3 * WIDTH)), full((1, LORA_COLS)), full((8, WIDTH)),
                 full((LANES, WIDTH)), full((LANES, WIDTH)), full((LANES, WIDTH))]
    args = [z_rkv, z_lora] + ([v_first] if has_vres else []) + [mu_rkv, mu_lora, vecs, wd, wa, wg]
    if has_vres:
        in_specs.append(full((LANES, WIDTH)))
        args.append(wv)
    out_dtypes = [BF16, F32, BF16, BF16, BF16, F32, F32, BF16, BF16, F32, F32, F32]
    return pl.pallas_call(
        functools.partial(_rwkv_pre_kernel, has_vres=has_vres, tb=tb),
        grid=(batch, seq // tb),
        in_specs=in_specs,
        out_specs=[big(WIDTH)] * len(out_dtypes),
        out_shape=[jax.ShapeDtypeStruct((batch, seq, WIDTH), dt) for dt in out_dtypes],
        scratch_shapes=[pltpu.VMEM((8, 3 * WIDTH), F32), pltpu.VMEM((8, LORA_COLS), F32)],
        compiler_params=_cparams(("parallel", "arbitrary")),
        name="rwkv_pre",
    )(*args)


def _rwkv_scan_kernel(at_ref, p_ref, tc_ref, bp_ref, kp_ref, v_ref, gr_ref, rt_ref, qc_ref, yk_ref,
                      bonus_ref, g_ref, lnx_ref, o_ref, st_ref, *, ts):
    t_idx = pl.program_id(1)

    @pl.when(t_idx == 0)
    def _():
        st_ref[...] = jnp.zeros_like(st_ref)

    lane = lax.broadcasted_iota(I32, (CHUNK, LANES), 1)
    m0 = lane < HEAD_DIM
    bdmask = (lax.shift_right_logical(lax.broadcasted_iota(I32, (LANES, LANES), 0), 6)
              == lax.shift_right_logical(lax.broadcasted_iota(I32, (LANES, LANES), 1), 6))
    ones_t = jnp.full((CHUNK, LANES), 1.0 / CHUNK, BF16)

    def stack(x):
        return jnp.concatenate([jnp.where(m0, x, 0.0), jnp.where(m0, 0.0, x)], axis=0)

    tn = functools.partial(lax.dot_general, dimension_numbers=TN_DIMS, preferred_element_type=F32)
    nseq = at_ref.shape[0]
    chains = [(bb, pp, slice(pp * LANES, (pp + 1) * LANES)) for bb in range(nseq) for pp in range(N_PAIRS)]
    chunks = [slice(ci * CHUNK, (ci + 1) * CHUNK) for ci in range(ts // CHUNK)]

    gcols, kvs = {}, {}
    for ci, rs in enumerate(chunks):
        for bb, pp, sl in chains:
            gh, gl = _split(gr_ref[bb, rs, sl])
            gcols[ci, bb, pp] = tn(gh, ones_t) + tn(gl, ones_t)
            kvs[ci, bb, pp] = jnp.where(bdmask, _dot1(kp_ref[bb, rs, sl], v_ref[bb, rs, sl], TN_DIMS), 0.0)

    states = [st_ref[bb, pp] for bb, pp, _ in chains]
    for ci, rs in enumerate(chunks):
        sts = [st.astype(BF16) for st in states]
        xs = [_dot1(at_ref[bb, rs, sl], sts[n]) + p_ref[bb, rs, sl] for n, (bb, pp, sl) in enumerate(chains)]
        us = [_dot1(tc_ref[bb, rs, sl], stack(xs[n])).astype(BF16) for n, (bb, pp, sl) in enumerate(chains)]
        new = [gcols[ci, bb, pp] * states[n]
               + (jnp.where(bdmask, _dot1(bp_ref[bb, rs, sl], us[n], TN_DIMS), 0.0) + kvs[ci, bb, pp])
               for n, (bb, pp, sl) in enumerate(chains)]
        for n, (bb, pp, sl) in enumerate(chains):
            lhs = jnp.concatenate([rt_ref[bb, rs, sl], qc_ref[bb, rs, sl]], axis=1)
            rhs = jnp.concatenate([sts[n], stack(us[n])], axis=0)
            o_ref[bb, rs, sl] = jnp.dot(lhs, rhs, preferred_element_type=F32) + yk_ref[bb, rs, sl]
        states = new
    for n, (bb, pp, _) in enumerate(chains):
        st_ref[bb, pp] = states[n]

    bd = _head_block_diag(1.0 / HEAD_DIM)
    for bb, pp, sl in chains:
        y = o_ref[bb, :, sl]
        mu = _dot2x(y, bd)
        d = y - mu
        var = _dot2x(d * d, bd)
        yn = d * lax.rsqrt(var + RWKV_GN_EPS) * lnx_ref[0:1, sl] + lnx_ref[1:2, sl]
        o_ref[bb, :, sl] = (yn + bonus_ref[bb, :, sl]) * g_ref[bb, :, sl]


def _rwkv_scan(pre, lnx, batch, seq, ts=128):
    nseq = next(c for c in (4, 2, 1) if batch % c == 0)
    big = pl.BlockSpec((nseq, ts, WIDTH), lambda b, t: (b, t, 0))
    return pl.pallas_call(
        functools.partial(_rwkv_scan_kernel, ts=ts),
        grid=(batch // nseq, seq // ts),
        in_specs=[big] * 12 + [pl.BlockSpec((8, WIDTH), lambda b, t: (0, 0))],
        out_specs=big,
        out_shape=jax.ShapeDtypeStruct((batch, seq, WIDTH), F32),
        scratch_shapes=[pltpu.VMEM((nseq, N_PAIRS, LANES, LANES), F32)],
        compiler_params=_cparams(("parallel", "arbitrary")),
        name="rwkv_scan",
    )(*pre, lnx)


def _outproj_kernel(x_ref, oa_ref, orw_ref, w_ref, ln_ref, o_ref):
    y = (jnp.dot(oa_ref[...].astype(BF16), w_ref[0:WIDTH, :], preferred_element_type=F32)
         + jnp.dot(orw_ref[...].astype(BF16), w_ref[WIDTH:2 * WIDTH, :], preferred_element_type=F32))
    o_ref[...] = _layer_norm(ALPHA * x_ref[...] + y, ln_ref[0:1, :], ln_ref[1:2, :])


def _outproj(x2d, o_att, o_rwkv, w_out, ln, tm=512):
    n = x2d.shape[0]
    return pl.pallas_call(
        _outproj_kernel,
        grid=(n // tm,),
        in_specs=[pl.BlockSpec((tm, D_MODEL), lambda i: (i, 0)),
                  pl.BlockSpec((tm, WIDTH), lambda i: (i, 0)),
                  pl.BlockSpec((tm, WIDTH), lambda i: (i, 0)),
                  pl.BlockSpec((2 * WIDTH, D_MODEL), lambda i: (0, 0)),
                  pl.BlockSpec((8, D_MODEL), lambda i: (0, 0))],
        out_specs=pl.BlockSpec((tm, D_MODEL), lambda i: (i, 0)),
        out_shape=jax.ShapeDtypeStruct((n, D_MODEL), F32),
        compiler_params=_cparams(("parallel",)),
        name="outproj_ln",
    )(x2d, o_att, o_rwkv, w_out, ln)


def _kvproj_kernel(m_ref, w_ref, o_ref):
    o_ref[...] = jnp.dot(m_ref[...].astype(BF16), w_ref[...], preferred_element_type=F32).astype(BF16)


def _kvproj(mem2d, w_kv, tm=256):
    n = mem2d.shape[0]
    return pl.pallas_call(
        _kvproj_kernel,
        grid=(n // tm,),
        in_specs=[pl.BlockSpec((tm, D_MODEL), lambda i: (i, 0)),
                  pl.BlockSpec((D_MODEL, 2 * D_MODEL), lambda i: (0, 0))],
        out_specs=pl.BlockSpec((tm, 2 * D_MODEL), lambda i: (i, 0)),
        out_shape=jax.ShapeDtypeStruct((n, 2 * D_MODEL), BF16),
        compiler_params=_cparams(("parallel",)),
        name="kvproj",
    )(mem2d, w_kv)


def _route_lanes(logits, b_router):
    lane = lax.broadcasted_iota(I32, logits.shape, 1)
    lane_f = lane.astype(F32)
    valid = lane < N_EXPERTS
    s = _sigmoid(logits)
    sel = jnp.where(valid, s + b_router, NEG)

    def partner(x, bit):
        return jnp.where((lane & bit) == 0, pltpu.roll(x, LANES - bit, 1), pltpu.roll(x, bit, 1))

    p1 = partner(sel, 1)
    hi1, lo1 = jnp.maximum(sel, p1), jnp.minimum(sel, p1)
    hi2, lo2 = partner(hi1, 2), partner(lo1, 2)
    gsum = jnp.maximum(hi1, hi2) + jnp.maximum(jnp.minimum(hi1, hi2), jnp.maximum(lo1, lo2))
    gmax = jnp.max(gsum, -1, keepdims=True)
    grp_f = lax.shift_right_logical(lane, 2).astype(F32)
    g_best = jnp.min(jnp.where((gsum == gmax) & valid, grp_f, 1e9), -1, keepdims=True)
    in_grp = (grp_f == g_best) & valid
    masked = jnp.where(in_grp, sel, NEG)
    v1 = jnp.max(masked, -1, keepdims=True)
    e1 = jnp.min(jnp.where(in_grp & (masked == v1), lane_f, 1e9), -1, keepdims=True)
    rest = in_grp & (lane_f != e1)
    masked2 = jnp.where(rest, sel, NEG)
    v2 = jnp.max(masked2, -1, keepdims=True)
    e2 = jnp.min(jnp.where(rest & (masked2 == v2), lane_f, 1e9), -1, keepdims=True)
    w1 = jnp.sum(jnp.where(lane_f == e1, s, 0.0), -1, keepdims=True)
    w2 = jnp.sum(jnp.where(lane_f == e2, s, 0.0), -1, keepdims=True)
    tot = w1 + w2
    out = jnp.where(lane == 0, e1, jnp.where(lane == 1, e2, jnp.where(lane == 2, w1 / tot, w2 / tot)))
    return jnp.where(lane < 4, out, 0.0)


def _memattn_kernel(x_ref, kv_ref, wq_ref, wo_ref, ln_ref, wr_ref, br_ref, o_ref, route_ref):
    x = x_ref[...]
    q = jnp.dot(x.astype(BF16), wq_ref[...], preferred_element_type=F32) * (MEM_HEAD_DIM ** -0.5)
    heads = []
    for h in range(N_MEM_HEADS):
        sl = slice(h * MEM_HEAD_DIM, (h + 1) * MEM_HEAD_DIM)
        kh = kv_ref[:, sl]
        vh = kv_ref[:, D_MODEL + h * MEM_HEAD_DIM:D_MODEL + (h + 1) * MEM_HEAD_DIM]
        s = lax.dot_general(q[:, sl].astype(BF16), kh, NT_DIMS, preferred_element_type=F32)
        p = jnp.exp(s - jnp.max(s, -1, keepdims=True))
        o = jnp.dot(p.astype(BF16), vh, preferred_element_type=F32)
        heads.append(o / jnp.sum(p, -1, keepdims=True))
    o = jnp.concatenate(heads, axis=-1).astype(BF16)
    y = jnp.dot(o, wo_ref[...], preferred_element_type=F32)
    x2 = _layer_norm(ALPHA * x + y, ln_ref[0:1, :], ln_ref[1:2, :])
    o_ref[...] = x2
    logits = _dot1(x2, wr_ref[...])
    route_ref[...] = _route_lanes(logits, br_ref[0:1, :])


def _memattn(x3d, kv3d, wq, wo, ln, w_router_pad, b_router_pad, tm=512):
    batch, seq, _ = x3d.shape
    mlen = kv3d.shape[1]
    full = lambda shape: pl.BlockSpec(shape, lambda b, t: (0,) * len(shape))
    return pl.pallas_call(
        _memattn_kernel,
        grid=(batch, seq // tm),
        in_specs=[pl.BlockSpec((None, tm, D_MODEL), lambda b, t: (b, t, 0)),
                  pl.BlockSpec((None, mlen, 2 * D_MODEL), lambda b, t: (b, 0, 0)),
                  full((D_MODEL, D_MODEL)), full((D_MODEL, D_MODEL)), full((8, D_MODEL)),
                  full((D_MODEL, LANES)), full((8, LANES))],
        out_specs=[pl.BlockSpec((None, tm, D_MODEL), lambda b, t: (b, t, 0)),
                   pl.BlockSpec((None, tm, LANES), lambda b, t: (b, t, 0))],
        out_shape=[jax.ShapeDtypeStruct((batch, seq, D_MODEL), F32),
                   jax.ShapeDtypeStruct((batch, seq, LANES), F32)],
        compiler_params=_cparams(("parallel", "parallel")),
        name="memattn_ln",
    )(x3d, kv3d, wq, wo, ln, w_router_pad, b_router_pad)


def _plan_kernel(route_ref, dest_ref, stat_ref, acc_ref):
    phase = pl.program_id(0)
    i = pl.program_id(1)
    tm = route_ref.shape[0]
    lane = lax.broadcasted_iota(I32, (tm, LANES), 1).astype(F32)
    e0 = route_ref[:, 0:1]
    e1 = route_ref[:, 1:2]
    hot = jnp.where((lane == e0) | (lane == e1), 1.0, 0.0)

    @pl.when((phase == 0) & (i == 0))
    def _():
        acc_ref[...] = jnp.zeros_like(acc_ref)

    @pl.when(phase == 0)
    def _():
        acc_ref[0:1, :] = acc_ref[0:1, :] + jnp.sum(hot, axis=0, keepdims=True)

    @pl.when((phase == 1) & (i == 0))
    def _():
        r = lax.broadcasted_iota(I32, (LANES, LANES), 0)
        c = lax.broadcasted_iota(I32, (LANES, LANES), 1)
        before = jnp.where(r < c, 1.0, 0.0)
        cnt8 = jnp.broadcast_to(acc_ref[0:1, :], (8, LANES))
        acc_ref[1:2, :] = _dot2x(cnt8, before.astype(BF16))[0:1, :]

    @pl.when(phase == 1)
    def _():
        r = lax.broadcasted_iota(I32, (tm, tm), 0)
        c = lax.broadcasted_iota(I32, (tm, tm), 1)
        earlier = jnp.where(c < r, 1.0, 0.0).astype(BF16)
        pos = (jnp.dot(earlier, hot.astype(BF16), preferred_element_type=F32)
               + acc_ref[1:2, :] + acc_ref[2:3, :])
        d0 = jnp.sum(jnp.where(lane == e0, pos, 0.0), -1, keepdims=True)
        d1 = jnp.sum(jnp.where(lane == e1, pos, 0.0), -1, keepdims=True)
        dest_ref[...] = jnp.where(lane == 0.0, d0, jnp.where(lane == 1.0, d1, 0.0)).astype(I32)
        acc_ref[2:3, :] = acc_ref[2:3, :] + jnp.sum(hot, axis=0, keepdims=True)
        stat_ref[...] = acc_ref[...]


def _plan(route, tm=1024):
    n = route.shape[0]
    tm = min(tm, n)
    return pl.pallas_call(
        _plan_kernel,
        grid=(2, n // tm),
        in_specs=[pl.BlockSpec((tm, LANES), lambda p, i: (i, 0))],
        out_specs=[pl.BlockSpec((tm, LANES), lambda p, i: (i * p, 0)),
                   pl.BlockSpec((8, LANES), lambda p, i: (0, 0))],
        out_shape=[jax.ShapeDtypeStruct((n, LANES), I32), jax.ShapeDtypeStruct((8, LANES), F32)],
        scratch_shapes=[pltpu.VMEM((8, LANES), F32)],
        compiler_params=_cparams(("arbitrary", "arbitrary")),
        name="moe_plan",
    )(route)


def _dispatch_kernel(dest_ref, prev_ref, x_ref, xs_hbm, stage_ref, sems):
    i = pl.program_id(0)
    tm = dest_ref.shape[2] // 2
    stage_ref[i % 2] = x_ref[...]

    def copy(step, dref, t, slot):
        src = stage_ref.at[step % 2, pl.ds(t, 1)]
        return pltpu.make_async_copy(src, xs_hbm.at[pl.ds(dref[0, 0, slot * tm + t], 1)], sems.at[step % 2])

    def start(t, c):
        copy(i, dest_ref, t, 0).start()
        copy(i, dest_ref, t, 1).start()
        return c

    def wait_step(step, dref):
        def wait(t, c):
            copy(step, dref, t, 0).wait()
            copy(step, dref, t, 1).wait()
            return c
        lax.fori_loop(0, tm, wait, 0, unroll=8)

    lax.fori_loop(0, tm, start, 0, unroll=8)

    @pl.when(i > 0)
    def _():
        wait_step(i - 1, prev_ref)

    @pl.when(i == pl.num_programs(0) - 1)
    def _():
        wait_step(i, dest_ref)


def _dispatch(x2d, dest3, n_rows):
    n_steps = dest3.shape[0]
    tm = dest3.shape[2] // 2
    blk = (1, 1, 2 * tm)
    return pl.pallas_call(
        _dispatch_kernel,
        grid=(n_steps,),
        in_specs=[pl.BlockSpec(blk, lambda i: (i, 0, 0), memory_space=pltpu.SMEM),
                  pl.BlockSpec(blk, lambda i: (jnp.maximum(i - 1, 0), 0, 0), memory_space=pltpu.SMEM),
                  pl.BlockSpec((tm, D_MODEL), lambda i: (i, 0))],
        out_specs=pl.BlockSpec(memory_space=pl.ANY),
        out_shape=jax.ShapeDtypeStruct((n_rows, D_MODEL), F32),
        scratch_shapes=[pltpu.VMEM((2, tm, D_MODEL), F32), pltpu.SemaphoreType.DMA((2,))],
        compiler_params=_cparams(("arbitrary",)),
        name="moe_dispatch",
    )(dest3, dest3, x2d)


def _experts_kernel(rb_ref, ex_ref, nit_ref, off_ref, cnt_ref, x_ref, wg_ref, wu_ref, wd_ref, o_ref,
                    wgb_ref, wub_ref, wdb_ref):
    w = pl.program_id(0)
    rows = x_ref.shape[0]

    @pl.when(w < nit_ref[0])
    def _():
        e = ex_ref[w]
        rb = rb_ref[w]
        prev = jnp.maximum(w - 1, 0)

        @pl.when((w == 0) | (ex_ref[prev] != e))
        def _():
            wgb_ref[...] = wg_ref[...].astype(BF16)
            wub_ref[...] = wu_ref[...].astype(BF16)
            wdb_ref[...] = wd_ref[...].astype(BF16)

        xb = x_ref[...].astype(BF16)
        hg = jnp.dot(xb, wgb_ref[...], preferred_element_type=F32)
        hu = jnp.dot(xb, wub_ref[...], preferred_element_type=F32)
        h = hg * _sigmoid(hg) * hu
        y = jnp.dot(h.astype(BF16), wdb_ref[...], preferred_element_type=F32)
        row = rb * rows + lax.broadcasted_iota(I32, (rows, 1), 0)
        mine = (row >= off_ref[e]) & (row < off_ref[e] + cnt_ref[e])
        first = (w == 0) | (rb_ref[prev] != rb)

        @pl.when(first)
        def _():
            o_ref[...] = jnp.where(mine, y, 0.0)

        @pl.when(jnp.logical_not(first))
        def _():
            o_ref[...] = jnp.where(mine, y, o_ref[...])


def _experts(xs, rb, ex, n_items, off, cnt, w_gate, w_up, w_down, layer, rows):
    n_work = rb.shape[0]
    item = lambda w, nit: jnp.minimum(w, nit[0] - 1)
    row_map = lambda w, rb, ex, nit, off, cnt: (rb[item(w, nit)], 0)
    w_map = lambda w, rb, ex, nit, off, cnt: (layer, ex[item(w, nit)], 0, 0)
    grid_spec = pltpu.PrefetchScalarGridSpec(
        num_scalar_prefetch=5,
        grid=(n_work,),
        in_specs=[pl.BlockSpec((rows, D_MODEL), row_map),
                  pl.BlockSpec((None, None, D_MODEL, D_EXPERT), w_map),
                  pl.BlockSpec((None, None, D_MODEL, D_EXPERT), w_map),
                  pl.BlockSpec((None, None, D_EXPERT, D_MODEL), w_map)],
        out_specs=pl.BlockSpec((rows, D_MODEL), row_map),
        scratch_shapes=[pltpu.VMEM((D_MODEL, D_EXPERT), BF16), pltpu.VMEM((D_MODEL, D_EXPERT), BF16),
                        pltpu.VMEM((D_EXPERT, D_MODEL), BF16)],
    )
    return pl.pallas_call(
        _experts_kernel,
        grid_spec=grid_spec,
        out_shape=jax.ShapeDtypeStruct(xs.shape, F32),
        compiler_params=_cparams(("arbitrary",)),
        name="moe_experts",
    )(rb, ex, n_items, off, cnt, xs, w_gate, w_up, w_down)


def _combine_kernel(dest_ref, next_ref, x_ref, gate_ref, ln_ref, y_hbm, o_ref, buf_ref, sems):
    i = pl.program_id(0)
    tm = x_ref.shape[0]

    def copy(step, dref, slot, t):
        src = y_hbm.at[pl.ds(dref[0, 0, slot * tm + t], 1)]
        return pltpu.make_async_copy(src, buf_ref.at[step % 2, slot, pl.ds(t, 1)], sems.at[step % 2])

    def start_step(step, dref):
        def start(t, c):
            copy(step, dref, 0, t).start()
            copy(step, dref, 1, t).start()
            return c
        lax.fori_loop(0, tm, start, 0, unroll=8)

    @pl.when(i == 0)
    def _():
        start_step(0, dest_ref)

    @pl.when(i + 1 < pl.num_programs(0))
    def _():
        start_step(i + 1, next_ref)

    def wait(t, c):
        copy(i, dest_ref, 0, t).wait()
        copy(i, dest_ref, 1, t).wait()
        return c

    lax.fori_loop(0, tm, wait, 0, unroll=8)
    cur = i % 2
    y = gate_ref[:, 0:1] * buf_ref[cur, 0] + gate_ref[:, 1:2] * buf_ref[cur, 1]
    o_ref[...] = _layer_norm(ALPHA * x_ref[...] + y, ln_ref[0:1, :], ln_ref[1:2, :])


def _combine(x2d, gates, dest3, ys, ln, tm=256):
    n = x2d.shape[0]
    n_steps = n // tm
    blk = (1, 1, 2 * tm)
    return pl.pallas_call(
        _combine_kernel,
        grid=(n_steps,),
        in_specs=[pl.BlockSpec(blk, lambda i: (i, 0, 0), memory_space=pltpu.SMEM),
                  pl.BlockSpec(blk, lambda i: (jnp.minimum(i + 1, n_steps - 1), 0, 0), memory_space=pltpu.SMEM),
                  pl.BlockSpec((tm, D_MODEL), lambda i: (i, 0)),
                  pl.BlockSpec((tm, 2), lambda i: (i, 0)),
                  pl.BlockSpec((8, D_MODEL), lambda i: (0, 0)),
                  pl.BlockSpec(memory_space=pl.ANY)],
        out_specs=pl.BlockSpec((tm, D_MODEL), lambda i: (i, 0)),
        out_shape=jax.ShapeDtypeStruct((n, D_MODEL), F32),
        scratch_shapes=[pltpu.VMEM((2, 2, tm, D_MODEL), F32), pltpu.SemaphoreType.DMA((2,))],
        compiler_params=_cparams(("arbitrary",)),
        name="moe_combine_ln",
    )(dest3, dest3, x2d, gates, ln, ys)


def _work_items(counts, off, n_rows, rows):
    n_work = n_rows // rows + N_EXPERTS
    first = off // rows
    last = (off + counts - 1) // rows
    per_expert = jnp.where(counts > 0, last - first + 1, 0)
    item_end = jnp.cumsum(per_expert)
    item_start = item_end - per_expert
    w = jnp.arange(n_work, dtype=I32)
    ex = jnp.minimum(jnp.sum((item_end[None, :] <= w[:, None]).astype(I32), axis=1), N_EXPERTS - 1)
    rb = jnp.clip(first[ex] + (w - item_start[ex]), 0, n_rows // rows - 1).astype(I32)
    return rb, ex, item_end[-1:].astype(I32)


def _moe(x2, route, w_gate, w_up, w_down, layer, ln, rows=MOE_ROWS, tm=256):
    n = x2.shape[0]
    n_rows = 2 * n
    gates = route[:, 2:4]
    dest_tile, stat = _plan(route)
    counts = stat[0, :N_EXPERTS].astype(I32)
    off = stat[1, :N_EXPERTS].astype(I32)
    rb, ex, n_items = _work_items(counts, off, n_rows, rows)
    dest3 = dest_tile[:, 0:2].reshape(n // tm, tm, 2).transpose(0, 2, 1).reshape(n // tm, 1, 2 * tm)
    xs = _dispatch(x2, dest3, n_rows)
    ys = _experts(xs, rb, ex, n_items, off, counts, w_gate, w_up, w_down, layer, rows)
    return _combine(x2, gates, dest3, ys, ln, tm)


def _pad_rows(w, rows_before, total):
    return jnp.pad(w, ((rows_before, total - rows_before - w.shape[0]), (0, 0)))


def kernel(x, mem, w_in, w_in_vres, mu_shift, mu_vres, w0, w_decay_up, a0, w_iclr_up, v0, w_vres_up,
           w_gate_up, k_k, k_a, r_k, lnx_g, lnx_b, attn_out_g, w_out, w_mem_q, w_mem_kv, w_mem_o,
           w_router, b_router, w_exp_gate, w_exp_up, w_exp_down, ln_g, ln_b):
    batch, seq, d = x.shape
    n = batch * seq
    mlen = mem.shape[1]
    zeros = jnp.zeros
    slopes = 2.0 ** (-8.0 * jnp.arange(1, N_HEADS + 1, dtype=F32) / N_HEADS)
    w_router_pad = jnp.pad(w_router, ((0, 0), (0, LANES - N_EXPERTS)))
    b_router_pad = jnp.pad(b_router.reshape(1, N_EXPERTS), ((0, 7), (0, LANES - N_EXPERTS)))
    mem2d = mem.reshape(batch * mlen, d)
    n_shift = w_in.shape[2] - 3 * WIDTH
    n_lora = n_shift - 3 * WIDTH
    n_vres = w_in_vres.shape[2]
    x2d = x.reshape(n, d)
    v_first = None
    for l in range(DEPTH):
        has_vres = l > 0
        w_extra = w_in_vres[l - 1] if has_vres else zeros((d, n_vres), F32)
        w_pad = jnp.concatenate(
            [w_in[l], w_extra, zeros((d, IN_COLS_PAD - w_in.shape[2] - n_vres), F32)], axis=1).astype(BF16)
        z_att, z_rkv, z_lora = _inproj(x2d, w_pad)
        o_att = _moba(z_att.reshape(batch, seq, 3 * WIDTH), slopes, attn_out_g[l].reshape(1, WIDTH), batch, seq)

        mu_rkv = mu_shift[l, :3 * WIDTH].reshape(1, 3 * WIDTH)
        mu_extra = mu_vres[l - 1] if has_vres else zeros((n_vres,), F32)
        mu_lora = jnp.concatenate(
            [mu_shift[l, 3 * WIDTH:], mu_extra, zeros((LORA_COLS - n_lora - n_vres,), F32)]).reshape(1, LORA_COLS)
        v0_l = v0[l - 1] if has_vres else zeros((WIDTH,), F32)
        vecs = jnp.stack([w0[l], a0[l], v0_l, k_k[l], k_a[l], r_k[l].reshape(WIDTH),
                          zeros((WIDTH,), F32), zeros((WIDTH,), F32)])
        n_dec, n_iclr = w_decay_up.shape[1], w_iclr_up.shape[1]
        wd = _pad_rows(w_decay_up[l], 0, LANES).astype(BF16)
        wa = _pad_rows(w_iclr_up[l], n_dec, LANES).astype(BF16)
        wg = w_gate_up[l].astype(BF16)
        wv = _pad_rows(w_vres_up[l - 1], 0, LANES).astype(BF16) if has_vres else None
        pre = _rwkv_pre(z_rkv.reshape(batch, seq, 3 * WIDTH), z_lora.reshape(batch, seq, LORA_COLS),
                        v_first, mu_rkv, mu_lora, vecs, wd, wa, wg, wv, batch, seq)
        if not has_vres:
            v_first = pre[5]
        lnx = jnp.concatenate([lnx_g[l][None], lnx_b[l][None], zeros((6, WIDTH), F32)])
        o_rwkv = _rwkv_scan(pre, lnx, batch, seq)

        ln = lambda j: jnp.concatenate([ln_g[l, j][None], ln_b[l, j][None], zeros((6, d), F32)])
        x1 = _outproj(x2d, o_att.reshape(n, WIDTH), o_rwkv.reshape(n, WIDTH), w_out[l].astype(BF16), ln(0))

        kv = _kvproj(mem2d, w_mem_kv[l].astype(BF16)).reshape(batch, mlen, 2 * d)
        x2, route = _memattn(x1.reshape(batch, seq, d), kv, w_mem_q[l].astype(BF16),
                             w_mem_o[l].astype(BF16), ln(1), w_router_pad, b_router_pad)
        x2d = _moe(x2.reshape(n, d), route.reshape(n, LANES), w_exp_gate, w_exp_up, w_exp_down, l, ln(2))
    return x2d.reshape(batch, seq, d)
```

```python
import functools

import jax
import jax.numpy as jnp
from jax import lax
from jax.experimental import pallas as pl
from jax.experimental.pallas import tpu as pltpu

F32 = jnp.float32
BF16 = jnp.bfloat16
I32 = jnp.int32

D_MODEL = 1024
DEPTH = 2
N_HEADS = 8
HEAD_DIM = 64
WIDTH = N_HEADS * HEAD_DIM
N_PAIRS = N_HEADS // 2
LANES = 128
MOBA_BLOCK = 256
MOBA_TOP = 3
GATE_ROWS = 16
POS_BLK = 16
POS_IN = 17
LORA_COLS = 384
IN_COLS_PAD = 3 * WIDTH + 3 * WIDTH + LORA_COLS
RWKV_GN_EPS = 64e-5
N_MEM_HEADS = 4
MEM_HEAD_DIM = D_MODEL // N_MEM_HEADS
N_EXPERTS = 32
N_GROUPS = 8
D_EXPERT = 512
LN_EPS = 1e-5
RMS_EPS = 1e-6
NEG = -1e30
ALPHA = (2 * DEPTH) ** 0.25
CHUNK = 64
MOE_ROWS = 256

VMEM_LIMIT = 48 * 1024 * 1024

NT_DIMS = (((1,), (1,)), ((), ()))
TN_DIMS = (((0,), (0,)), ((), ()))


def _cparams(sem):
    return pltpu.CompilerParams(dimension_semantics=sem, vmem_limit_bytes=VMEM_LIMIT)


def _split(a):
    hi = a.astype(BF16)
    lo = (a - hi.astype(F32)).astype(BF16)
    return hi, lo


def _dot1(a, b, dims=(((1,), (0,)), ((), ()))):
    return lax.dot_general(a.astype(BF16), b.astype(BF16), dims, preferred_element_type=F32)


def _layer_norm(y, g, b):
    mu = jnp.mean(y, -1, keepdims=True)
    d = y - mu
    var = jnp.mean(d * d, -1, keepdims=True)
    return d * lax.rsqrt(var + LN_EPS) * g + b


def _sigmoid(x):
    return 1.0 / (1.0 + jnp.exp(-x))


def _inproj_kernel(x_ref, w_ref, att_ref, rkv_ref, lora_ref):
    xb = x_ref[...].astype(BF16)
    att_ref[...] = jnp.dot(xb, w_ref[:, 0:3 * WIDTH], preferred_element_type=F32)
    rkv_ref[...] = jnp.dot(xb, w_ref[:, 3 * WIDTH:6 * WIDTH], preferred_element_type=F32)
    lora_ref[...] = jnp.dot(xb, w_ref[:, 6 * WIDTH:IN_COLS_PAD], preferred_element_type=F32)


def _inproj(x2d, w_pad, tm=256):
    n = x2d.shape[0]
    return pl.pallas_call(
        _inproj_kernel,
        grid=(n // tm,),
        in_specs=[pl.BlockSpec((tm, D_MODEL), lambda i: (i, 0)),
                  pl.BlockSpec((D_MODEL, IN_COLS_PAD), lambda i: (0, 0))],
        out_specs=[pl.BlockSpec((tm, 3 * WIDTH), lambda i: (i, 0)),
                   pl.BlockSpec((tm, 3 * WIDTH), lambda i: (i, 0)),
                   pl.BlockSpec((tm, LORA_COLS), lambda i: (i, 0))],
        out_shape=[jax.ShapeDtypeStruct((n, 3 * WIDTH), F32),
                   jax.ShapeDtypeStruct((n, 3 * WIDTH), F32),
                   jax.ShapeDtypeStruct((n, LORA_COLS), F32)],
        compiler_params=_cparams(("parallel",)),
        name="inproj",
    )(x2d, w_pad)


def _moba_kernel(slopes_ref, q_ref, qf_ref, k_ref, v_ref, gain_ref, aug_ref, o_ref, kaug_ref, vb_ref, kmp_ref,
                 sel_ref, *, seq):
    pair = pl.program_id(1)
    i = pl.program_id(2)
    nb = seq // MOBA_BLOCK
    blk = MOBA_BLOCK

    @pl.when(i == 0)
    def _():
        k = k_ref[...]
        lane_s = lax.broadcasted_iota(I32, (seq, LANES), 1)
        kmean = jnp.sum(k.reshape(nb, blk, LANES), axis=1) * (1.0 / blk)
        for hh in range(2):
            ob = HEAD_DIM * (1 - hh)
            inhead = (lane_s >= HEAD_DIM * hh) & (lane_s < HEAD_DIM * (hh + 1))
            kaug_ref[hh] = jnp.where(inhead, k.astype(BF16), aug_ref[hh])
            kmp_ref[hh] = jnp.zeros((LANES, LANES), F32)
            kmp_ref[hh, ob:ob + nb, :] = kmean
            qm = jnp.where(inhead, qf_ref[...] * (HEAD_DIM ** -0.5), 0.0)
            gate = _dot1(kmp_ref[hh], qm, NT_DIMS)[ob:ob + GATE_ROWS, :]
            jrow = lax.broadcasted_iota(I32, (GATE_ROWS, seq), 0)
            qblk = lax.shift_right_logical(lax.broadcasted_iota(I32, (GATE_ROWS, seq), 1), 8)
            cnt = jnp.zeros((GATE_ROWS, seq), I32)
            for jp in range(nb - 1):
                rowv = gate[jp:jp + 1, :]
                beats = ((rowv > gate) | ((rowv == gate) & (jp < jrow))) & (jp < qblk)
                cnt = cnt + jnp.where(beats, 1, 0)
            drop = jnp.where((jrow < qblk) & (cnt >= MOBA_TOP) & (qblk > MOBA_TOP), NEG, 0.0).astype(BF16)
            place = (lax.broadcasted_iota(I32, (GATE_ROWS, LANES), 1) - ob
                     == lax.broadcasted_iota(I32, (GATE_ROWS, LANES), 0))
            sel_ref[hh] = lax.dot_general(drop, jnp.where(place, 1.0, 0.0).astype(BF16), TN_DIMS,
                                          preferred_element_type=F32)
        vb_ref[...] = v_ref[...].astype(BF16)

    def query_block(ii):
        q = q_ref[...] * (HEAD_DIM ** -0.5)
        causal = (lax.broadcasted_iota(I32, (blk, blk), 0) >= lax.broadcasted_iota(I32, (blk, blk), 1))
        lane = lax.broadcasted_iota(I32, (blk, LANES), 1)
        gain = gain_ref[...]
        nk = (ii + 1) * blk
        inheads = [(lane >= HEAD_DIM * hh) & (lane < HEAD_DIM * (hh + 1)) for hh in range(2)]
        qas = []
        for hh in range(2):
            ob = HEAD_DIM * (1 - hh)
            inhead = inheads[hh]
            c = lane - ob
            qaug = jnp.where((c == POS_BLK) | (c == POS_IN), 1.0, 0.0)
            if ii > MOBA_TOP:
                qaug = qaug + sel_ref[hh, ii * blk:(ii + 1) * blk, :]
            qas.append(jnp.where(inhead, q, qaug).astype(BF16))

        scores = [lax.dot_general(qas[hh], kaug_ref[hh, 0:nk, :], NT_DIMS, preferred_element_type=F32)
                  for hh in range(2)]
        probs, sums = [], []
        for s in scores:
            parts = [s[:, j * blk:(j + 1) * blk] for j in range(ii)]
            parts.append(jnp.where(causal, s[:, ii * blk:nk], NEG))
            m = jnp.max(functools.reduce(jnp.maximum, parts), -1, keepdims=True)
            ps = [jnp.exp(x - m) for x in parts]
            sums.append(jnp.sum(functools.reduce(lambda a, b: a + b, ps), -1, keepdims=True))
            probs.append(jnp.concatenate([x.astype(BF16) for x in ps], axis=1))
        outs = []
        for hh in range(2):
            out = jnp.dot(probs[hh], vb_ref[0:nk, :], preferred_element_type=F32) / sums[hh]
            ms = jnp.sum(jnp.where(inheads[hh], out * out, 0.0), -1, keepdims=True) * (1.0 / HEAD_DIM)
            outs.append(out * lax.rsqrt(ms + RMS_EPS) * gain)
        o_ref[...] = jnp.where(lane < HEAD_DIM, outs[0], outs[1])

    for ii in range(nb):
        pl.when(i == ii)(functools.partial(query_block, ii))


def _moba(z_att, slopes, gain, batch, seq):
    nb = seq // MOBA_BLOCK
    grid_spec = pltpu.PrefetchScalarGridSpec(
        num_scalar_prefetch=1,
        grid=(batch, N_PAIRS, nb),
        in_specs=[pl.BlockSpec((None, MOBA_BLOCK, LANES), lambda b, p, i, s: (b, i, p)),
                  pl.BlockSpec((None, seq, LANES), lambda b, p, i, s: (b, 0, p)),
                  pl.BlockSpec((None, seq, LANES), lambda b, p, i, s: (b, 0, N_PAIRS + p)),
                  pl.BlockSpec((None, seq, LANES), lambda b, p, i, s: (b, 0, 2 * N_PAIRS + p)),
                  pl.BlockSpec((1, LANES), lambda b, p, i, s: (0, p)),
                  pl.BlockSpec((None, 2, seq, LANES), lambda b, p, i, s: (p, 0, 0, 0))],
        out_specs=pl.BlockSpec((None, MOBA_BLOCK, LANES), lambda b, p, i, s: (b, i, p)),
        scratch_shapes=[pltpu.VMEM((2, seq, LANES), BF16),
                        pltpu.VMEM((seq, LANES), BF16),
                        pltpu.VMEM((2, LANES, LANES), F32),
                        pltpu.VMEM((2, seq, LANES), F32)],
    )
    return pl.pallas_call(
        functools.partial(_moba_kernel, seq=seq),
        grid_spec=grid_spec,
        out_shape=jax.ShapeDtypeStruct((batch, seq, WIDTH), F32),
        compiler_params=_cparams(("parallel", "parallel", "arbitrary")),
        name="moba",
    )(slopes, z_att, z_att, z_att, z_att, gain, _moba_key_aug(slopes, seq))


def _moba_key_aug(slopes, seq):
    rowi = jnp.arange(seq, dtype=I32)[None, None, :, None]
    lane = jnp.arange(LANES, dtype=I32)[None, None, None, :]
    hh = jnp.arange(2, dtype=I32)[None, :, None, None]
    c = lane - HEAD_DIM * (1 - hh)
    slope = slopes.reshape(N_PAIRS, 2, 1, 1)
    rowblk = rowi // MOBA_BLOCK
    aug = jnp.where(c == rowblk, 1.0, 0.0)
    aug = jnp.where(c == POS_BLK, (slope * MOBA_BLOCK) * rowblk.astype(F32), aug)
    aug = jnp.where(c == POS_IN, slope * (rowi % MOBA_BLOCK).astype(F32), aug)
    return aug.astype(BF16)


def _dot2x(a, b_exact):
    ah, al = _split(a)
    return (jnp.dot(ah, b_exact, preferred_element_type=F32)
            + jnp.dot(al, b_exact, preferred_element_type=F32))


def _head_block_diag(scale):
    r = lax.shift_right_logical(lax.broadcasted_iota(I32, (LANES, LANES), 0), 6)
    c = lax.shift_right_logical(lax.broadcasted_iota(I32, (LANES, LANES), 1), 6)
    return jnp.where(r == c, scale, 0.0).astype(BF16)


def _softplus(x):
    return jnp.maximum(x, 0.0) + jnp.log(1.0 + jnp.exp(-jnp.abs(x)))


def _rwkv_chunks(items):
    c_ = CHUNK
    tri = jnp.where(lax.broadcasted_iota(I32, (c_, c_), 0) >= lax.broadcasted_iota(I32, (c_, c_), 1),
                    1.0, 0.0).astype(BF16)
    lane = lax.broadcasted_iota(I32, (c_, LANES), 1)
    m0 = lane < HEAD_DIM
    t_idx = lax.broadcasted_iota(I32, (c_, LANES), 0)
    s_idx = lane & (HEAD_DIM - 1)
    strict = t_idx > s_idx
    eye2 = jnp.where(t_idx == s_idx, 1.0, 0.0)
    keep = jnp.concatenate([strict, t_idx >= s_idx], axis=0)

    def stack(x):
        return jnp.concatenate([jnp.where(m0, x, 0.0), jnp.where(m0, 0.0, x)], axis=0)

    splits = [_split(it[0]) for it in items]
    cs = [jnp.dot(tri, hi, preferred_element_type=F32) + jnp.dot(tri, lo, preferred_element_type=F32)
          for hi, lo in splits]
    base = []
    for (lw, r, k2, v, a, b), c in zip(items, cs):
        e_pos = jnp.exp(c)
        e_neg = jnp.exp(-c)
        at = a * jnp.exp(c - lw)
        bt = b * e_neg
        kt = k2 * e_neg
        rt = r * e_pos
        gc = e_pos[c_ - 1:c_, :]
        base.append(dict(at=at, rt=rt, gc=gc, bp=bt * gc, kp=kt * gc, vs=stack(v).astype(BF16),
                         ar=jnp.concatenate([at, rt], axis=0).astype(BF16),
                         sb=stack(bt).astype(BF16), sk=stack(kt).astype(BF16)))
    gb = [jnp.where(keep, lax.dot_general(d['ar'], d['sb'], NT_DIMS, preferred_element_type=F32), 0.0)
          for d in base]
    gk = [jnp.where(keep, lax.dot_general(d['ar'], d['sk'], NT_DIMS, preferred_element_type=F32), 0.0)
          for d in base]
    ls = [g[0:c_] for g in gb]
    ts = [eye2 + l for l in ls]
    lps = [_dot1(l, stack(l)) for l in ls]
    for _ in range(4):
        xs = [_dot1(jnp.concatenate([t, lp], axis=0), stack(lp)) for t, lp in zip(ts, lps)]
        ts = [t + x[0:c_] for t, x in zip(ts, xs)]
        lps = [x[c_:2 * c_] for x in xs]
    ts = [t + _dot1(t, stack(lp)) for t, lp in zip(ts, lps)]
    outs = []
    for d, g_b, g_k, t in zip(base, gb, gk, ts):
        pyk = jnp.dot(g_k.astype(BF16), d['vs'], preferred_element_type=F32)
        grow = jnp.broadcast_to(d['gc'], (c_, LANES))
        outs.append((d['at'].astype(BF16), pyk[0:c_], t.astype(BF16), d['bp'].astype(BF16),
                     d['kp'].astype(BF16), grow, d['rt'].astype(BF16), g_b[c_:2 * c_].astype(BF16),
                     pyk[c_:2 * c_]))
    return outs


def _rwkv_pre_kernel(*refs, has_vres, tb):
    if has_vres:
        (rkv_ref, lora_ref, vf_ref, mu_rkv_ref, mu_lora_ref, vec_ref, wd_ref, wa_ref, wg_ref, wv_ref,
         at_ref, p_ref, tc_ref, bp_ref, kp_ref, v_ref, gr_ref, rt_ref, qc_ref, yk_ref, bonus_ref, g_ref,
         carry_ref, carryl_ref) = refs
    else:
        (rkv_ref, lora_ref, mu_rkv_ref, mu_lora_ref, vec_ref, wd_ref, wa_ref, wg_ref,
         at_ref, p_ref, tc_ref, bp_ref, kp_ref, v_ref, gr_ref, rt_ref, qc_ref, yk_ref, bonus_ref, g_ref,
         carry_ref, carryl_ref) = refs
    t_idx = pl.program_id(1)

    @pl.when(t_idx == 0)
    def _():
        carry_ref[...] = jnp.zeros_like(carry_ref)
        carryl_ref[...] = jnp.zeros_like(carryl_ref)

    def token_shift(z, cref, mu):
        first = lax.broadcasted_iota(I32, z.shape, 0) == 0
        prev = jnp.where(first, cref[0:1, :], pltpu.roll(z, 1, 0))
        cref[0:1, :] = z[tb - 1:tb, :]
        return z + (prev - z) * mu

    zs = token_shift(rkv_ref[...], carry_ref, mu_rkv_ref[...])
    zl = token_shift(lora_ref[...], carryl_ref, mu_lora_ref[...])
    l0 = zl[:, 0:LANES]
    w0, a0, v0 = vec_ref[0:1, :], vec_ref[1:2, :], vec_ref[2:3, :]
    k_k, k_a, r_k = vec_ref[3:4, :], vec_ref[4:5, :], vec_ref[5:6, :]
    dw = _dot1(jnp.tanh(l0), wd_ref[...])
    lw_all = -jnp.exp(-_softplus(-(w0 + dw)) - 0.5)
    a_lr = _sigmoid(a0 + _dot1(l0, wa_ref[...]))
    g_ref[...] = _dot1(_sigmoid(zl[:, LANES:2 * LANES]), wg_ref[...])
    r_all = zs[:, 0:WIDTH]
    k_all = zs[:, WIDTH:2 * WIDTH]
    v_all = zs[:, 2 * WIDTH:3 * WIDTH]
    if has_vres:
        mix = _sigmoid(v0 + _dot1(zl[:, 2 * LANES:3 * LANES], wv_ref[...]))
        v_all = v_all + (vf_ref[...] - v_all) * mix
    v_ref[...] = v_all
    bd = _head_block_diag(1.0)
    items, where = [], []
    for pp in range(N_PAIRS):
        sl = slice(pp * LANES, (pp + 1) * LANES)
        r, k, v = r_all[:, sl], k_all[:, sl], v_all[:, sl]
        alr = a_lr[:, sl]
        kk = k * k_k[:, sl]
        ss = _dot2x(kk * kk, bd)
        kk = kk * lax.rsqrt(jnp.maximum(ss, 1e-24))
        k2 = k * (1.0 + (alr - 1.0) * k_a[:, sl])
        bonus_ref[:, sl] = _dot2x(r * k2 * r_k[:, sl], bd) * v
        a = -kk
        b = kk * alr
        lw = lw_all[:, sl]
        for ci in range(tb // CHUNK):
            rs = slice(ci * CHUNK, (ci + 1) * CHUNK)
            items.append((lw[rs], r[rs], k2[rs], v[rs], a[rs], b[rs]))
            where.append((rs, sl))
    for (rs, sl), outs in zip(where, _rwkv_chunks(items)):
        for ref, val in zip((at_ref, p_ref, tc_ref, bp_ref, kp_ref, gr_ref, rt_ref, qc_ref, yk_ref), outs):
            ref[rs, sl] = val


def _rwkv_pre(z_rkv, z_lora, v_first, mu_rkv, mu_lora, vecs, wd, wa, wg, wv, batch, seq, tb=128):
    has_vres = v_first is not None
    big = lambda w: pl.BlockSpec((None, tb, w), lambda b, t: (b, t, 0))
    full = lambda shape: pl.BlockSpec(shape, lambda b, t: (0,) * len(shape))
    in_specs = [big(3 * WIDTH), big(LORA_COLS)] + ([big(WIDTH)] if has_vres else [])
    in_specs += [full((1, 3 * WIDTH)), full((1, LORA_COLS)), full((8, WIDTH)),
                 full((LANES, WIDTH)), full((LANES, WIDTH)), full((LANES, WIDTH))]
    args = [z_rkv, z_lora] + ([v_first] if has_vres else []) + [mu_rkv, mu_lora, vecs, wd, wa, wg]
    if has_vres:
        in_specs.append(full((LANES, WIDTH)))
        args.append(wv)
    out_dtypes = [BF16, F32, BF16, BF16, BF16, F32, F32, BF16, BF16, F32, F32, F32]
    return pl.pallas_call(
        functools.partial(_rwkv_pre_kernel, has_vres=has_vres, tb=tb),
        grid=(batch, seq // tb),
        in_specs=in_specs,
        out_specs=[big(WIDTH)] * len(out_dtypes),
        out_shape=[jax.ShapeDtypeStruct((batch, seq, WIDTH), dt) for dt in out_dtypes],
        scratch_shapes=[pltpu.VMEM((8, 3 * WIDTH), F32), pltpu.VMEM((8, LORA_COLS), F32)],
        compiler_params=_cparams(("parallel", "arbitrary")),
        name="rwkv_pre",
    )(*args)


def _rwkv_scan_kernel(at_ref, p_ref, tc_ref, bp_ref, kp_ref, v_ref, gr_ref, rt_ref, qc_ref, yk_ref,
                      bonus_ref, g_ref, lnx_ref, o_ref, st_ref, *, ts):
    t_idx = pl.program_id(1)

    @pl.when(t_idx == 0)
    def _():
        st_ref[...] = jnp.zeros_like(st_ref)

    lane = lax.broadcasted_iota(I32, (CHUNK, LANES), 1)
    m0 = lane < HEAD_DIM
    bdmask = (lax.shift_right_logical(lax.broadcasted_iota(I32, (LANES, LANES), 0), 6)
              == lax.shift_right_logical(lax.broadcasted_iota(I32, (LANES, LANES), 1), 6))
    ones_t = jnp.full((CHUNK, LANES), 1.0 / CHUNK, BF16)

    def stack(x):
        return jnp.concatenate([jnp.where(m0, x, 0.0), jnp.where(m0, 0.0, x)], axis=0)

    tn = functools.partial(lax.dot_general, dimension_numbers=TN_DIMS, preferred_element_type=F32)
    nseq = at_ref.shape[0]
    chains = [(bb, pp, slice(pp * LANES, (pp + 1) * LANES)) for bb in range(nseq) for pp in range(N_PAIRS)]
    chunks = [slice(ci * CHUNK, (ci + 1) * CHUNK) for ci in range(ts // CHUNK)]

    gcols, kvs = {}, {}
    for ci, rs in enumerate(chunks):
        for bb, pp, sl in chains:
            gh, gl = _split(gr_ref[bb, rs, sl])
            gcols[ci, bb, pp] = tn(gh, ones_t) + tn(gl, ones_t)
            kvs[ci, bb, pp] = jnp.where(bdmask, _dot1(kp_ref[bb, rs, sl], v_ref[bb, rs, sl], TN_DIMS), 0.0)

    states = [st_ref[bb, pp] for bb, pp, _ in chains]
    for ci, rs in enumerate(chunks):
        sts = [st.astype(BF16) for st in states]
        xs = [_dot1(at_ref[bb, rs, sl], sts[n]) + p_ref[bb, rs, sl] for n, (bb, pp, sl) in enumerate(chains)]
        us = [_dot1(tc_ref[bb, rs, sl], stack(xs[n])).astype(BF16) for n, (bb, pp, sl) in enumerate(chains)]
        new = [gcols[ci, bb, pp] * states[n]
               + (jnp.where(bdmask, _dot1(bp_ref[bb, rs, sl], us[n], TN_DIMS), 0.0) + kvs[ci, bb, pp])
               for n, (bb, pp, sl) in enumerate(chains)]
        for n, (bb, pp, sl) in enumerate(chains):
            lhs = jnp.concatenate([rt_ref[bb, rs, sl], qc_ref[bb, rs, sl]], axis=1)
            rhs = jnp.concatenate([sts[n], stack(us[n])], axis=0)
            o_ref[bb, rs, sl] = jnp.dot(lhs, rhs, preferred_element_type=F32) + yk_ref[bb, rs, sl]
        states = new
    for n, (bb, pp, _) in enumerate(chains):
        st_ref[bb, pp] = states[n]

    bd = _head_block_diag(1.0 / HEAD_DIM)
    for bb, pp, sl in chains:
        y = o_ref[bb, :, sl]
        mu = _dot2x(y, bd)
        d = y - mu
        var = _dot2x(d * d, bd)
        yn = d * lax.rsqrt(var + RWKV_GN_EPS) * lnx_ref[0:1, sl] + lnx_ref[1:2, sl]
        o_ref[bb, :, sl] = (yn + bonus_ref[bb, :, sl]) * g_ref[bb, :, sl]


def _rwkv_scan(pre, lnx, batch, seq, ts=128):
    nseq = next(c for c in (4, 2, 1) if batch % c == 0)
    big = pl.BlockSpec((nseq, ts, WIDTH), lambda b, t: (b, t, 0))
    return pl.pallas_call(
        functools.partial(_rwkv_scan_kernel, ts=ts),
        grid=(batch // nseq, seq // ts),
        in_specs=[big] * 12 + [pl.BlockSpec((8, WIDTH), lambda b, t: (0, 0))],
        out_specs=big,
        out_shape=jax.ShapeDtypeStruct((batch, seq, WIDTH), F32),
        scratch_shapes=[pltpu.VMEM((nseq, N_PAIRS, LANES, LANES), F32)],
        compiler_params=_cparams(("parallel", "arbitrary")),
        name="rwkv_scan",
    )(*pre, lnx)


def _outproj_kernel(x_ref, oa_ref, orw_ref, w_ref, ln_ref, o_ref):
    y = (jnp.dot(oa_ref[...].astype(BF16), w_ref[0:WIDTH, :], preferred_element_type=F32)
         + jnp.dot(orw_ref[...].astype(BF16), w_ref[WIDTH:2 * WIDTH, :], preferred_element_type=F32))
    o_ref[...] = _layer_norm(ALPHA * x_ref[...] + y, ln_ref[0:1, :], ln_ref[1:2, :])


def _outproj(x2d, o_att, o_rwkv, w_out, ln, tm=512):
    n = x2d.shape[0]
    return pl.pallas_call(
        _outproj_kernel,
        grid=(n // tm,),
        in_specs=[pl.BlockSpec((tm, D_MODEL), lambda i: (i, 0)),
                  pl.BlockSpec((tm, WIDTH), lambda i: (i, 0)),
                  pl.BlockSpec((tm, WIDTH), lambda i: (i, 0)),
                  pl.BlockSpec((2 * WIDTH, D_MODEL), lambda i: (0, 0)),
                  pl.BlockSpec((8, D_MODEL), lambda i: (0, 0))],
        out_specs=pl.BlockSpec((tm, D_MODEL), lambda i: (i, 0)),
        out_shape=jax.ShapeDtypeStruct((n, D_MODEL), F32),
        compiler_params=_cparams(("parallel",)),
        name="outproj_ln",
    )(x2d, o_att, o_rwkv, w_out, ln)


def _kvproj_kernel(m_ref, w_ref, o_ref):
    o_ref[...] = jnp.dot(m_ref[...].astype(BF16), w_ref[...], preferred_element_type=F32).astype(BF16)


def _kvproj(mem2d, w_kv, tm=256):
    n = mem2d.shape[0]
    return pl.pallas_call(
        _kvproj_kernel,
        grid=(n // tm,),
        in_specs=[pl.BlockSpec((tm, D_MODEL), lambda i: (i, 0)),
                  pl.BlockSpec((D_MODEL, 2 * D_MODEL), lambda i: (0, 0))],
        out_specs=pl.BlockSpec((tm, 2 * D_MODEL), lambda i: (i, 0)),
        out_shape=jax.ShapeDtypeStruct((n, 2 * D_MODEL), BF16),
        compiler_params=_cparams(("parallel",)),
        name="kvproj",
    )(mem2d, w_kv)


def _route_lanes(logits, b_router):
    lane = lax.broadcasted_iota(I32, logits.shape, 1)
    lane_f = lane.astype(F32)
    valid = lane < N_EXPERTS
    s = _sigmoid(logits)
    sel = jnp.where(valid, s + b_router, NEG)

    def partner(x, bit):
        return jnp.where((lane & bit) == 0, pltpu.roll(x, LANES - bit, 1), pltpu.roll(x, bit, 1))

    p1 = partner(sel, 1)
    hi1, lo1 = jnp.maximum(sel, p1), jnp.minimum(sel, p1)
    hi2, lo2 = partner(hi1, 2), partner(lo1, 2)
    gsum = jnp.maximum(hi1, hi2) + jnp.maximum(jnp.minimum(hi1, hi2), jnp.maximum(lo1, lo2))
    gmax = jnp.max(gsum, -1, keepdims=True)
    grp_f = lax.shift_right_logical(lane, 2).astype(F32)
    g_best = jnp.min(jnp.where((gsum == gmax) & valid, grp_f, 1e9), -1, keepdims=True)
    in_grp = (grp_f == g_best) & valid
    masked = jnp.where(in_grp, sel, NEG)
    v1 = jnp.max(masked, -1, keepdims=True)
    e1 = jnp.min(jnp.where(in_grp & (masked == v1), lane_f, 1e9), -1, keepdims=True)
    rest = in_grp & (lane_f != e1)
    masked2 = jnp.where(rest, sel, NEG)
    v2 = jnp.max(masked2, -1, keepdims=True)
    e2 = jnp.min(jnp.where(rest & (masked2 == v2), lane_f, 1e9), -1, keepdims=True)
    w1 = jnp.sum(jnp.where(lane_f == e1, s, 0.0), -1, keepdims=True)
    w2 = jnp.sum(jnp.where(lane_f == e2, s, 0.0), -1, keepdims=True)
    tot = w1 + w2
    out = jnp.where(lane == 0, e1, jnp.where(lane == 1, e2, jnp.where(lane == 2, w1 / tot, w2 / tot)))
    return jnp.where(lane < 4, out, 0.0)


def _memattn_kernel(x_ref, kv_ref, wq_ref, wo_ref, ln_ref, wr_ref, br_ref, o_ref, route_ref):
    x = x_ref[...]
    q = jnp.dot(x.astype(BF16), wq_ref[...], preferred_element_type=F32) * (MEM_HEAD_DIM ** -0.5)
    heads = []
    for h in range(N_MEM_HEADS):
        sl = slice(h * MEM_HEAD_DIM, (h + 1) * MEM_HEAD_DIM)
        kh = kv_ref[:, sl]
        vh = kv_ref[:, D_MODEL + h * MEM_HEAD_DIM:D_MODEL + (h + 1) * MEM_HEAD_DIM]
        s = lax.dot_general(q[:, sl].astype(BF16), kh, NT_DIMS, preferred_element_type=F32)
        p = jnp.exp(s - jnp.max(s, -1, keepdims=True))
        o = jnp.dot(p.astype(BF16), vh, preferred_element_type=F32)
        heads.append(o / jnp.sum(p, -1, keepdims=True))
    o = jnp.concatenate(heads, axis=-1).astype(BF16)
    y = jnp.dot(o, wo_ref[...], preferred_element_type=F32)
    x2 = _layer_norm(ALPHA * x + y, ln_ref[0:1, :], ln_ref[1:2, :])
    o_ref[...] = x2
    logits = _dot1(x2, wr_ref[...])
    route_ref[...] = _route_lanes(logits, br_ref[0:1, :])


def _memattn(x3d, kv3d, wq, wo, ln, w_router_pad, b_router_pad, tm=512):
    batch, seq, _ = x3d.shape
    mlen = kv3d.shape[1]
    full = lambda shape: pl.BlockSpec(shape, lambda b, t: (0,) * len(shape))
    return pl.pallas_call(
        _memattn_kernel,
        grid=(batch, seq // tm),
        in_specs=[pl.BlockSpec((None, tm, D_MODEL), lambda b, t: (b, t, 0)),
                  pl.BlockSpec((None, mlen, 2 * D_MODEL), lambda b, t: (b, 0, 0)),
                  full((D_MODEL, D_MODEL)), full((D_MODEL, D_MODEL)), full((8, D_MODEL)),
                  full((D_MODEL, LANES)), full((8, LANES))],
        out_specs=[pl.BlockSpec((None, tm, D_MODEL), lambda b, t: (b, t, 0)),
                   pl.BlockSpec((None, tm, LANES), lambda b, t: (b, t, 0))],
        out_shape=[jax.ShapeDtypeStruct((batch, seq, D_MODEL), F32),
                   jax.ShapeDtypeStruct((batch, seq, LANES), F32)],
        compiler_params=_cparams(("parallel", "parallel")),
        name="memattn_ln",
    )(x3d, kv3d, wq, wo, ln, w_router_pad, b_router_pad)


def _plan_kernel(route_ref, dest_ref, stat_ref, acc_ref):
    phase = pl.program_id(0)
    i = pl.program_id(1)
    tm = route_ref.shape[0]
    lane = lax.broadcasted_iota(I32, (tm, LANES), 1).astype(F32)
    e0 = route_ref[:, 0:1]
    e1 = route_ref[:, 1:2]
    hot = jnp.where((lane == e0) | (lane == e1), 1.0, 0.0)

    @pl.when((phase == 0) & (i == 0))
    def _():
        acc_ref[...] = jnp.zeros_like(acc_ref)

    @pl.when(phase == 0)
    def _():
        acc_ref[0:1, :] = acc_ref[0:1, :] + jnp.sum(hot, axis=0, keepdims=True)

    @pl.when((phase == 1) & (i == 0))
    def _():
        r = lax.broadcasted_iota(I32, (LANES, LANES), 0)
        c = lax.broadcasted_iota(I32, (LANES, LANES), 1)
        before = jnp.where(r < c, 1.0, 0.0)
        cnt8 = jnp.broadcast_to(acc_ref[0:1, :], (8, LANES))
        acc_ref[1:2, :] = _dot2x(cnt8, before.astype(BF16))[0:1, :]

    @pl.when(phase == 1)
    def _():
        r = lax.broadcasted_iota(I32, (tm, tm), 0)
        c = lax.broadcasted_iota(I32, (tm, tm), 1)
        earlier = jnp.where(c < r, 1.0, 0.0).astype(BF16)
        pos = (jnp.dot(earlier, hot.astype(BF16), preferred_element_type=F32)
               + acc_ref[1:2, :] + acc_ref[2:3, :])
        d0 = jnp.sum(jnp.where(lane == e0, pos, 0.0), -1, keepdims=True)
        d1 = jnp.sum(jnp.where(lane == e1, pos, 0.0), -1, keepdims=True)
        dest_ref[...] = jnp.where(lane == 0.0, d0, jnp.where(lane == 1.0, d1, 0.0)).astype(I32)
        acc_ref[2:3, :] = acc_ref[2:3, :] + jnp.sum(hot, axis=0, keepdims=True)
        stat_ref[...] = acc_ref[...]


def _plan(route, tm=1024):
    n = route.shape[0]
    tm = min(tm, n)
    return pl.pallas_call(
        _plan_kernel,
        grid=(2, n // tm),
        in_specs=[pl.BlockSpec((tm, LANES), lambda p, i: (i, 0))],
        out_specs=[pl.BlockSpec((tm, LANES), lambda p, i: (i * p, 0)),
                   pl.BlockSpec((8, LANES), lambda p, i: (0, 0))],
        out_shape=[jax.ShapeDtypeStruct((n, LANES), I32), jax.ShapeDtypeStruct((8, LANES), F32)],
        scratch_shapes=[pltpu.VMEM((8, LANES), F32)],
        compiler_params=_cparams(("arbitrary", "arbitrary")),
        name="moe_plan",
    )(route)


def _dispatch_kernel(dest_ref, prev_ref, x_ref, xs_hbm, stage_ref, sems):
    i = pl.program_id(0)
    tm = dest_ref.shape[2] // 2
    stage_ref[i % 2] = x_ref[...]

    def copy(step, dref, t, slot):
        src = stage_ref.at[step % 2, pl.ds(t, 1)]
        return pltpu.make_async_copy(src, xs_hbm.at[pl.ds(dref[0, 0, slot * tm + t], 1)], sems.at[step % 2])

    def start(t, c):
        copy(i, dest_ref, t, 0).start()
        copy(i, dest_ref, t, 1).start()
        return c

    def wait_step(step, dref):
        def wait(t, c):
            copy(step, dref, t, 0).wait()
            copy(step, dref, t, 1).wait()
            return c
        lax.fori_loop(0, tm, wait, 0, unroll=8)

    lax.fori_loop(0, tm, start, 0, unroll=8)

    @pl.when(i > 0)
    def _():
        wait_step(i - 1, prev_ref)

    @pl.when(i == pl.num_programs(0) - 1)
    def _():
        wait_step(i, dest_ref)


def _dispatch(x2d, dest3, n_rows):
    n_steps = dest3.shape[0]
    tm = dest3.shape[2] // 2
    blk = (1, 1, 2 * tm)
    return pl.pallas_call(
        _dispatch_kernel,
        grid=(n_steps,),
        in_specs=[pl.BlockSpec(blk, lambda i: (i, 0, 0), memory_space=pltpu.SMEM),
                  pl.BlockSpec(blk, lambda i: (jnp.maximum(i - 1, 0), 0, 0), memory_space=pltpu.SMEM),
                  pl.BlockSpec((tm, D_MODEL), lambda i: (i, 0))],
        out_specs=pl.BlockSpec(memory_space=pl.ANY),
        out_shape=jax.ShapeDtypeStruct((n_rows, D_MODEL), F32),
        scratch_shapes=[pltpu.VMEM((2, tm, D_MODEL), F32), pltpu.SemaphoreType.DMA((2,))],
        compiler_params=_cparams(("arbitrary",)),
        name="moe_dispatch",
    )(dest3, dest3, x2d)


def _experts_kernel(rb_ref, ex_ref, nit_ref, off_ref, cnt_ref, x_ref, wg_ref, wu_ref, wd_ref, o_ref,
                    wgb_ref, wub_ref, wdb_ref):
    w = pl.program_id(0)
    rows = x_ref.shape[0]

    @pl.when(w < nit_ref[0])
    def _():
        e = ex_ref[w]
        rb = rb_ref[w]
        prev = jnp.maximum(w - 1, 0)

        @pl.when((w == 0) | (ex_ref[prev] != e))
        def _():
            wgb_ref[...] = wg_ref[...].astype(BF16)
            wub_ref[...] = wu_ref[...].astype(BF16)
            wdb_ref[...] = wd_ref[...].astype(BF16)

        xb = x_ref[...].astype(BF16)
        hg = jnp.dot(xb, wgb_ref[...], preferred_element_type=F32)
        hu = jnp.dot(xb, wub_ref[...], preferred_element_type=F32)
        h = hg * _sigmoid(hg) * hu
        y = jnp.dot(h.astype(BF16), wdb_ref[...], preferred_element_type=F32)
        row = rb * rows + lax.broadcasted_iota(I32, (rows, 1), 0)
        mine = (row >= off_ref[e]) & (row < off_ref[e] + cnt_ref[e])
        first = (w == 0) | (rb_ref[prev] != rb)

        @pl.when(first)
        def _():
            o_ref[...] = jnp.where(mine, y, 0.0)

        @pl.when(jnp.logical_not(first))
        def _():
            o_ref[...] = jnp.where(mine, y, o_ref[...])


def _experts(xs, rb, ex, n_items, off, cnt, w_gate, w_up, w_down, layer, rows):
    n_work = rb.shape[0]
    item = lambda w, nit: jnp.minimum(w, nit[0] - 1)
    row_map = lambda w, rb, ex, nit, off, cnt: (rb[item(w, nit)], 0)
    w_map = lambda w, rb, ex, nit, off, cnt: (layer, ex[item(w, nit)], 0, 0)
    grid_spec = pltpu.PrefetchScalarGridSpec(
        num_scalar_prefetch=5,
        grid=(n_work,),
        in_specs=[pl.BlockSpec((rows, D_MODEL), row_map),
                  pl.BlockSpec((None, None, D_MODEL, D_EXPERT), w_map),
                  pl.BlockSpec((None, None, D_MODEL, D_EXPERT), w_map),
                  pl.BlockSpec((None, None, D_EXPERT, D_MODEL), w_map)],
        out_specs=pl.BlockSpec((rows, D_MODEL), row_map),
        scratch_shapes=[pltpu.VMEM((D_MODEL, D_EXPERT), BF16), pltpu.VMEM((D_MODEL, D_EXPERT), BF16),
                        pltpu.VMEM((D_EXPERT, D_MODEL), BF16)],
    )
    return pl.pallas_call(
        _experts_kernel,
        grid_spec=grid_spec,
        out_shape=jax.ShapeDtypeStruct(xs.shape, F32),
        compiler_params=_cparams(("arbitrary",)),
        name="moe_experts",
    )(rb, ex, n_items, off, cnt, xs, w_gate, w_up, w_down)


def _combine_kernel(dest_ref, next_ref, x_ref, gate_ref, ln_ref, y_hbm, o_ref, buf_ref, sems):
    i = pl.program_id(0)
    tm = x_ref.shape[0]

    def copy(step, dref, slot, t):
        src = y_hbm.at[pl.ds(dref[0, 0, slot * tm + t], 1)]
        return pltpu.make_async_copy(src, buf_ref.at[step % 2, slot, pl.ds(t, 1)], sems.at[step % 2])

    def start_step(step, dref):
        def start(t, c):
            copy(step, dref, 0, t).start()
            copy(step, dref, 1, t).start()
            return c
        lax.fori_loop(0, tm, start, 0, unroll=8)

    @pl.when(i == 0)
    def _():
        start_step(0, dest_ref)

    @pl.when(i + 1 < pl.num_programs(0))
    def _():
        start_step(i + 1, next_ref)

    def wait(t, c):
        copy(i, dest_ref, 0, t).wait()
        copy(i, dest_ref, 1, t).wait()
        return c

    lax.fori_loop(0, tm, wait, 0, unroll=8)
    cur = i % 2
    y = gate_ref[:, 0:1] * buf_ref[cur, 0] + gate_ref[:, 1:2] * buf_ref[cur, 1]
    o_ref[...] = _layer_norm(ALPHA * x_ref[...] + y, ln_ref[0:1, :], ln_ref[1:2, :])


def _combine(x2d, gates, dest3, ys, ln, tm=256):
    n = x2d.shape[0]
    n_steps = n // tm
    blk = (1, 1, 2 * tm)
    return pl.pallas_call(
        _combine_kernel,
        grid=(n_steps,),
        in_specs=[pl.BlockSpec(blk, lambda i: (i, 0, 0), memory_space=pltpu.SMEM),
                  pl.BlockSpec(blk, lambda i: (jnp.minimum(i + 1, n_steps - 1), 0, 0), memory_space=pltpu.SMEM),
                  pl.BlockSpec((tm, D_MODEL), lambda i: (i, 0)),
                  pl.BlockSpec((tm, 2), lambda i: (i, 0)),
                  pl.BlockSpec((8, D_MODEL), lambda i: (0, 0)),
                  pl.BlockSpec(memory_space=pl.ANY)],
        out_specs=pl.BlockSpec((tm, D_MODEL), lambda i: (i, 0)),
        out_shape=jax.ShapeDtypeStruct((n, D_MODEL), F32),
        scratch_shapes=[pltpu.VMEM((2, 2, tm, D_MODEL), F32), pltpu.SemaphoreType.DMA((2,))],
        compiler_params=_cparams(("arbitrary",)),
        name="moe_combine_ln",
    )(dest3, dest3, x2d, gates, ln, ys)


def _work_items(counts, off, n_rows, rows):
    n_work = n_rows // rows + N_EXPERTS
    first = off // rows
    last = (off + counts - 1) // rows
    per_expert = jnp.where(counts > 0, last - first + 1, 0)
    item_end = jnp.cumsum(per_expert)
    item_start = item_end - per_expert
    w = jnp.arange(n_work, dtype=I32)
    ex = jnp.minimum(jnp.sum((item_end[None, :] <= w[:, None]).astype(I32), axis=1), N_EXPERTS - 1)
    rb = jnp.clip(first[ex] + (w - item_start[ex]), 0, n_rows // rows - 1).astype(I32)
    return rb, ex, item_end[-1:].astype(I32)


def _moe(x2, route, w_gate, w_up, w_down, layer, ln, rows=MOE_ROWS, tm=256):
    n = x2.shape[0]
    n_rows = 2 * n
    gates = route[:, 2:4]
    dest_tile, stat = _plan(route)
    counts = stat[0, :N_EXPERTS].astype(I32)
    off = stat[1, :N_EXPERTS].astype(I32)
    rb, ex, n_items = _work_items(counts, off, n_rows, rows)
    dest3 = dest_tile[:, 0:2].reshape(n // tm, tm, 2).transpose(0, 2, 1).reshape(n // tm, 1, 2 * tm)
    xs = _dispatch(x2, dest3, n_rows)
    ys = _experts(xs, rb, ex, n_items, off, counts, w_gate, w_up, w_down, layer, rows)
    return _combine(x2, gates, dest3, ys, ln, tm)


def _pad_rows(w, rows_before, total):
    return jnp.pad(w, ((rows_before, total - rows_before - w.shape[0]), (0, 0)))


def kernel(x, mem, w_in, w_in_vres, mu_shift, mu_vres, w0, w_decay_up, a0, w_iclr_up, v0, w_vres_up,
           w_gate_up, k_k, k_a, r_k, lnx_g, lnx_b, attn_out_g, w_out, w_mem_q, w_mem_kv, w_mem_o,
           w_router, b_router, w_exp_gate, w_exp_up, w_exp_down, ln_g, ln_b):
    batch, seq, d = x.shape
    n = batch * seq
    mlen = mem.shape[1]
    zeros = jnp.zeros
    slopes = 2.0 ** (-8.0 * jnp.arange(1, N_HEADS + 1, dtype=F32) / N_HEADS)
    w_router_pad = jnp.pad(w_router, ((0, 0), (0, LANES - N_EXPERTS)))
    b_router_pad = jnp.pad(b_router.reshape(1, N_EXPERTS), ((0, 7), (0, LANES - N_EXPERTS)))
    mem2d = mem.reshape(batch * mlen, d)
    n_shift = w_in.shape[2] - 3 * WIDTH
    n_lora = n_shift - 3 * WIDTH
    n_vres = w_in_vres.shape[2]
    x2d = x.reshape(n, d)
    v_first = None
    for l in range(DEPTH):
        has_vres = l > 0
        w_extra = w_in_vres[l - 1] if has_vres else zeros((d, n_vres), F32)
        w_pad = jnp.concatenate(
            [w_in[l], w_extra, zeros((d, IN_COLS_PAD - w_in.shape[2] - n_vres), F32)], axis=1).astype(BF16)
        z_att, z_rkv, z_lora = _inproj(x2d, w_pad)
        o_att = _moba(z_att.reshape(batch, seq, 3 * WIDTH), slopes, attn_out_g[l].reshape(1, WIDTH), batch, seq)

        mu_rkv = mu_shift[l, :3 * WIDTH].reshape(1, 3 * WIDTH)
        mu_extra = mu_vres[l - 1] if has_vres else zeros((n_vres,), F32)
        mu_lora = jnp.concatenate(
            [mu_shift[l, 3 * WIDTH:], mu_extra, zeros((LORA_COLS - n_lora - n_vres,), F32)]).reshape(1, LORA_COLS)
        v0_l = v0[l - 1] if has_vres else zeros((WIDTH,), F32)
        vecs = jnp.stack([w0[l], a0[l], v0_l, k_k[l], k_a[l], r_k[l].reshape(WIDTH),
                          zeros((WIDTH,), F32), zeros((WIDTH,), F32)])
        n_dec, n_iclr = w_decay_up.shape[1], w_iclr_up.shape[1]
        wd = _pad_rows(w_decay_up[l], 0, LANES).astype(BF16)
        wa = _pad_rows(w_iclr_up[l], n_dec, LANES).astype(BF16)
        wg = w_gate_up[l].astype(BF16)
        wv = _pad_rows(w_vres_up[l - 1], 0, LANES).astype(BF16) if has_vres else None
        pre = _rwkv_pre(z_rkv.reshape(batch, seq, 3 * WIDTH), z_lora.reshape(batch, seq, LORA_COLS),
                        v_first, mu_rkv, mu_lora, vecs, wd, wa, wg, wv, batch, seq)
        if not has_vres:
            v_first = pre[5]
        lnx = jnp.concatenate([lnx_g[l][None], lnx_b[l][None], zeros((6, WIDTH), F32)])
        o_rwkv = _rwkv_scan(pre, lnx, batch, seq)

        ln = lambda j: jnp.concatenate([ln_g[l, j][None], ln_b[l, j][None], zeros((6, d), F32)])
        x1 = _outproj(x2d, o_att.reshape(n, WIDTH), o_rwkv.reshape(n, WIDTH), w_out[l].astype(BF16), ln(0))

        kv = _kvproj(mem2d, w_mem_kv[l].astype(BF16)).reshape(batch, mlen, 2 * d)
        x2, route = _memattn(x1.reshape(batch, seq, d), kv, w_mem_q[l].astype(BF16),
                             w_mem_o[l].astype(BF16), ln(1), w_router_pad, b_router_pad)
        x2d = _moe(x2.reshape(n, d), route.reshape(n, LANES), w_exp_gate, w_exp_up, w_exp_down, l, ln(2))
    return x2d.reshape(batch, seq, d)
```
